```python
import math
import jax, jax.numpy as jnp
from jax import lax
import numpy as np

D_MODEL = 1024
BATCH = 1
SEQ = 16384
DEPTH = 2
DEC_BATCH = 32
DEC_SEQ = 8
PAST_LEN = 16384
PAGE_SIZE = 128

N_EVEN = (DEPTH + 1) // 2
N_ODD = DEPTH // 2
A_HEADS = 8
A_HEAD_DIM = 64
A_W = A_HEADS * A_HEAD_DIM
IDX_HEADS = 8
IDX_DIM = 64
TOPK_MAX = 256
Q_BLOCK = 128
REL_BUCKETS = 32
REL_MAX_DIST = 128
B_HEADS = 8
B_DK = 64
B_DV = 64
B_W = B_HEADS * B_DK
GLA_CHUNK = 64
MIX_W = A_W + B_HEADS * B_DV
IN_SPLITS = (A_W, A_W, A_W, IDX_HEADS * IDX_DIM, IDX_DIM, IDX_HEADS, B_W, B_W, B_HEADS * B_DV, B_HEADS * B_DV)
IN_W = sum(IN_SPLITS)
S5_GROUP_CH = 16
S5_GROUPS = D_MODEL // S5_GROUP_CH
S5_STATE = 64
S5_BLOCK = 1024
D_FF = 2816
N_EXPERTS = 8
TOP_K_EXPERTS = 2
MOE_FF = 1408
NORM_EPS = 1e-6

kernel_name = 'hybrid_dsa_hgrn2_s5_moe_step'


def rmsnorm(x, g):
    xf = x.astype(jnp.float32)
    y = xf * lax.rsqrt(jnp.mean(xf * xf, axis=-1, keepdims=True) + NORM_EPS)
    return (y * g.astype(jnp.float32)).astype(x.dtype)


def ada_modulation(c, w, b):
    m = jax.nn.silu(c) @ w + b
    m = m.reshape(c.shape[0], 6, 1, D_MODEL)
    return tuple(m[:, i] for i in range(6))


def t5_bucket(dist):
    n = jnp.maximum(dist, 0)
    max_exact = REL_BUCKETS // 2
    nf = jnp.maximum(n, 1).astype(jnp.float32)
    large = max_exact + (jnp.log(nf / max_exact) / math.log(REL_MAX_DIST / max_exact)
                         * (REL_BUCKETS - max_exact)).astype(jnp.int32)
    large = jnp.minimum(large, REL_BUCKETS - 1)
    return jnp.where(n < max_exact, n, large)


def gather_rows(x, idx):
    return jax.vmap(lambda xb, ib: xb[ib])(x, idx)


def indexer_topk(qi, wi, ki, q_pos, topk):
    rel = jax.nn.relu(jnp.einsum('bthd,bsd->bths', qi, ki).astype(jnp.float32))
    score = jnp.einsum('bths,bth->bts', rel, wi.astype(jnp.float32))
    causal = jnp.arange(ki.shape[1])[None, :] <= q_pos[:, None]
    score = jnp.where(causal[None], score, -jnp.inf)
    _, idx = lax.top_k(score, topk)
    valid = idx <= q_pos[None, :, None]
    return idx, valid


def sparse_attention(q, k_sel, v_sel, valid, dist, rel_bias):
    logits = jnp.einsum('bthd,btkhd->bthk', q, k_sel).astype(jnp.float32) * (A_HEAD_DIM ** -0.5)
    bias = rel_bias[t5_bucket(dist)]
    logits = logits + jnp.moveaxis(bias, -1, 2).astype(jnp.float32)
    logits = jnp.where(valid[:, :, None, :], logits, -jnp.inf)
    p = jax.nn.softmax(logits, axis=-1)
    return jnp.einsum('bthk,btkhd->bthd', p.astype(v_sel.dtype), v_sel)


def dsa_prompt(q, k, v, qi, ki, wi, rel_bias):
    B, T = q.shape[0], q.shape[1]
    topk = min(TOPK_MAX, T // 4)
    qb = math.gcd(T, Q_BLOCK)
    nb = T // qb

    def to_blocks(a):
        return jnp.moveaxis(a.reshape(B, nb, qb, *a.shape[2:]), 1, 0)

    pos = jnp.arange(T, dtype=jnp.int32).reshape(nb, qb)

    def one_block(args):
        q_b, qi_b, wi_b, p_b = args
        idx, valid = indexer_topk(qi_b, wi_b, ki, p_b, topk)
        dist = p_b[None, :, None] - idx
        return sparse_attention(q_b, gather_rows(k, idx), gather_rows(v, idx), valid, dist, rel_bias)

    out = lax.map(one_block, (to_blocks(q), to_blocks(qi), to_blocks(wi), pos))
    return jnp.moveaxis(out, 0, 1).reshape(B, T, A_HEADS, A_HEAD_DIM)


def dsa_sample(q, k_new, v_new, qi, ki_new, wi, cache_k, cache_v, cache_ik, page_table, rel_bias):
    Bd, T = q.shape[0], q.shape[1]
    past = page_table.shape[1] * PAGE_SIZE
    topk = min(TOPK_MAX, (past + T) // 4)
    ki_past = cache_ik[page_table].reshape(Bd, past, IDX_DIM).astype(ki_new.dtype)
    ki_all = jnp.concatenate([ki_past, ki_new], axis=1)
    q_pos = past + jnp.arange(T, dtype=jnp.int32)
    idx, valid = indexer_topk(qi, wi, ki_all, q_pos, topk)
    is_new = idx >= past
    pidx = jnp.minimum(idx, past - 1)
    phys = jnp.take_along_axis(page_table, (pidx // PAGE_SIZE).reshape(Bd, -1), axis=1).reshape(idx.shape)
    off = pidx % PAGE_SIZE
    nidx = jnp.clip(idx - past, 0, T - 1)
    k_sel = jnp.where(is_new[..., None, None], gather_rows(k_new, nidx), cache_k[phys, off].astype(k_new.dtype))
    v_sel = jnp.where(is_new[..., None, None], gather_rows(v_new, nidx), cache_v[phys, off].astype(v_new.dtype))
    dist = q_pos[None, :, None] - idx
    return sparse_attention(q, k_sel, v_sel, valid, dist, rel_bias)


def gla_chunked(q, k, v, log_f, s0):
    B, T = q.shape[0], q.shape[1]
    C = math.gcd(T, GLA_CHUNK)
    n = T // C

    def blocks(a):
        return jnp.moveaxis(a.astype(jnp.float32).reshape(B, n, C, *a.shape[2:]), 1, 0)

    tri = jnp.tril(jnp.ones((C, C), dtype=bool))

    def step(s, inp):
        qc, kc, vc, gc = inp
        b = jnp.cumsum(gc, axis=1)
        o_inter = jnp.einsum('bchk,bhkv->bchv', qc * jnp.exp(b), s)
        diff = b[:, :, None] - b[:, None, :]
        dec = jnp.where(tri[None, :, :, None, None], jnp.exp(jnp.minimum(diff, 0.0)), 0.0)
        att = jnp.einsum('bthk,btshk,bshk->bhts', qc, dec, kc)
        o_intra = jnp.einsum('bhts,bshv->bthv', att, vc)
        b_last = b[:, -1]
        s_new = s * jnp.exp(b_last)[..., None] + jnp.einsum('bshk,bshv->bhkv', kc * jnp.exp(b_last[:, None] - b), vc)
        return s_new, o_inter + o_intra

    s_fin, o = lax.scan(step, s0.astype(jnp.float32), (blocks(q), blocks(k), blocks(v), blocks(log_f)))
    return jnp.moveaxis(o, 0, 1).reshape(B, T, v.shape[2], v.shape[3]), s_fin


def hgrn2_mixer(q, f_raw, i, g, lb, onorm_g, s0):
    B, T = q.shape[0], q.shape[1]

    def heads(a, d):
        return a.reshape(B, T, B_HEADS, d)

    f = lb + (1.0 - lb) * jax.nn.sigmoid(f_raw.astype(jnp.float32))
    o, s_fin = gla_chunked(heads(q.astype(jnp.float32) * (B_DK ** -0.5), B_DK), heads(1.0 - f, B_DK),
                           heads(i, B_DV), heads(jnp.log(f), B_DK), s0)
    o = rmsnorm(o, onorm_g) * jax.nn.silu(heads(g, B_DV).astype(jnp.float32))
    return o.reshape(B, T, B_HEADS * B_DV), s_fin


def s5_scan(u, lam_re, lam_im, log_dt, b_re, b_im, c_re, c_im, d_skip, h0_re, h0_im):
    B, T = u.shape[0], u.shape[1]
    tb = math.gcd(T, S5_BLOCK)
    n = T // tb
    lr = jnp.minimum(lam_re.astype(jnp.float32), -1e-4)
    li = lam_im.astype(jnp.float32)
    dt = jnp.exp(log_dt.astype(jnp.float32))[:, None]
    mag = jnp.exp(lr * dt)
    a_re = mag * jnp.cos(li * dt)
    a_im = mag * jnp.sin(li * dt)
    den = lr * lr + li * li
    nr = a_re - 1.0
    coef_re = (nr * lr + a_im * li) / den
    coef_im = (a_im * lr - nr * li) / den
    br = b_re.astype(jnp.float32)
    bi = b_im.astype(jnp.float32)
    bb_re = coef_re[..., None] * br - coef_im[..., None] * bi
    bb_im = coef_re[..., None] * bi + coef_im[..., None] * br
    cr = c_re.astype(jnp.float32)
    ci = c_im.astype(jnp.float32)
    dg = d_skip.astype(jnp.float32).reshape(S5_GROUPS, S5_GROUP_CH)
    ug = jnp.moveaxis(u.astype(jnp.float32).reshape(B, n, tb, S5_GROUPS, S5_GROUP_CH), 1, 0)

    def combine(e1, e2):
        a1r, a1i, b1r, b1i = e1
        a2r, a2i, b2r, b2i = e2
        return (a2r * a1r - a2i * a1i, a2r * a1i + a2i * a1r,
                a2r * b1r - a2i * b1i + b2r, a2r * b1i + a2i * b1r + b2i)

    def step(carry, ub):
        hr, hi = carry
        xr = jnp.einsum('gpc,btgc->btgp', bb_re, ub)
        xi = jnp.einsum('gpc,btgc->btgp', bb_im, ub)
        xr = xr.at[:, 0].add(a_re * hr - a_im * hi)
        xi = xi.at[:, 0].add(a_re * hi + a_im * hr)
        ar_b = jnp.broadcast_to(a_re, xr.shape)
        ai_b = jnp.broadcast_to(a_im, xr.shape)
        _, _, sr, si = lax.associative_scan(combine, (ar_b, ai_b, xr, xi), axis=1)
        y = jnp.einsum('gcp,btgp->btgc', cr, sr) - jnp.einsum('gcp,btgp->btgc', ci, si) + dg * ub
        return (sr[:, -1], si[:, -1]), y

    (hr, hi), y = lax.scan(step, (h0_re.astype(jnp.float32), h0_im.astype(jnp.float32)), ug)
    return jnp.moveaxis(y, 0, 1).reshape(B, T, D_MODEL), hr, hi


def swiglu(h, w1, w3, w2):
    return (jax.nn.silu(h @ w1) * (h @ w3)) @ w2


def moe_swiglu(h, rw, rb, w1, w3, w2):
    logits = (h @ rw + rb).astype(jnp.float32)
    top_v, top_i = lax.top_k(logits, TOP_K_EXPERTS)
    top_w = jax.nn.softmax(top_v, axis=-1)
    gates = jnp.einsum('btk,btke->bte', top_w, jax.nn.one_hot(top_i, N_EXPERTS, dtype=jnp.float32))
    out = jnp.zeros(h.shape[:-1] + (D_MODEL,), jnp.float32)
    for e in range(N_EXPERTS):
        out = out + gates[..., e:e + 1] * swiglu(h, w1[e], w3[e], w2[e]).astype(jnp.float32)
    return out


def even_mixer(h, p, j, lb, attend, s0):
    B, T = h.shape[0], h.shape[1]
    proj = h @ p['w_in'][j]
    offs = np.cumsum(IN_SPLITS)[:-1].tolist()
    q, k, v, qi, ki, wi, bq, bf, bi, bg = jnp.split(proj, offs, axis=-1)
    q = rmsnorm(q.reshape(B, T, A_HEADS, A_HEAD_DIM), p['qk_norm_g'][j, 0])
    k = rmsnorm(k.reshape(B, T, A_HEADS, A_HEAD_DIM), p['qk_norm_g'][j, 1])
    v = v.reshape(B, T, A_HEADS, A_HEAD_DIM)
    qi = qi.reshape(B, T, IDX_HEADS, IDX_DIM) * (IDX_DIM ** -0.5)
    wi = wi * (IDX_HEADS ** -0.5)
    a_out = attend(j, q, k, v, qi, ki, wi)
    b_out, s_fin = hgrn2_mixer(bq, bf, bi, bg, lb, p['hgrn_onorm_g'][j], s0)
    merged = jnp.concatenate([a_out.reshape(B, T, A_W).astype(jnp.float32), b_out], axis=-1)
    return merged @ p['w_out'][j], k, v, ki, s_fin


def odd_mixer(h, p, j, h0_re, h0_im):
    y, hr, hi = s5_scan(h, p['s5_lambda_re'][j], p['s5_lambda_im'][j], p['s5_log_dt'][j],
                        p['s5_b_re'][j], p['s5_b_im'][j], p['s5_c_re'][j], p['s5_c_im'][j],
                        p['s5_d'][j], h0_re, h0_im)
    z = jax.nn.gelu(y) @ p['s5_glu_w'][j]
    return z[..., :D_MODEL] * jax.nn.sigmoid(z[..., D_MODEL:]), hr, hi


def trunk(x, c, p, attend, hgrn_s0, s5_h0_re, s5_h0_im):
    lbs = jnp.cumsum(jax.nn.softmax(p['hgrn_gamma'].astype(jnp.float32), axis=0), axis=0)
    ks, vs, iks, hs, srs, sis = [], [], [], [], [], []
    for layer in range(DEPTH):
        j = layer // 2
        sh1, sc1, g1, sh2, sc2, g2 = ada_modulation(c, p['ada_w'][layer], p['ada_b'][layer])
        h = rmsnorm(x, p['norm_g'][layer, 0]) * (1.0 + sc1) + sh1
        if layer % 2 == 0:
            mix, k, v, ik, s_fin = even_mixer(h, p, j, lbs[j], attend, hgrn_s0[j])
            ks.append(k)
            vs.append(v)
            iks.append(ik)
            hs.append(s_fin)
        else:
            mix, sr, si = odd_mixer(h, p, j, s5_h0_re[j], s5_h0_im[j])
            srs.append(sr)
            sis.append(si)
        x = x + g1 * mix
        h = rmsnorm(x, p['norm_g'][layer, 1]) * (1.0 + sc2) + sh2
        if layer % 2 == 0:
            ff = swiglu(h, p['ffn_w1'][j], p['ffn_w3'][j], p['ffn_w2'][j])
        else:
            ff = moe_swiglu(h, p['moe_router_w'][j], p['moe_router_b'][j],
                            p['moe_w1'][j], p['moe_w3'][j], p['moe_w2'][j])
        x = x + g2 * ff
    return x, jnp.stack(ks), jnp.stack(vs), jnp.stack(iks), jnp.stack(hs), jnp.stack(srs), jnp.stack(sis)


def setup_inputs(seed: int = 0) -> dict:
    key = jax.random.key(seed)
    keys = iter(jax.random.split(key, 64))
    f32 = jnp.float32

    def nrm(shape, scale=1.0):
        return jax.random.normal(next(keys), shape, f32) * scale

    n_pages = PAST_LEN // PAGE_SIZE
    used = DEC_BATCH * n_pages
    n_phys = used + max(1, used // 4)
    page_table = jax.random.permutation(next(keys), n_phys)[:used].reshape(DEC_BATCH, n_pages).astype(jnp.int32)
    s5_shape = (N_ODD, S5_GROUPS, S5_STATE)
    return {
        'x_prompt': nrm((BATCH, SEQ, D_MODEL)),
        'x_sample': nrm((DEC_BATCH, DEC_SEQ, D_MODEL)),
        'c_prompt': nrm((BATCH, D_MODEL)),
        'c_sample': nrm((DEC_BATCH, D_MODEL)),
        'cache_k': nrm((N_EVEN, n_phys, PAGE_SIZE, A_HEADS, A_HEAD_DIM)),
        'cache_v': nrm((N_EVEN, n_phys, PAGE_SIZE, A_HEADS, A_HEAD_DIM)),
        'cache_idx_k': nrm((N_EVEN, n_phys, PAGE_SIZE, IDX_DIM)),
        'state_hgrn': nrm((N_EVEN, DEC_BATCH, B_HEADS, B_DK, B_DV), 0.5),
        'state_s5_re': nrm((N_ODD, DEC_BATCH, S5_GROUPS, S5_STATE), 0.1),
        'state_s5_im': nrm((N_ODD, DEC_BATCH, S5_GROUPS, S5_STATE), 0.1),
        'page_table': page_table,
        'rel_bias': nrm((REL_BUCKETS, A_HEADS), 0.5),
        'ada_w': nrm((DEPTH, D_MODEL, 6 * D_MODEL), 0.02),
        'ada_b': nrm((DEPTH, 6 * D_MODEL), 0.02),
        'norm_g': 1.0 + nrm((DEPTH, 2, D_MODEL), 0.05),
        'w_in': nrm((N_EVEN, D_MODEL, IN_W), D_MODEL ** -0.5),
        'qk_norm_g': 1.0 + nrm((N_EVEN, 2, A_HEAD_DIM), 0.05),
        'hgrn_gamma': nrm((N_EVEN + 1, B_W), 0.1),
        'hgrn_onorm_g': 1.0 + nrm((N_EVEN, B_DV), 0.05),
        'w_out': nrm((N_EVEN, MIX_W, D_MODEL), MIX_W ** -0.5),
        'ffn_w1': nrm((N_EVEN, D_MODEL, D_FF), D_MODEL ** -0.5),
        'ffn_w3': nrm((N_EVEN, D_MODEL, D_FF), D_MODEL ** -0.5),
        'ffn_w2': nrm((N_EVEN, D_FF, D_MODEL), D_FF ** -0.5),
        's5_lambda_re': -0.5 + nrm(s5_shape, 0.01),
        's5_lambda_im': jnp.pi * jnp.arange(S5_STATE, dtype=f32) + nrm(s5_shape, 0.01),
        's5_log_dt': jax.random.uniform(next(keys), (N_ODD, S5_GROUPS), f32, math.log(1e-3), math.log(1e-1)),
        's5_b_re': nrm((N_ODD, S5_GROUPS, S5_STATE, S5_GROUP_CH), (2 * S5_GROUP_CH) ** -0.5),
        's5_b_im': nrm((N_ODD, S5_GROUPS, S5_STATE, S5_GROUP_CH), (2 * S5_GROUP_CH) ** -0.5),
        's5_c_re': nrm((N_ODD, S5_GROUPS, S5_GROUP_CH, S5_STATE), S5_STATE ** -0.5),
        's5_c_im': nrm((N_ODD, S5_GROUPS, S5_GROUP_CH, S5_STATE), S5_STATE ** -0.5),
        's5_d': nrm((N_ODD, D_MODEL)),
        's5_glu_w': nrm((N_ODD, D_MODEL, 2 * D_MODEL), D_MODEL ** -0.5),
        'moe_router_w': nrm((N_ODD, D_MODEL, N_EXPERTS), D_MODEL ** -0.5),
        'moe_router_b': nrm((N_ODD, N_EXPERTS), 0.01),
        'moe_w1': nrm((N_ODD, N_EXPERTS, D_MODEL, MOE_FF), D_MODEL ** -0.5),
        'moe_w3': nrm((N_ODD, N_EXPERTS, D_MODEL, MOE_FF), D_MODEL ** -0.5),
        'moe_w2': nrm((N_ODD, N_EXPERTS, MOE_FF, D_MODEL), MOE_FF ** -0.5),
    }


def reference(x_prompt, x_sample, c_prompt, c_sample, cache_k, cache_v, cache_idx_k, state_hgrn,
              state_s5_re, state_s5_im, page_table, rel_bias, ada_w, ada_b, norm_g, w_in, qk_norm_g,
              hgrn_gamma, hgrn_onorm_g, w_out, ffn_w1, ffn_w3, ffn_w2, s5_lambda_re, s5_lambda_im,
              s5_log_dt, s5_b_re, s5_b_im, s5_c_re, s5_c_im, s5_d, s5_glu_w, moe_router_w, moe_router_b,
              moe_w1, moe_w3, moe_w2):
    p = dict(ada_w=ada_w, ada_b=ada_b, norm_g=norm_g, w_in=w_in, qk_norm_g=qk_norm_g,
             hgrn_gamma=hgrn_gamma, hgrn_onorm_g=hgrn_onorm_g, w_out=w_out,
             ffn_w1=ffn_w1, ffn_w3=ffn_w3, ffn_w2=ffn_w2,
             s5_lambda_re=s5_lambda_re, s5_lambda_im=s5_lambda_im, s5_log_dt=s5_log_dt,
             s5_b_re=s5_b_re, s5_b_im=s5_b_im, s5_c_re=s5_c_re, s5_c_im=s5_c_im, s5_d=s5_d,
             s5_glu_w=s5_glu_w, moe_router_w=moe_router_w, moe_router_b=moe_router_b,
             moe_w1=moe_w1, moe_w3=moe_w3, moe_w2=moe_w2)

    def prompt_attend(j, q, k, v, qi, ki, wi):
        return dsa_prompt(q, k, v, qi, ki, wi, rel_bias)

    def sample_attend(j, q, k, v, qi, ki, wi):
        return dsa_sample(q, k, v, qi, ki, wi, cache_k[j], cache_v[j], cache_idx_k[j], page_table, rel_bias)

    bp = x_prompt.shape[0]
    hgrn0 = jnp.zeros((N_EVEN, bp, B_HEADS, B_DK, B_DV), jnp.float32)
    s50 = jnp.zeros((N_ODD, bp, S5_GROUPS, S5_STATE), jnp.float32)
    y_prompt, kp, vp, ikp, hp, srp, sip = trunk(x_prompt, c_prompt, p, prompt_attend, hgrn0, s50, s50)
    y_sample, ks, vs, iks, hs, srs, sis = trunk(x_sample, c_sample, p, sample_attend,
                                                 state_hgrn, state_s5_re, state_s5_im)
    return (y_prompt, y_sample, kp, vp, ikp, hp, srp, sip, ks, vs, iks, hs, srs, sis)
```

```python
import functools
import math

import numpy as np
import jax
import jax.numpy as jnp
from jax import lax
from jax.experimental import pallas as pl
from jax.experimental.pallas import tpu as pltpu

F32 = jnp.float32
BF16 = jnp.bfloat16
I32 = jnp.int32

D_MODEL = 1024
N_HEADS = 8
HEAD_DIM = 64
A_W = N_HEADS * HEAD_DIM
LANES = 128
TOPK_MAX = 256
PAGE = 128
REL_BUCKETS = 32
REL_MAX_DIST = 128
S5_GROUPS = 64
S5_CH = 16
S5_STATE = 64
S5_W = S5_GROUPS * S5_STATE
S5_BLK_GROUPS = 8
S5_NBLK = S5_GROUPS // S5_BLK_GROUPS
D_FF = 2816
N_EXPERTS = 8
MOE_FF = 1408
EPS = 1e-6
NEG = -1e30
INT_MIN = -2 ** 31
VMEM_LIMIT = 56 * 2 ** 20


def _cp(sem, vmem=VMEM_LIMIT):
    return pltpu.CompilerParams(dimension_semantics=sem, vmem_limit_bytes=vmem)


def _resident(shape):
    n = len(shape)
    return pl.BlockSpec(shape, lambda *_: (0,) * n, pipeline_mode=pl.Buffered(1))


def _dot(a, b):
    return jnp.dot(a, b, preferred_element_type=F32)


def _dot_nt(a, b):
    return lax.dot_general(a, b, (((1,), (1,)), ((), ())), preferred_element_type=F32)


def _split2(x):
    hi = x.astype(BF16)
    lo = (x - hi.astype(F32)).astype(BF16)
    return hi, lo


def _dot_x3(a, b, nt=False):
    d = _dot_nt if nt else _dot
    ah, al = _split2(a)
    bh, bl = _split2(b)
    return d(ah, bh) + (d(ah, bl) + d(al, bh))


def _sigmoid(x):
    return 1.0 / (1.0 + jnp.exp(-x))


def _silu(x):
    return x * _sigmoid(x)


def _gelu_tanh(x):
    return 0.5 * x * (1.0 + jnp.tanh(math.sqrt(2.0 / math.pi) * (x + 0.044715 * (x * x * x))))


def _norm_mod(x, g, sc, sh):
    ms = jnp.mean(x * x, axis=-1, keepdims=True)
    return (x * lax.rsqrt(ms + EPS) * g) * (1.0 + sc) + sh


def _seg_matrix(n, seg, dtype, scale=1.0):
    r = lax.broadcasted_iota(I32, (n, n), 0) // seg
    c = lax.broadcasted_iota(I32, (n, n), 1) // seg
    return jnp.where(r == c, scale, 0.0).astype(dtype)


def _ada_kernel(c_ref, w_ref, b_ref, o_ref):
    c = c_ref[...]
    o_ref[...] = _dot_x3(_silu(c), w_ref[...]) + b_ref[...]


def _ada(c_all, ada_w, ada_b):
    depth, d, n6 = ada_w.shape
    rows = c_all.shape[0]
    tn = 1536
    return pl.pallas_call(
        _ada_kernel,
        grid=(depth, n6 // tn),
        in_specs=[pl.BlockSpec((rows, d), lambda l, j: (0, 0)),
                  pl.BlockSpec((None, d, tn), lambda l, j: (l, 0, j)),
                  pl.BlockSpec((None, 1, tn), lambda l, j: (l, 0, j))],
        out_specs=pl.BlockSpec((None, rows, tn), lambda l, j: (l, 0, j)),
        out_shape=jax.ShapeDtypeStruct((depth, rows, n6), F32),
        compiler_params=_cp(("arbitrary", "arbitrary")),
        name="ada_mod",
    )(c_all, ada_w, ada_b.reshape(depth, 1, n6))


W_OFF_Q, W_OFF_K, W_OFF_V, W_OFF_QI, W_OFF_KW, W_OFF_B = 0, 1024, 1536, 2048, 3072, 3200
W_IN_PACKED = 3200 + 4 * 512


def _pack_w_in(w):
    d = w.shape[0]
    z64 = jnp.zeros((d, N_HEADS, HEAD_DIM), w.dtype)
    q = w[:, 0:512].reshape(d, N_HEADS, HEAD_DIM)
    even = (jnp.arange(N_HEADS) % 2 == 0)[None, :, None]
    q_pad = jnp.concatenate([jnp.where(even, q, z64), jnp.where(even, z64, q)], axis=-1).reshape(d, 1024)
    qi = w[:, 1536:2048].reshape(d, N_HEADS, HEAD_DIM)
    qi_pad = jnp.concatenate([qi, z64], axis=-1).reshape(d, 1024)
    kw = jnp.concatenate([w[:, 2048:2120], jnp.zeros((d, 56), w.dtype)], axis=-1)
    return jnp.concatenate([q_pad, w[:, 512:1536], qi_pad, kw, w[:, 2120:4168]], axis=-1).astype(BF16)


def _inproj_kernel(x_ref, sc_ref, sh_ref, ng_ref, w_ref, gq_ref, gk_ref, gam_ref,
                   q_ref, kf_ref, kb_ref, vf_ref, vb_ref, qi_ref, kw_ref,
                   gq_o, gk_o, gl_o, gi_o, gg_o):
    h = _norm_mod(x_ref[...], ng_ref[...], sc_ref[...], sh_ref[...]).astype(BF16)
    ones_seg = jnp.full((LANES, LANES), 1.0 / HEAD_DIM, BF16)
    pair_seg = _seg_matrix(LANES, HEAD_DIM, BF16, 1.0 / HEAD_DIM)

    def proj(off, n):
        return _dot(h, w_ref[:, off:off + n])

    pq = proj(W_OFF_Q, 1024)
    for j in range(8):
        blk = pq[:, j * LANES:(j + 1) * LANES]
        ms = _dot((blk * blk).astype(BF16), ones_seg)
        q_ref[:, j * LANES:(j + 1) * LANES] = (
            blk * lax.rsqrt(ms + EPS) * gq_ref[:, j * LANES:(j + 1) * LANES] * 0.125).astype(q_ref.dtype)
    pk = proj(W_OFF_K, 512)
    for j in range(4):
        blk = pk[:, j * LANES:(j + 1) * LANES]
        ms = _dot((blk * blk).astype(BF16), pair_seg)
        kn = blk * lax.rsqrt(ms + EPS) * gk_ref[:, j * LANES:(j + 1) * LANES]
        kf_ref[:, j * LANES:(j + 1) * LANES] = kn
        kb_ref[:, j * LANES:(j + 1) * LANES] = kn.astype(BF16)
    pv = proj(W_OFF_V, 512)
    vf_ref[...] = pv
    vb_ref[...] = pv.astype(BF16)
    qi_ref[...] = (proj(W_OFF_QI, 1024) * 0.125).astype(qi_ref.dtype)
    lane = lax.broadcasted_iota(I32, (1, LANES), 1)
    kw_ref[...] = proj(W_OFF_KW, LANES) * jnp.where(lane < HEAD_DIM, 1.0, N_HEADS ** -0.5)
    gq_o[...] = proj(W_OFF_B, 512) * 0.125
    gam = gam_ref[...]
    gmax = jnp.max(gam, axis=0, keepdims=True)
    ge = jnp.exp(gam - gmax)
    lb = ge[0:1, :] / jnp.sum(ge, axis=0, keepdims=True)
    f = lb + (1.0 - lb) * _sigmoid(proj(W_OFF_B + 512, 512))
    gk_o[...] = 1.0 - f
    gl_o[...] = jnp.log(f)
    gi_o[...] = proj(W_OFF_B + 1024, 512)
    gg_o[...] = proj(W_OFF_B + 1536, 512)


def _inproj(x, sc, sh, ng, w_packed, gq_pad, gk_tiled, gamma, *, tm, qi_dtype):
    n, d = x.shape
    row = lambda i: (i, 0)
    mod_spec = (pl.BlockSpec((1, d), lambda i: (0, 0)) if sc.shape[0] == 1
                else pl.BlockSpec((tm, d), row))
    outs = [((n, 1024), BF16), ((n, 512), F32), ((n, 512), BF16), ((n, 512), F32), ((n, 512), BF16),
            ((n, 1024), qi_dtype), ((n, LANES), F32)] + [((n, 512), F32)] * 5
    return pl.pallas_call(
        _inproj_kernel,
        grid=(n // tm,),
        in_specs=[pl.BlockSpec((tm, d), row), mod_spec, mod_spec,
                  _resident((1, d)), _resident(w_packed.shape), _resident((1, 1024)),
                  _resident((1, 512)), _resident(gamma.shape)],
        out_specs=[pl.BlockSpec((tm, s[1]), row) for s, _ in outs],
        out_shape=[jax.ShapeDtypeStruct(s, dt) for s, dt in outs],
        compiler_params=_cp(("arbitrary",)),
        name="in_proj",
    )(x, sc, sh, ng, w_packed, gq_pad, gk_tiled, gamma)


def _key_to_f32(key):
    neg = key < 0
    mag = jnp.where(neg, -key, key)
    bits = jnp.where(neg, mag | jnp.int32(INT_MIN), mag)
    f = lax.bitcast_convert_type(bits, F32)
    return jnp.where(key == jnp.int32(INT_MIN), -jnp.inf, f)


def _kth_largest(count_ge, rows, topk):
    def bit_body(i, base):
        cand = base + lax.shift_left(jnp.int32(1), 31 - i)
        cnt = count_ge(_key_to_f32(cand))
        return jnp.where(cnt >= topk, cand, base)
    base = lax.fori_loop(0, 32, bit_body, jnp.full((rows, 1), INT_MIN, I32))
    return _key_to_f32(base)


def _pidx_kernel(qi_ref, kw_ref, kit_ref, bias_ref, sc_ref, *, tq, tk, topk, seq):
    q0 = pl.program_id(0) * tq
    n_kt = (q0 + tq + tk - 1) // tk
    w = kw_ref[...]
    row = lax.broadcasted_iota(I32, (tq, tk), 0) + q0
    col = lax.broadcasted_iota(I32, (tq, tk), 1)

    def tile_off(kt):
        return pl.multiple_of(kt * tk, tk)

    def scores(kt, _):
        off = tile_off(kt)
        ks = kit_ref[:, pl.ds(off, tk)]
        acc = jnp.zeros((tq, tk), F32)
        for h in range(N_HEADS):
            x = _dot(qi_ref[:, h * LANES:h * LANES + HEAD_DIM], ks)
            acc = acc + w[:, HEAD_DIM + h:HEAD_DIM + h + 1] * jnp.maximum(x, 0.0)
        sc_ref[:, pl.ds(off, tk)] = jnp.where(col + off <= row, acc, -jnp.inf)
        return 0
    lax.fori_loop(0, n_kt, scores, 0)

    def fold(m):
        part = m[:, 0:LANES]
        for j in range(1, tk // LANES):
            part = part + m[:, j * LANES:(j + 1) * LANES]
        return part

    def counter(cmp):
        def count(thr):
            def body(kt, c):
                s = sc_ref[:, pl.ds(tile_off(kt), tk)]
                return c + fold(jnp.where(cmp(s, thr), 1.0, 0.0))
            c = lax.fori_loop(0, n_kt, body, jnp.zeros((tq, LANES), F32))
            return jnp.sum(c, axis=-1, keepdims=True)
        return count

    thr = _kth_largest(counter(lambda s, t: s >= t), tq, topk)
    need = topk - counter(lambda s, t: s > t)(thr)
    upper = jnp.where(lax.broadcasted_iota(I32, (tk, tk), 0) < lax.broadcasted_iota(I32, (tk, tk), 1),
                      1.0, 0.0).astype(BF16)

    def select(kt, ties_before):
        off = tile_off(kt)
        s = sc_ref[:, pl.ds(off, tk)]
        eq = jnp.where(s == thr, 1.0, 0.0)
        rank = _dot(eq.astype(BF16), upper) + ties_before
        keep = jnp.where(s > thr, 0.0, jnp.where(s == thr, jnp.where(rank < need, 0.0, NEG), NEG))
        bias_ref[:, pl.ds(off, tk)] = jnp.where(col + off <= row, keep, NEG).astype(BF16)
        return ties_before + jnp.sum(eq, axis=-1, keepdims=True)
    lax.fori_loop(0, n_kt, select, jnp.zeros((tq, 1), F32))

    def fill(kt, _):
        bias_ref[:, pl.ds(tile_off(kt), tk)] = jnp.full((tq, tk), NEG, BF16)
        return 0
    lax.fori_loop(n_kt, seq // tk, fill, 0)


def _prompt_indexer(qi_pad, kw, ki_t, *, tq, tk, topk):
    seq = qi_pad.shape[0]
    return pl.pallas_call(
        functools.partial(_pidx_kernel, tq=tq, tk=tk, topk=topk, seq=seq),
        grid=(seq // tq,),
        in_specs=[pl.BlockSpec((tq, 1024), lambda i: (i, 0)),
                  pl.BlockSpec((tq, LANES), lambda i: (i, 0)),
                  _resident(ki_t.shape)],
        out_specs=pl.BlockSpec((tq, seq), lambda i: (i, 0)),
        out_shape=jax.ShapeDtypeStruct((seq, seq), BF16),
        scratch_shapes=[pltpu.VMEM((tq, seq), F32)],
        compiler_params=_cp(("arbitrary",)),
        name="prompt_indexer",
    )(qi_pad, kw, ki_t)


def _t5_bucket_table():
    n = np.arange(REL_MAX_DIST, dtype=np.int64)
    max_exact = REL_BUCKETS // 2
    nf = np.maximum(n, 1).astype(np.float32)
    large = max_exact + (np.log(nf / np.float32(max_exact)) / np.float32(math.log(REL_MAX_DIST / max_exact))
                         * np.float32(REL_BUCKETS - max_exact)).astype(np.int32)
    large = np.minimum(large, REL_BUCKETS - 1)
    return np.where(n < max_exact, n, large).astype(np.int32)


def _bias_by_distance(rel_bias, dist):
    table = _t5_bucket_table()
    bucket = np.where(dist >= REL_MAX_DIST, REL_BUCKETS - 1, table[np.clip(dist, 0, REL_MAX_DIST - 1)])
    b = jnp.moveaxis(rel_bias[bucket], -1, 0)
    far = rel_bias[REL_BUCKETS - 1].reshape((N_HEADS,) + (1,) * dist.ndim)
    return (b - far).astype(F32)


def _pattn_kernel(q_ref, k_ref, v_ref, bias_ref, tab_ref, o_ref, m_ref, l_ref, acc_ref, *, tq, tk):
    q0 = pl.program_id(0) * tq
    last = (q0 + tq - 1) // tk
    m_ref[...] = jnp.full(m_ref.shape, NEG, F32)
    l_ref[...] = jnp.zeros(l_ref.shape, F32)
    acc_ref[...] = jnp.zeros(acc_ref.shape, F32)

    def tile(j, near):
        off = pl.multiple_of(j * tk, tk)
        mb = bias_ref[:, pl.ds(off, tk)].astype(F32)
        kt = k_ref[pl.ds(off, tk), :]
        vt = v_ref[pl.ds(off, tk), :]
        for h in range(N_HEADS):
            pr = h // 2
            s = _dot_nt(q_ref[:, h * LANES:(h + 1) * LANES], kt[:, pr * LANES:(pr + 1) * LANES]) + mb
            if near:
                s = s + tab_ref[(q0 - off) // LANES, h]
            m_old = m_ref[h]
            m_new = jnp.maximum(m_old, jnp.max(s, axis=-1, keepdims=True))
            alpha = jnp.exp(m_old - m_new)
            p = jnp.exp(s - m_new)
            l_ref[h] = alpha * l_ref[h] + jnp.sum(p, axis=-1, keepdims=True)
            acc_ref[h] = alpha * acc_ref[h] + _dot(p.astype(BF16), vt[:, pr * LANES:(pr + 1) * LANES])
            m_ref[h] = m_new

    def far_body(j, _):
        tile(j, False)
        return 0
    lax.fori_loop(0, jnp.maximum(last - 1, 0), far_body, 0)

    @pl.when(last >= 1)
    def _():
        tile(last - 1, True)
    tile(last, True)

    lane = lax.broadcasted_iota(I32, (tq, LANES), 1)
    for pr in range(N_HEADS // 2):
        lo = acc_ref[2 * pr] / l_ref[2 * pr]
        hi = acc_ref[2 * pr + 1] / l_ref[2 * pr + 1]
        o_ref[:, pr * LANES:(pr + 1) * LANES] = jnp.where(lane < HEAD_DIM, lo, hi).astype(o_ref.dtype)


def _prompt_attention(q_pad, k_bf, v_bf, bias, rel_bias, *, tq, tk):
    seq = q_pad.shape[0]
    qi = np.arange(tq)[:, None]
    si = np.arange(tk)[None, :]
    n_tab = (tq + tk) // LANES + 1
    dist = np.stack([d * LANES + qi - si for d in range(n_tab)])
    tab = _bias_by_distance(rel_bias, dist)
    tab = jnp.moveaxis(tab, 0, 1)
    return pl.pallas_call(
        functools.partial(_pattn_kernel, tq=tq, tk=tk),
        grid=(seq // tq,),
        in_specs=[pl.BlockSpec((tq, 1024), lambda i: (i, 0)),
                  _resident(k_bf.shape), _resident(v_bf.shape),
                  pl.BlockSpec((tq, seq), lambda i: (i, 0)),
                  _resident(tab.shape)],
        out_specs=pl.BlockSpec((tq, A_W), lambda i: (i, 0)),
        out_shape=jax.ShapeDtypeStruct((seq, A_W), BF16),
        scratch_shapes=[pltpu.VMEM((N_HEADS, tq, 1), F32), pltpu.VMEM((N_HEADS, tq, 1), F32),
                        pltpu.VMEM((N_HEADS, tq, LANES), F32)],
        compiler_params=_cp(("arbitrary",)),
        name="prompt_attention",
    )(q_pad, k_bf, v_bf, bias, tab)


def _cumsum_rows(x):
    c = x.shape[0]
    tri = jnp.where(lax.broadcasted_iota(I32, (c, c), 0) >= lax.broadcasted_iota(I32, (c, c), 1),
                    1.0, 0.0).astype(BF16)
    hi = x.astype(BF16)
    r1 = x - hi.astype(F32)
    mid = r1.astype(BF16)
    lo = (r1 - mid.astype(F32)).astype(BF16)
    return _dot(tri, hi) + (_dot(tri, mid) + _dot(tri, lo))


def _gla_kernel(q_ref, k_ref, v_ref, g_ref, gate_ref, s0_ref, on_ref, o_ref, sfin_ref,
                st_ref, b_ref, oi_ref, *, chunk):
    @pl.when(pl.program_id(1) == 0)
    def _():
        st_ref[...] = s0_ref[...]

    w = q_ref.shape[-1]
    b = _cumsum_rows(g_ref[...])
    b_ref[...] = b
    k = k_ref[...]
    v = v_ref[...]
    seg = _seg_matrix(w, HEAD_DIM, BF16)
    st = st_ref[...]
    o_inter = _dot_nt((q_ref[...] * jnp.exp(b)).astype(BF16), st.astype(BF16))
    rows = lax.broadcasted_iota(I32, (chunk, w), 0)

    def one_query(t, _):
        bt = b_ref[pl.ds(t, 1), :]
        dec = jnp.where(rows <= t, jnp.exp(jnp.minimum(bt - b, 0.0)), 0.0)
        prod = (q_ref[pl.ds(t, 1), :] * dec * k).astype(BF16)
        att = _dot(prod, seg)
        oi_ref[pl.ds(t, 1), :] = jnp.sum(att * v, axis=0, keepdims=True)
        return 0
    lax.fori_loop(0, chunk, one_query, 0)

    o = o_inter + oi_ref[...]
    ms = _dot((o * o).astype(BF16), seg) * (1.0 / HEAD_DIM)
    o_ref[...] = (o * lax.rsqrt(ms + EPS) * on_ref[...] * _silu(gate_ref[...])).astype(o_ref.dtype)

    b_last = b[chunk - 1:chunk, :]
    kd = (k * jnp.exp(b_last - b)).astype(BF16)
    upd = _dot(v.T.astype(BF16), kd)
    blockdiag = (lax.broadcasted_iota(I32, (w, w), 0) // HEAD_DIM
                 == lax.broadcasted_iota(I32, (w, w), 1) // HEAD_DIM)
    st_new = st * jnp.exp(b_last) + jnp.where(blockdiag, upd, 0.0)
    st_ref[...] = st_new
    sfin_ref[...] = st_new


def _gla(gq, gk, gv, glog, gate, s0_t, onorm_tiled, *, batch, chunk):
    n, w = gq.shape
    nc = n // batch // chunk
    row = lambda b, c: (b * nc + c, 0)
    tile = pl.BlockSpec((chunk, w), row)
    return pl.pallas_call(
        functools.partial(_gla_kernel, chunk=chunk),
        grid=(batch, nc),
        in_specs=[tile] * 5 + [pl.BlockSpec((None, w, w), lambda b, c: (b, 0, 0)),
                               pl.BlockSpec((1, w), lambda b, c: (0, 0))],
        out_specs=[tile, pl.BlockSpec((None, w, w), lambda b, c: (b, 0, 0))],
        out_shape=[jax.ShapeDtypeStruct((n, w), BF16), jax.ShapeDtypeStruct((batch, w, w), F32)],
        scratch_shapes=[pltpu.VMEM((w, w), F32), pltpu.VMEM((chunk, w), F32), pltpu.VMEM((chunk, w), F32)],
        compiler_params=_cp(("arbitrary", "arbitrary")),
        name="hgrn2_gla",
    )(gq, gk, gv, glog, gate, s0_t, onorm_tiled)


def _state_to_blockdiag_t(s0):
    b = s0.shape[0]
    eye = jnp.eye(N_HEADS, dtype=s0.dtype)
    st = jnp.einsum('bhkv,hg->bhvgk', s0, eye)
    return st.reshape(b, A_W, A_W)


def _blockdiag_t_to_state(st):
    b = st.shape[0]
    s5 = st.reshape(b, N_HEADS, HEAD_DIM, N_HEADS, HEAD_DIM)
    idx = jnp.arange(N_HEADS)
    diag = s5[:, idx, :, idx, :]
    return jnp.transpose(diag, (1, 0, 3, 2))


def _even_tail_kernel(x_ref, a_ref, b_ref, g1_ref, sc_ref, sh_ref, g2_ref, ng_ref,
                      wa_ref, wb_ref, w1_ref, w3_ref, w2_ref, o_ref, *, ff_split):
    mix = _dot(a_ref[...], wa_ref[...]) + _dot(b_ref[...], wb_ref[...])
    x1 = x_ref[...] + g1_ref[...] * mix
    h = _norm_mod(x1, ng_ref[...], sc_ref[...], sh_ref[...]).astype(BF16)
    ff = jnp.zeros(x1.shape, F32)
    step = w1_ref.shape[1] // ff_split
    for j in range(ff_split):
        a = _dot(h, w1_ref[:, j * step:(j + 1) * step])
        g = _dot(h, w3_ref[:, j * step:(j + 1) * step])
        ff = ff + _dot((_silu(a) * g).astype(BF16), w2_ref[j * step:(j + 1) * step, :])
    o_ref[...] = x1 + g2_ref[...] * ff


def _mod_spec(m, tm, d):
    return (pl.BlockSpec((1, d), lambda i: (0, 0)) if m.shape[0] == 1
            else pl.BlockSpec((tm, d), lambda i: (i, 0)))


def _even_tail(x, a_out, b_out, g1, sc2, sh2, g2, ng2, wa, wb, w1, w3, w2, *, tm):
    n, d = x.shape
    row = lambda i: (i, 0)
    return pl.pallas_call(
        functools.partial(_even_tail_kernel, ff_split=2),
        grid=(n // tm,),
        in_specs=[pl.BlockSpec((tm, d), row), pl.BlockSpec((tm, A_W), row), pl.BlockSpec((tm, A_W), row),
                  _mod_spec(g1, tm, d), _mod_spec(sc2, tm, d), _mod_spec(sh2, tm, d), _mod_spec(g2, tm, d),
                  _resident((1, d)), _resident(wa.shape), _resident(wb.shape),
                  _resident(w1.shape), _resident(w3.shape), _resident(w2.shape)],
        out_specs=pl.BlockSpec((tm, d), row),
        out_shape=jax.ShapeDtypeStruct((n, d), F32),
        compiler_params=_cp(("arbitrary",)),
        name="even_tail",
    )(x, a_out, b_out, g1, sc2, sh2, g2, ng2, wa, wb, w1, w3, w2)


S5_LG = S5_W // LANES
S5_LG_PER_BLK = S5_LG // S5_NBLK


def _s5_prep_kernel(lr_ref, li_ref, ldt_ref, br_ref, bi_ref, ar_o, ai_o, bbr_o, bbi_o):
    lr = jnp.minimum(lr_ref[...], -1e-4)
    li = li_ref[...]
    dt = jnp.exp(ldt_ref[...])
    mag = jnp.exp(lr * dt)
    a_re = mag * jnp.cos(li * dt)
    a_im = mag * jnp.sin(li * dt)
    den = lr * lr + li * li
    nr = a_re - 1.0
    coef_re = (nr * lr + a_im * li) / den
    coef_im = (a_im * lr - nr * li) / den
    ar_o[...] = a_re
    ai_o[...] = a_im
    br = br_ref[...]
    bi = bi_ref[...]
    bbr_o[...] = coef_re * br - coef_im * bi
    bbi_o[...] = coef_re * bi + coef_im * br


def _s5_prep(lam_re, lam_im, log_dt, b_re, b_im):
    g, p = lam_re.shape
    v3 = lambda a: a.reshape(g, 1, p)
    bt = lambda b: jnp.transpose(b, (0, 2, 1))
    sds = jax.ShapeDtypeStruct
    return pl.pallas_call(
        _s5_prep_kernel,
        out_shape=[sds((g, 1, p), F32), sds((g, 1, p), F32), sds((g, S5_CH, p), F32), sds((g, S5_CH, p), F32)],
        name="s5_prep",
    )(v3(lam_re), v3(lam_im), log_dt.reshape(g, 1, 1), bt(b_re), bt(b_im))


def _s5_blockdiag_in(bb_t):
    x = bb_t.reshape(S5_NBLK, S5_BLK_GROUPS, S5_CH, S5_STATE)
    bd = jnp.einsum('bgcp,gh->bgchp', x, jnp.eye(S5_BLK_GROUPS, dtype=x.dtype))
    return bd.reshape(S5_NBLK, S5_BLK_GROUPS * S5_CH, S5_BLK_GROUPS * S5_STATE)


def _s5_blockdiag_out(c):
    x = c.reshape(S5_NBLK, S5_BLK_GROUPS, S5_CH, S5_STATE)
    bd = jnp.einsum('bgcp,gh->bgphc', x, jnp.eye(S5_BLK_GROUPS, dtype=x.dtype))
    return bd.reshape(S5_NBLK, S5_BLK_GROUPS * S5_STATE, S5_BLK_GROUPS * S5_CH)


def _s5_project_in(u, bbr_hi, bbr_lo, bbi_hi, bbi_lo, xre_s, xim_s):
    for blk in range(S5_NBLK):
        uh, ul = _split2(u[:, blk * LANES:(blk + 1) * LANES])
        for hi, lo, dst in ((bbr_hi, bbr_lo, xre_s), (bbi_hi, bbi_lo, xim_s)):
            r = _dot(uh, hi[blk]) + (_dot(uh, lo[blk]) + _dot(ul, hi[blk]))
            for j in range(S5_LG_PER_BLK):
                dst[blk * S5_LG_PER_BLK + j] = r[:, j * LANES:(j + 1) * LANES]


def _s5_scan(xre_s, xim_s, st_re, st_im, ar_ref, ai_ref, *, n_steps, r, store):
    def blk_body(blk, _):
        base = blk * S5_LG_PER_BLK
        ar = [jnp.broadcast_to(ar_ref[base + j], (r, LANES)) for j in range(S5_LG_PER_BLK)]
        ai = [jnp.broadcast_to(ai_ref[base + j], (r, LANES)) for j in range(S5_LG_PER_BLK)]

        def step(s, carry):
            r0 = pl.multiple_of(s * r, r)
            new = []
            for j in range(S5_LG_PER_BLK):
                sr, si = carry[2 * j], carry[2 * j + 1]
                nr = ar[j] * sr - ai[j] * si + xre_s[base + j, pl.ds(r0, r), :]
                ni = ar[j] * si + ai[j] * sr + xim_s[base + j, pl.ds(r0, r), :]
                if store:
                    xre_s[base + j, pl.ds(r0, r), :] = nr
                    xim_s[base + j, pl.ds(r0, r), :] = ni
                new += [nr, ni]
            return tuple(new)

        init = []
        for j in range(S5_LG_PER_BLK):
            init += [st_re[base + j], st_im[base + j]]
        fin = lax.fori_loop(0, n_steps, step, tuple(init))
        for j in range(S5_LG_PER_BLK):
            st_re[base + j] = fin[2 * j]
            st_im[base + j] = fin[2 * j + 1]
        return 0
    lax.fori_loop(0, S5_NBLK, blk_body, 0)


def _s5_local_kernel(x_ref, sc_ref, sh_ref, ng_ref, bbr_hi, bbr_lo, bbi_hi, bbi_lo, ar_ref, ai_ref,
                     lre_o, lim_o, xre_s, xim_s, st_re, st_im, *, n_steps, r):
    @pl.when(pl.program_id(0) == 0)
    def _():
        st_re[...] = jnp.zeros(st_re.shape, F32)
        st_im[...] = jnp.zeros(st_im.shape, F32)
    u = _norm_mod(x_ref[...], ng_ref[...], sc_ref[...], sh_ref[...])
    _s5_project_in(u, bbr_hi, bbr_lo, bbi_hi, bbi_lo, xre_s, xim_s)
    _s5_scan(xre_s, xim_s, st_re, st_im, ar_ref, ai_ref, n_steps=n_steps, r=r, store=False)
    lre_o[...] = st_re[...]
    lim_o[...] = st_im[...]


def _cpow(ar, ai, n):
    rr, ri = None, None
    br, bi = ar, ai
    while n:
        if n & 1:
            if rr is None:
                rr, ri = br, bi
            else:
                rr, ri = rr * br - ri * bi, rr * bi + ri * br
        n >>= 1
        if n:
            br, bi = br * br - bi * bi, 2.0 * br * bi
    return rr, ri


def _s5_full_kernel(x_ref, sc_ref, sh_ref, ng_ref, g1_ref, bbr_hi, bbr_lo, bbi_hi, bbi_lo, ar_ref, ai_ref,
                    cre_ref, cim_ref, d_ref, h0re_ref, h0im_ref, lre_ref, lim_ref, gluw_ref,
                    o_ref, fre_o, fim_o, xre_s, xim_s, st_re, st_im, *, n_steps, r, seg_len):
    @pl.when(pl.program_id(0) == 0)
    def _():
        if seg_len is None:
            st_re[...] = h0re_ref[...]
            st_im[...] = h0im_ref[...]
        else:
            pr, pi = _cpow(ar_ref[...], ai_ref[...], seg_len)
            sr = h0re_ref[:, 0:1, :]
            si = h0im_ref[:, 0:1, :]
            st_re[:, 0:1, :] = sr
            st_im[:, 0:1, :] = si
            for row in range(1, r):
                sr, si = (lre_ref[:, row - 1:row, :] + (pr * sr - pi * si),
                          lim_ref[:, row - 1:row, :] + (pr * si + pi * sr))
                st_re[:, row:row + 1, :] = sr
                st_im[:, row:row + 1, :] = si

    x = x_ref[...]
    u = _norm_mod(x, ng_ref[...], sc_ref[...], sh_ref[...])
    _s5_project_in(u, bbr_hi, bbr_lo, bbi_hi, bbi_lo, xre_s, xim_s)
    _s5_scan(xre_s, xim_s, st_re, st_im, ar_ref, ai_ref, n_steps=n_steps, r=r, store=True)
    fre_o[...] = st_re[...]
    fim_o[...] = st_im[...]
    ys = []
    for blk in range(S5_NBLK):
        acc = None
        for j in range(S5_LG_PER_BLK):
            lg = blk * S5_LG_PER_BLK + j
            t = (_dot(xre_s[lg].astype(BF16), cre_ref[blk, j * LANES:(j + 1) * LANES, :])
                 - _dot(xim_s[lg].astype(BF16), cim_ref[blk, j * LANES:(j + 1) * LANES, :]))
            acc = t if acc is None else acc + t
        ys.append(acc)
    y = jnp.concatenate(ys, axis=-1) + d_ref[...] * u
    z = _dot(_gelu_tanh(y).astype(BF16), gluw_ref[...])
    dm = x.shape[-1]
    mix = z[:, :dm] * _sigmoid(z[:, dm:])
    o_ref[...] = x + g1_ref[...] * mix


def _s5_weight_specs(ws):
    return [_resident(w.shape) for w in ws]


def _s5_local(x, sc, sh, ng, bb, a3, *, r, rows):
    n, d = x.shape
    n_steps = rows // r
    sds = jax.ShapeDtypeStruct((S5_LG, r, LANES), F32)
    return pl.pallas_call(
        functools.partial(_s5_local_kernel, n_steps=n_steps, r=r),
        grid=(n // rows,),
        in_specs=[pl.BlockSpec((rows, d), lambda i: (i, 0)), _mod_spec(sc, rows, d), _mod_spec(sh, rows, d),
                  _resident((1, d))] + _s5_weight_specs(bb + a3),
        out_specs=[_resident(sds.shape)] * 2,
        out_shape=[sds, sds],
        scratch_shapes=[pltpu.VMEM((S5_LG, rows, LANES), F32)] * 2 + [pltpu.VMEM((S5_LG, r, LANES), F32)] * 2,
        compiler_params=_cp(("arbitrary",)),
        name="s5_local_scan",
    )(x, sc, sh, ng, *bb, *a3)


def _s5_full(x, sc, sh, ng, g1, bb, a3, cre, cim, dskip, h0re, h0im, lre, lim, gluw, *, r, rows, seg_len):
    n, d = x.shape
    n_steps = rows // r
    st = jax.ShapeDtypeStruct((S5_LG, r, LANES), F32)
    return pl.pallas_call(
        functools.partial(_s5_full_kernel, n_steps=n_steps, r=r, seg_len=seg_len),
        grid=(n // rows,),
        in_specs=[pl.BlockSpec((rows, d), lambda i: (i, 0)), _mod_spec(sc, rows, d), _mod_spec(sh, rows, d),
                  _resident((1, d)), _mod_spec(g1, rows, d)]
                 + _s5_weight_specs(bb + a3 + [cre, cim, dskip, h0re, h0im, lre, lim, gluw]),
        out_specs=[pl.BlockSpec((rows, d), lambda i: (i, 0)), _resident(st.shape), _resident(st.shape)],
        out_shape=[jax.ShapeDtypeStruct((n, d), F32), st, st],
        scratch_shapes=[pltpu.VMEM((S5_LG, rows, LANES), F32)] * 2 + [pltpu.VMEM((S5_LG, r, LANES), F32)] * 2,
        compiler_params=_cp(("arbitrary",)),
        name="s5_scan_glu",
    )(x, sc, sh, ng, g1, *bb, *a3, cre, cim, dskip, h0re, h0im, lre, lim, gluw)


def _to_lane_groups(s):
    r = s.shape[0]
    return jnp.transpose(s.reshape(r, S5_LG, LANES), (1, 0, 2))


def _from_lane_groups(s):
    r = s.shape[1]
    return jnp.transpose(s, (1, 0, 2)).reshape(r, S5_W)


def _moe_kernel(x_ref, sc_ref, sh_ref, g2_ref, ng_ref, rw_hi, rw_lo, rb_ref, w1_ref, w3_ref, w2_ref,
                o_ref, h_s, gate_s, acc_s):
    e = pl.program_id(1)
    lane = lax.broadcasted_iota(I32, gate_s.shape, 1)

    @pl.when(e == 0)
    def _():
        h = _norm_mod(x_ref[...], ng_ref[...], sc_ref[...], sh_ref[...])
        h_s[...] = h.astype(BF16)
        hh, hl = _split2(h)
        logits = _dot(hh, rw_hi[...]) + (_dot(hh, rw_lo[...]) + _dot(hl, rw_hi[...])) + rb_ref[...]
        logits = jnp.where(lane < N_EXPERTS, logits, -jnp.inf)
        m1 = jnp.max(logits, axis=-1, keepdims=True)
        i1 = jnp.min(jnp.where(logits == m1, lane, LANES), axis=-1, keepdims=True)
        rest = jnp.where(lane == i1, -jnp.inf, logits)
        m2 = jnp.max(rest, axis=-1, keepdims=True)
        i2 = jnp.min(jnp.where(rest == m2, lane, LANES), axis=-1, keepdims=True)
        e2 = jnp.exp(m2 - m1)
        den = 1.0 + e2
        gate_s[...] = jnp.where(lane == i1, 1.0 / den, 0.0) + jnp.where(lane == i2, e2 / den, 0.0)
        acc_s[...] = jnp.zeros(acc_s.shape, F32)

    h = h_s[...]
    ge = jnp.sum(jnp.where(lane == e, gate_s[...], 0.0), axis=-1, keepdims=True)
    y = _dot((_silu(_dot(h, w1_ref[...])) * _dot(h, w3_ref[...])).astype(BF16), w2_ref[...])
    acc_s[...] += ge * y

    @pl.when(e == N_EXPERTS - 1)
    def _():
        o_ref[...] = x_ref[...] + g2_ref[...] * acc_s[...]


def _moe(x, sc, sh, g2, ng, rw_hi, rw_lo, rb, w1, w3, w2, *, tm):
    n, d = x.shape
    ne, _, ff = w1.shape
    mod = lambda m: (pl.BlockSpec((1, d), lambda i, e: (0, 0)) if m.shape[0] == 1
                     else pl.BlockSpec((tm, d), lambda i, e: (i, 0)))
    const = lambda shape: pl.BlockSpec(shape, lambda i, e: (0,) * len(shape), pipeline_mode=pl.Buffered(1))
    return pl.pallas_call(
        _moe_kernel,
        grid=(n // tm, ne),
        in_specs=[pl.BlockSpec((tm, d), lambda i, e: (i, 0)), mod(sc), mod(sh), mod(g2), const((1, d)),
                  const(rw_hi.shape), const(rw_lo.shape), const(rb.shape),
                  pl.BlockSpec((None, d, ff), lambda i, e: (e, 0, 0)),
                  pl.BlockSpec((None, d, ff), lambda i, e: (e, 0, 0)),
                  pl.BlockSpec((None, ff, d), lambda i, e: (e, 0, 0))],
        out_specs=pl.BlockSpec((tm, d), lambda i, e: (i, 0)),
        out_shape=jax.ShapeDtypeStruct((n, d), F32),
        scratch_shapes=[pltpu.VMEM((tm, d), BF16), pltpu.VMEM((tm, LANES), F32), pltpu.VMEM((tm, d), F32)],
        compiler_params=_cp(("arbitrary", "arbitrary")),
        name="moe_dense",
    )(x, sc, sh, g2, ng, rw_hi, rw_lo, rb, w1, w3, w2)


def _sidx_kernel(pt_ref, qi_ref, w_ref, ikp_ref, ikn_ref, bias_ref, sc_ref, *, n_pages, topk, t_new):
    del pt_ref
    p = pl.program_id(1)
    rows = t_new

    def page_scores(keys):
        r = jnp.maximum(_dot_x3(qi_ref[...], keys, nt=True), 0.0) * w_ref[...]
        acc = r[0:rows]
        for h in range(1, N_HEADS):
            acc = acc + r[h * rows:(h + 1) * rows]
        return acc

    sc_ref[p] = page_scores(ikp_ref[...])

    @pl.when(p == n_pages - 1)
    def _():
        row = lax.broadcasted_iota(I32, (rows, PAGE), 0)
        col = lax.broadcasted_iota(I32, (rows, PAGE), 1)
        sc_ref[n_pages] = jnp.where(col <= row, page_scores(ikn_ref[...]), -jnp.inf)

        def counter(cmp):
            def count(thr):
                def body(j, c):
                    return c + jnp.where(cmp(sc_ref[j], thr), 1.0, 0.0)
                c = lax.fori_loop(0, n_pages + 1, body, jnp.zeros((rows, PAGE), F32))
                return jnp.sum(c, axis=-1, keepdims=True)
            return count

        thr = _kth_largest(counter(lambda s, t: s >= t), rows, topk)
        need = topk - counter(lambda s, t: s > t)(thr)
        upper = jnp.where(lax.broadcasted_iota(I32, (PAGE, PAGE), 0) < lax.broadcasted_iota(I32, (PAGE, PAGE), 1),
                          1.0, 0.0).astype(BF16)

        def select(j, ties_before):
            s = sc_ref[j]
            eq = jnp.where(s == thr, 1.0, 0.0)
            rank = _dot(eq.astype(BF16), upper) + ties_before
            keep = jnp.where(s > thr, 0.0, jnp.where(s == thr, jnp.where(rank < need, 0.0, NEG), NEG))
            bias_ref[j] = jnp.where(s == -jnp.inf, NEG, keep)
            return ties_before + jnp.sum(eq, axis=-1, keepdims=True)
        lax.fori_loop(0, n_pages + 1, select, jnp.zeros((rows, 1), F32))


def _sample_indexer(page_table, qi_rows, w_rows, cache_ik, ki_new, *, topk, t_new):
    bd, n_pages = page_table.shape
    hq = qi_rows.shape[1]
    grid_spec = pltpu.PrefetchScalarGridSpec(
        num_scalar_prefetch=1,
        grid=(bd, n_pages),
        in_specs=[pl.BlockSpec((None, hq, HEAD_DIM), lambda b, p, pt: (b, 0, 0)),
                  pl.BlockSpec((None, hq, 1), lambda b, p, pt: (b, 0, 0)),
                  pl.BlockSpec((None, PAGE, HEAD_DIM), lambda b, p, pt: (pt[b, p], 0, 0)),
                  pl.BlockSpec((None, PAGE, HEAD_DIM), lambda b, p, pt: (b, 0, 0))],
        out_specs=pl.BlockSpec((None, n_pages + 1, t_new, PAGE), lambda b, p, pt: (b, 0, 0, 0)),
        scratch_shapes=[pltpu.VMEM((n_pages + 1, t_new, PAGE), F32)])
    return pl.pallas_call(
        functools.partial(_sidx_kernel, n_pages=n_pages, topk=topk, t_new=t_new),
        grid_spec=grid_spec,
        out_shape=jax.ShapeDtypeStruct((bd, n_pages + 1, t_new, PAGE), F32),
        compiler_params=_cp(("arbitrary", "arbitrary")),
        name="sample_indexer",
    )(page_table, qi_rows, w_rows, cache_ik, ki_new)


def _sattn_kernel(pt_ref, q_ref, kp_ref, vp_ref, kn_ref, vn_ref, mb_ref, tabl_ref, tabn_ref, o_ref,
                  m_s, l_s, acc_s, *, n_pages, t_new):
    del pt_ref
    p = pl.program_id(1)

    @pl.when(p == 0)
    def _():
        m_s[...] = jnp.full(m_s.shape, NEG, F32)
        l_s[...] = jnp.zeros(l_s.shape, F32)
        acc_s[...] = jnp.zeros(acc_s.shape, F32)

    is_new = p == n_pages
    kt = jnp.where(is_new, kn_ref[...], kp_ref[...]).astype(BF16)
    vt = jnp.where(is_new, vn_ref[...], vp_ref[...]).astype(BF16)
    mb = mb_ref[p]
    s = _dot_nt(q_ref[...], kt) + jnp.concatenate([mb] * N_HEADS, axis=0)
    s = s + jnp.where(is_new, tabn_ref[...], jnp.where(p == n_pages - 1, tabl_ref[...], 0.0))
    m_old = m_s[...]
    m_new = jnp.maximum(m_old, jnp.max(s, axis=-1, keepdims=True))
    alpha = jnp.exp(m_old - m_new)
    pexp = jnp.exp(s - m_new)
    l_s[...] = alpha * l_s[...] + jnp.sum(pexp, axis=-1, keepdims=True)
    acc_s[...] = alpha * acc_s[...] + _dot(pexp.astype(BF16), vt)
    m_s[...] = m_new

    @pl.when(is_new)
    def _():
        r = acc_s[...] / l_s[...]
        lane_head = lax.broadcasted_iota(I32, (t_new, A_W), 1) // HEAD_DIM
        out = jnp.zeros((t_new, A_W), F32)
        for h in range(N_HEADS):
            out = out + jnp.where(lane_head == h, r[h * t_new:(h + 1) * t_new], 0.0)
        o_ref[...] = out.astype(o_ref.dtype)


def _sample_attention(page_table, q_bd, cache_k, cache_v, k_new, v_new, mask, tab_last, tab_new, *, t_new):
    bd, n_pages = page_table.shape
    hq = q_bd.shape[1]
    page = lambda b, p, pt: (pt[b, jnp.minimum(p, n_pages - 1)], 0, 0)
    per_b = lambda b, p, pt: (b, 0, 0)
    const2 = lambda b, p, pt: (0, 0)
    grid_spec = pltpu.PrefetchScalarGridSpec(
        num_scalar_prefetch=1,
        grid=(bd, n_pages + 1),
        in_specs=[pl.BlockSpec((None, hq, A_W), per_b),
                  pl.BlockSpec((None, PAGE, A_W), page), pl.BlockSpec((None, PAGE, A_W), page),
                  pl.BlockSpec((None, PAGE, A_W), per_b), pl.BlockSpec((None, PAGE, A_W), per_b),
                  pl.BlockSpec((None, n_pages + 1, t_new, PAGE), lambda b, p, pt: (b, 0, 0, 0)),
                  pl.BlockSpec((hq, PAGE), const2), pl.BlockSpec((hq, PAGE), const2)],
        out_specs=pl.BlockSpec((None, t_new, A_W), per_b),
        scratch_shapes=[pltpu.VMEM((hq, 1), F32), pltpu.VMEM((hq, 1), F32), pltpu.VMEM((hq, A_W), F32)])
    return pl.pallas_call(
        functools.partial(_sattn_kernel, n_pages=n_pages, t_new=t_new),
        grid_spec=grid_spec,
        out_shape=jax.ShapeDtypeStruct((bd, t_new, A_W), BF16),
        compiler_params=_cp(("arbitrary", "arbitrary")),
        name="sample_attention",
    )(page_table, q_bd, cache_k, cache_v, k_new, v_new, mask, tab_last, tab_new)


S5_ROWS = 256
S5_SEGMENTS = 8


def _hi_lo(w):
    hi = w.astype(BF16)
    return hi, (w - hi.astype(F32)).astype(BF16)


def _even_layer_front(x, sc1, sh1, p, *, tm, qi_dtype):
    return _inproj(x, sc1, sh1, p['ng0'], p['w_in'], p['gq_pad'], p['gk_tiled'], p['gamma'],
                   tm=tm, qi_dtype=qi_dtype)


def _odd_layer(x, m, p, h0re, h0im, *, r, seg_len):
    sh1, sc1, g1, sh2, sc2, g2 = m
    zeros = jnp.zeros((S5_LG, r, LANES), F32)
    if seg_len is None:
        lre, lim = zeros, zeros
    else:
        lre, lim = _s5_local(x, sc1, sh1, p['ng1'], p['bb'], p['a3'], r=r, rows=S5_ROWS)
    x, fre, fim = _s5_full(x, sc1, sh1, p['ng1'], g1, p['bb'], p['a3'], p['cre'], p['cim'], p['dskip'],
                           h0re, h0im, lre, lim, p['gluw'], r=r, rows=S5_ROWS, seg_len=seg_len)
    x = _moe(x, sc2, sh2, g2, p['ng1b'], p['rw_hi'], p['rw_lo'], p['rb'], p['mw1'], p['mw3'], p['mw2'],
             tm=min(512, x.shape[0]))
    return x, _from_lane_groups(fre), _from_lane_groups(fim)


def kernel(x_prompt, x_sample, c_prompt, c_sample, cache_k, cache_v, cache_idx_k, state_hgrn, state_s5_re, state_s5_im, page_table, rel_bias, ada_w, ada_b, norm_g, w_in, qk_norm_g, hgrn_gamma, hgrn_onorm_g, w_out, ffn_w1, ffn_w3, ffn_w2, s5_lambda_re, s5_lambda_im, s5_log_dt, s5_b_re, s5_b_im, s5_c_re, s5_c_im, s5_d, s5_glu_w, moe_router_w, moe_router_b, moe_w1, moe_w3, moe_w2):
    bp, seq, d = x_prompt.shape
    bd, t_new, _ = x_sample.shape
    n_dec = bd * t_new
    n_phys = cache_k.shape[1]
    past = page_table.shape[1] * PAGE
    assert bp == 1 and d == D_MODEL and seq % 512 == 0
    assert n_dec == S5_ROWS and t_new == 8

    g_q, g_k = qk_norm_g[0, 0], qk_norm_g[0, 1]
    ar, ai, bbr_t, bbi_t = _s5_prep(s5_lambda_re[0], s5_lambda_im[0], s5_log_dt[0], s5_b_re[0], s5_b_im[0])
    bb = []
    for w in (_s5_blockdiag_in(bbr_t), _s5_blockdiag_in(bbi_t)):
        bb += list(_hi_lo(w))
    rw = jnp.pad(moe_router_w[0], ((0, 0), (0, LANES - N_EXPERTS)))
    rw_hi, rw_lo = _hi_lo(rw)
    p = dict(
        ng0=norm_g[0, 0][None], ng0b=norm_g[0, 1][None], ng1=norm_g[1, 0][None], ng1b=norm_g[1, 1][None],
        w_in=_pack_w_in(w_in[0]), gq_pad=jnp.tile(g_q, 2 * N_HEADS)[None], gk_tiled=jnp.tile(g_k, N_HEADS)[None],
        gamma=hgrn_gamma, onorm=jnp.tile(hgrn_onorm_g[0], N_HEADS)[None],
        wa=w_out[0, :A_W].astype(BF16), wb=w_out[0, A_W:].astype(BF16),
        w1=ffn_w1[0].astype(BF16), w3=ffn_w3[0].astype(BF16), w2=ffn_w2[0].astype(BF16),
        bb=bb, a3=[ar.reshape(S5_LG, 1, LANES), ai.reshape(S5_LG, 1, LANES)],
        cre=_s5_blockdiag_out(s5_c_re[0]).astype(BF16), cim=_s5_blockdiag_out(s5_c_im[0]).astype(BF16),
        dskip=s5_d[0][None], gluw=s5_glu_w[0].astype(BF16),
        rw_hi=rw_hi, rw_lo=rw_lo, rb=jnp.pad(moe_router_b[0], (0, LANES - N_EXPERTS))[None],
        mw1=moe_w1[0].astype(BF16), mw3=moe_w3[0].astype(BF16), mw2=moe_w2[0].astype(BF16),
    )

    c_rows = bp + bd
    c_all = jnp.concatenate([c_prompt, c_sample, jnp.zeros((-c_rows % 8, d), F32)], axis=0)
    mod = _ada(c_all, ada_w, ada_b)

    def mods(layer, rows, expand):
        m = mod[layer, rows]
        return tuple(expand(m[:, i * d:(i + 1) * d]) for i in range(6))

    xp = x_prompt.reshape(seq, d)
    sh1, sc1, g1, sh2, sc2, g2 = mods(0, slice(0, 1), lambda a: a)
    (q_pad, k_f, k_b, v_f, v_b, qi_pad, kw, gq, gk, gl, gi, gg) = _even_layer_front(
        xp, sc1, sh1, p, tm=512, qi_dtype=BF16)
    ki_p = kw[:, :HEAD_DIM]
    mask = _prompt_indexer(qi_pad, kw, ki_p.T.astype(BF16), tq=128, tk=512, topk=min(TOPK_MAX, seq // 4))
    a_out = _prompt_attention(q_pad, k_b, v_b, mask, rel_bias, tq=128, tk=256)
    b_out, st_p = _gla(gq, gk, gi, gl, gg, jnp.zeros((1, A_W, A_W), F32), p['onorm'], batch=1, chunk=64)
    xp = _even_tail(xp, a_out, b_out, g1, sc2, sh2, g2, p['ng0b'], p['wa'], p['wb'], p['w1'], p['w3'], p['w2'],
                    tm=512)
    seg_len = seq // S5_SEGMENTS
    to_seg = lambda a: a.reshape(S5_SEGMENTS, seg_len, d).transpose(1, 0, 2).reshape(seq, d)
    zero_state = jnp.zeros((S5_LG, S5_SEGMENTS, LANES), F32)
    xp, fre_p, fim_p = _odd_layer(to_seg(xp), mods(1, slice(0, 1), lambda a: a), p, zero_state, zero_state,
                                  r=S5_SEGMENTS, seg_len=seg_len)
    y_prompt = xp.reshape(seg_len, S5_SEGMENTS, d).transpose(1, 0, 2).reshape(bp, seq, d)

    xs = x_sample.reshape(n_dec, d)
    per_token = lambda a: jnp.repeat(a, t_new, axis=0)
    sh1, sc1, g1, sh2, sc2, g2 = mods(0, slice(bp, bp + bd), per_token)
    (q_pad_s, k_fs, _, v_fs, _, qi_s, kw_s, gq, gk, gl, gi, gg) = _even_layer_front(
        xs, sc1, sh1, p, tm=n_dec, qi_dtype=F32)
    heads_first = lambda a: jnp.transpose(a, (0, 2, 1, 3)).reshape(bd, N_HEADS * t_new, a.shape[-1])
    qi_rows = heads_first(qi_s.reshape(bd, t_new, N_HEADS, LANES)[..., :HEAD_DIM])
    w_rows = heads_first(kw_s[:, HEAD_DIM:HEAD_DIM + N_HEADS].reshape(bd, t_new, N_HEADS, 1))
    pad_new = lambda a: jnp.pad(a.reshape(bd, t_new, -1), ((0, 0), (0, PAGE - t_new), (0, 0)))
    ki_s = kw_s[:, :HEAD_DIM]
    mask_s = _sample_indexer(page_table, qi_rows, w_rows, cache_idx_k[0], pad_new(ki_s),
                             topk=min(TOPK_MAX, (past + t_new) // 4), t_new=t_new)
    q4 = q_pad_s.reshape(bd, t_new, N_HEADS, LANES)
    even = (jnp.arange(N_HEADS) % 2 == 0)[None, None, :, None]
    q_nat = jnp.where(even, q4[..., :HEAD_DIM], q4[..., HEAD_DIM:])
    q_bd = jnp.einsum('bthd,hg->bhtgd', q_nat, jnp.eye(N_HEADS, dtype=q_nat.dtype))
    q_bd = q_bd.reshape(bd, N_HEADS * t_new, A_W)
    qpos = np.arange(t_new)[:, None]
    col = np.arange(PAGE)[None, :]
    rows_hq = lambda t: t.reshape(N_HEADS * t_new, PAGE)
    tab_last = rows_hq(_bias_by_distance(rel_bias, PAGE + qpos - col))
    tab_new = rows_hq(_bias_by_distance(rel_bias, qpos - col))
    a_out_s = _sample_attention(page_table, q_bd, cache_k[0].reshape(n_phys, PAGE, A_W),
                                cache_v[0].reshape(n_phys, PAGE, A_W), pad_new(k_fs), pad_new(v_fs),
                                mask_s, tab_last, tab_new, t_new=t_new).reshape(n_dec, A_W)
    gla_chunk = 64
    pad_t = lambda a: jnp.pad(a.reshape(bd, t_new, A_W), ((0, 0), (0, gla_chunk - t_new), (0, 0))).reshape(-1, A_W)
    b_out_s, st_s = _gla(pad_t(gq), pad_t(gk), pad_t(gi), pad_t(gl), pad_t(gg),
                         _state_to_blockdiag_t(state_hgrn[0]), p['onorm'], batch=bd, chunk=gla_chunk)
    b_out_s = b_out_s.reshape(bd, gla_chunk, A_W)[:, :t_new].reshape(n_dec, A_W)
    xs = _even_tail(xs, a_out_s, b_out_s, g1, sc2, sh2, g2, p['ng0b'], p['wa'], p['wb'],
                    p['w1'], p['w3'], p['w2'], tm=n_dec)
    step_major = lambda a: a.reshape(bd, t_new, d).transpose(1, 0, 2).reshape(n_dec, d)
    xs, fre_s, fim_s = _odd_layer(step_major(xs), mods(1, slice(bp, bp + bd), lambda a: jnp.tile(a, (t_new, 1))), p,
                                  _to_lane_groups(state_s5_re[0].reshape(bd, S5_W)),
                                  _to_lane_groups(state_s5_im[0].reshape(bd, S5_W)), r=bd, seg_len=None)
    y_sample = xs.reshape(t_new, bd, d).transpose(1, 0, 2)

    heads = lambda a, b, t: a.reshape(1, b, t, N_HEADS, HEAD_DIM)
    s5_state = lambda f: f.reshape(1, -1, S5_GROUPS, S5_STATE)
    return (y_prompt, y_sample,
            heads(k_f, bp, seq), heads(v_f, bp, seq), ki_p.reshape(1, bp, seq, HEAD_DIM),
            _blockdiag_t_to_state(st_p)[None], s5_state(fre_p[S5_SEGMENTS - 1:]), s5_state(fim_p[S5_SEGMENTS - 1:]),
            heads(k_fs, bd, t_new), heads(v_fs, bd, t_new), ki_s.reshape(1, bd, t_new, HEAD_DIM),
            _blockdiag_t_to_state(st_s)[None], s5_state(fre_s), s5_state(fim_s))
```

```python
import functools
import math

import numpy as np
import jax
import jax.numpy as jnp
from jax import lax
from jax.experimental import pallas as pl
from jax.experimental.pallas import tpu as pltpu

F32 = jnp.float32
BF16 = jnp.bfloat16
I32 = jnp.int32

D_MODEL = 1024
N_HEADS = 8
HEAD_DIM = 64
A_W = N_HEADS * HEAD_DIM
LANES = 128
TOPK_MAX = 256
PAGE = 128
REL_BUCKETS = 32
REL_MAX_DIST = 128
S5_GROUPS = 64
S5_CH = 16
S5_STATE = 64
S5_W = S5_GROUPS * S5_STATE
S5_BLK_GROUPS = 8
S5_NBLK = S5_GROUPS // S5_BLK_GROUPS
D_FF = 2816
N_EXPERTS = 8
MOE_FF = 1408
EPS = 1e-6
NEG = -1e30
INT_MIN = -2 ** 31
VMEM_LIMIT = 56 * 2 ** 20


def _cp(sem, vmem=VMEM_LIMIT):
    return pltpu.CompilerParams(dimension_semantics=sem, vmem_limit_bytes=vmem)


def _resident(shape):
    n = len(shape)
    return pl.BlockSpec(shape, lambda *_: (0,) * n, pipeline_mode=pl.Buffered(1))


def _dot(a, b):
    return jnp.dot(a, b, preferred_element_type=F32)


def _dot_nt(a, b):
    return lax.dot_general(a, b, (((1,), (1,)), ((), ())), preferred_element_type=F32)


def _split2(x):
    hi = x.astype(BF16)
    lo = (x - hi.astype(F32)).astype(BF16)
    return hi, lo


def _dot_x3(a, b, nt=False):
    d = _dot_nt if nt else _dot
    ah, al = _split2(a)
    bh, bl = _split2(b)
    return d(ah, bh) + (d(ah, bl) + d(al, bh))


def _sigmoid(x):
    return 1.0 / (1.0 + jnp.exp(-x))


def _silu(x):
    return x * _sigmoid(x)


def _gelu_tanh(x):
    return 0.5 * x * (1.0 + jnp.tanh(math.sqrt(2.0 / math.pi) * (x + 0.044715 * (x * x * x))))


def _norm_mod(x, g, sc, sh):
    ms = jnp.mean(x * x, axis=-1, keepdims=True)
    return (x * lax.rsqrt(ms + EPS) * g) * (1.0 + sc) + sh


def _seg_matrix(n, seg, dtype, scale=1.0):
    r = lax.broadcasted_iota(I32, (n, n), 0) // seg
    c = lax.broadcasted_iota(I32, (n, n), 1) // seg
    return jnp.where(r == c, scale, 0.0).astype(dtype)


def _ada_kernel(c_ref, w_ref, b_ref, o_ref):
    c = c_ref[...]
    o_ref[...] = _dot_x3(_silu(c), w_ref[...]) + b_ref[...]


def _ada(c_all, ada_w, ada_b):
    depth, d, n6 = ada_w.shape
    rows = c_all.shape[0]
    tn = 1536
    return pl.pallas_call(
        _ada_kernel,
        grid=(depth, n6 // tn),
        in_specs=[pl.BlockSpec((rows, d), lambda l, j: (0, 0)),
                  pl.BlockSpec((None, d, tn), lambda l, j: (l, 0, j)),
                  pl.BlockSpec((None, 1, tn), lambda l, j: (l, 0, j))],
        out_specs=pl.BlockSpec((None, rows, tn), lambda l, j: (l, 0, j)),
        out_shape=jax.ShapeDtypeStruct((depth, rows, n6), F32),
        compiler_params=_cp(("arbitrary", "arbitrary")),
        name="ada_mod",
    )(c_all, ada_w, ada_b.reshape(depth, 1, n6))


W_OFF_Q, W_OFF_K, W_OFF_V, W_OFF_QI, W_OFF_KW, W_OFF_B = 0, 1024, 1536, 2048, 2560, 2688


def _pack_w_in(w):
    d = w.shape[0]
    z64 = jnp.zeros((d, N_HEADS, HEAD_DIM), w.dtype)
    q = w[:, 0:512].reshape(d, N_HEADS, HEAD_DIM)
    even = (jnp.arange(N_HEADS) % 2 == 0)[None, :, None]
    q_pad = jnp.concatenate([jnp.where(even, q, z64), jnp.where(even, z64, q)], axis=-1).reshape(d, 1024)
    kw = jnp.concatenate([w[:, 2048:2120], jnp.zeros((d, 56), w.dtype)], axis=-1)
    return jnp.concatenate([q_pad, w[:, 512:2048], kw, w[:, 2120:4168]], axis=-1).astype(BF16)


def _inproj_kernel(x_ref, sc_ref, sh_ref, ng_ref, w_ref, gq_ref, gk_ref, gam_ref,
                   q_ref, kf_ref, kb_ref, vf_ref, vb_ref, qi_ref, kw_ref,
                   gq_o, gk_o, gl_o, gi_o, gg_o):
    h = _norm_mod(x_ref[...], ng_ref[...], sc_ref[...], sh_ref[...]).astype(BF16)
    ones_seg = jnp.full((LANES, LANES), 1.0 / HEAD_DIM, BF16)
    pair_seg = _seg_matrix(LANES, HEAD_DIM, BF16, 1.0 / HEAD_DIM)

    def proj(off, n):
        return _dot(h, w_ref[:, off:off + n])

    pq = proj(W_OFF_Q, 1024)
    for j in range(8):
        blk = pq[:, j * LANES:(j + 1) * LANES]
        ms = _dot((blk * blk).astype(BF16), ones_seg)
        q_ref[:, j * LANES:(j + 1) * LANES] = (
            blk * lax.rsqrt(ms + EPS) * gq_ref[:, j * LANES:(j + 1) * LANES] * 0.125).astype(q_ref.dtype)
    pk = proj(W_OFF_K, 512)
    for j in range(4):
        blk = pk[:, j * LANES:(j + 1) * LANES]
        ms = _dot((blk * blk).astype(BF16), pair_seg)
        kn = blk * lax.rsqrt(ms + EPS) * gk_ref[:, j * LANES:(j + 1) * LANES]
        kf_ref[:, j * LANES:(j + 1) * LANES] = kn
        kb_ref[:, j * LANES:(j + 1) * LANES] = kn.astype(BF16)
    pv = proj(W_OFF_V, 512)
    vf_ref[...] = pv
    vb_ref[...] = pv.astype(BF16)
    qi_ref[...] = (proj(W_OFF_QI, 512) * 0.125).astype(qi_ref.dtype)
    lane = lax.broadcasted_iota(I32, (1, LANES), 1)
    kw_ref[...] = proj(W_OFF_KW, LANES) * jnp.where(lane < HEAD_DIM, 1.0, N_HEADS ** -0.5)
    gq_o[...] = proj(W_OFF_B, 512) * 0.125
    gam = gam_ref[...]
    gmax = jnp.max(gam, axis=0, keepdims=True)
    ge = jnp.exp(gam - gmax)
    lb = ge[0:1, :] / jnp.sum(ge, axis=0, keepdims=True)
    f = lb + (1.0 - lb) * _sigmoid(proj(W_OFF_B + 512, 512))
    gk_o[...] = 1.0 - f
    gl_o[...] = jnp.log(f)
    gi_o[...] = proj(W_OFF_B + 1024, 512)
    gg_o[...] = proj(W_OFF_B + 1536, 512)


def _inproj(x, sc, sh, ng, w_packed, gq_pad, gk_tiled, gamma, *, tm, qi_dtype):
    n, d = x.shape
    row = lambda i: (i, 0)
    mod_spec = (pl.BlockSpec((1, d), lambda i: (0, 0)) if sc.shape[0] == 1
                else pl.BlockSpec((tm, d), row))
    outs = [((n, 1024), BF16), ((n, 512), F32), ((n, 512), BF16), ((n, 512), F32), ((n, 512), BF16),
            ((n, 512), qi_dtype), ((n, LANES), F32)] + [((n, 512), F32)] * 5
    return pl.pallas_call(
        _inproj_kernel,
        grid=(n // tm,),
        in_specs=[pl.BlockSpec((tm, d), row), mod_spec, mod_spec,
                  _resident((1, d)), _resident(w_packed.shape), _resident((1, 1024)),
                  _resident((1, 512)), _resident(gamma.shape)],
        out_specs=[pl.BlockSpec((tm, s[1]), row) for s, _ in outs],
        out_shape=[jax.ShapeDtypeStruct(s, dt) for s, dt in outs],
        compiler_params=_cp(("arbitrary",)),
        name="in_proj",
    )(x, sc, sh, ng, w_packed, gq_pad, gk_tiled, gamma)


def _key_to_f32(key):
    neg = key < 0
    mag = jnp.where(neg, -key, key)
    bits = jnp.where(neg, mag | jnp.int32(INT_MIN), mag)
    f = lax.bitcast_convert_type(bits, F32)
    return jnp.where(key == jnp.int32(INT_MIN), -jnp.inf, f)


def _kth_largest(count_ge, shape, topk):
    def cond(state):
        i, _, done, _ = state
        return jnp.logical_and(i < 32, jnp.min(done) == 0)

    def body(state):
        i, base, done, thr = state
        cand = base + lax.shift_left(jnp.int32(1), 31 - i)
        cand_f = _key_to_f32(cand)
        cnt = count_ge(cand_f)
        hit = jnp.logical_and(cnt == topk, done == 0)
        thr = jnp.where(hit, cand_f, thr)
        done = jnp.where(hit, 1, done)
        return i + 1, jnp.where(cnt >= topk, cand, base), done, thr

    init = (jnp.int32(0), jnp.full(shape, INT_MIN, I32), jnp.zeros(shape, I32), jnp.zeros(shape, F32))
    _, base, done, thr = lax.while_loop(cond, body, init)
    return jnp.where(done == 1, thr, _key_to_f32(base))


def _pidx_kernel(qit_ref, wt_ref, ki_ref, bias_ref, sc_ref, *, tq, tk, topk, seq):
    q0 = pl.program_id(0) * tq
    n_kt = (q0 + tq + tk - 1) // tk
    key = lax.broadcasted_iota(I32, (tk, tq), 0)
    qry = lax.broadcasted_iota(I32, (tk, tq), 1) + q0

    def tile_off(kt):
        return pl.multiple_of(kt * tk, tk)

    def scores(kt, _):
        off = tile_off(kt)
        ks = ki_ref[pl.ds(off, tk), :]
        acc = jnp.zeros((tk, tq), F32)
        for h in range(N_HEADS):
            x = _dot(ks, qit_ref[h * HEAD_DIM:(h + 1) * HEAD_DIM, :])
            acc = acc + wt_ref[h:h + 1, :] * jnp.maximum(x, 0.0)
        sc_ref[pl.ds(off, tk), :] = jnp.where(key + off <= qry, acc, -jnp.inf)
        return 0
    lax.fori_loop(0, n_kt, scores, 0)

    def counter(cmp):
        def count(thr):
            def body(kt, c):
                hit = jnp.where(cmp(sc_ref[pl.ds(tile_off(kt), tk), :], thr), 1.0, 0.0)
                return c + jnp.sum(hit.reshape(tk // 8, 8, tq), axis=0)
            c = lax.fori_loop(0, n_kt, body, jnp.zeros((8, tq), F32))
            return jnp.sum(c, axis=0, keepdims=True)
        return count

    thr = _kth_largest(counter(lambda s, t: s >= t), (1, tq), topk)
    need = topk - counter(lambda s, t: s > t)(thr)
    lower = jnp.where(lax.broadcasted_iota(I32, (tk, tk), 0) > lax.broadcasted_iota(I32, (tk, tk), 1),
                      1.0, 0.0).astype(BF16)

    def select(kt, ties_before):
        off = tile_off(kt)
        s = sc_ref[pl.ds(off, tk), :]
        eq = jnp.where(s == thr, 1.0, 0.0)
        rank = _dot(lower, eq.astype(BF16)) + ties_before
        keep = jnp.where(s > thr, 0.0, jnp.where(s == thr, jnp.where(rank < need, 0.0, NEG), NEG))
        bias_ref[pl.ds(off, tk), :] = jnp.where(key + off <= qry, keep, NEG).astype(BF16)
        return ties_before + jnp.sum(eq, axis=0, keepdims=True)
    lax.fori_loop(0, n_kt, select, jnp.zeros((1, tq), F32))

    def fill(kt, _):
        bias_ref[pl.ds(tile_off(kt), tk), :] = jnp.full((tk, tq), NEG, BF16)
        return 0
    lax.fori_loop(n_kt, seq // tk, fill, 0)


def _prompt_indexer(qi_t, w_t, ki, *, tq, tk, topk):
    seq = ki.shape[0]
    return pl.pallas_call(
        functools.partial(_pidx_kernel, tq=tq, tk=tk, topk=topk, seq=seq),
        grid=(seq // tq,),
        in_specs=[pl.BlockSpec((A_W, tq), lambda i: (0, i)),
                  pl.BlockSpec((N_HEADS, tq), lambda i: (0, i)),
                  _resident(ki.shape)],
        out_specs=pl.BlockSpec((None, seq, tq), lambda i: (i, 0, 0)),
        out_shape=jax.ShapeDtypeStruct((seq // tq, seq, tq), BF16),
        scratch_shapes=[pltpu.VMEM((seq, tq), F32)],
        compiler_params=_cp(("arbitrary",)),
        name="prompt_indexer",
    )(qi_t, w_t, ki)


def _t5_bucket_table():
    n = np.arange(REL_MAX_DIST, dtype=np.int64)
    max_exact = REL_BUCKETS // 2
    nf = np.maximum(n, 1).astype(np.float32)
    large = max_exact + (np.log(nf / np.float32(max_exact)) / np.float32(math.log(REL_MAX_DIST / max_exact))
                         * np.float32(REL_BUCKETS - max_exact)).astype(np.int32)
    large = np.minimum(large, REL_BUCKETS - 1)
    return np.where(n < max_exact, n, large).astype(np.int32)


def _bias_by_distance(rel_bias, dist):
    table = _t5_bucket_table()
    bucket = np.where(dist >= REL_MAX_DIST, REL_BUCKETS - 1, table[np.clip(dist, 0, REL_MAX_DIST - 1)])
    b = jnp.moveaxis(rel_bias[bucket], -1, 0)
    far = rel_bias[REL_BUCKETS - 1].reshape((N_HEADS,) + (1,) * dist.ndim)
    return (b - far).astype(F32)


def _pattn_kernel(qt_ref, k_ref, vt_ref, mask_ref, tab_ref, ot_ref, m_ref, l_ref, acc_ref, *slots, tile, chunk):
    slot_a, slot_b = slots[:4], slots[4:]
    qb = pl.program_id(0)
    first = pl.program_id(1) * (chunk // tile)
    n_tiles = chunk // tile

    @pl.when(pl.program_id(1) == 0)
    def _():
        m_ref[...] = jnp.full(m_ref.shape, NEG, F32)
        l_ref[...] = jnp.zeros(l_ref.shape, F32)
        acc_ref[...] = jnp.zeros(acc_ref.shape, F32)

    def scores(j, slot, near):
        s_ref, mx_ref, _, _ = slot
        off = pl.multiple_of(j * tile, tile)
        mb = mask_ref[pl.ds(off, tile), :].astype(F32)
        for h in range(N_HEADS):
            pr = h // 2
            s = _dot(k_ref[pl.ds(off, tile), pr * LANES:(pr + 1) * LANES], qt_ref[h * LANES:(h + 1) * LANES, :])
            s = s + mb
            if near is not None:
                s = s + tab_ref[near, h]
            s_ref[h] = s
            mx_ref[h] = jnp.max(s, axis=0, keepdims=True)

    def accumulate(j, slot):
        s_ref, mx_ref, p_ref, al_ref = slot
        off = pl.multiple_of(j * tile, tile)
        for h in range(N_HEADS):
            m_old = m_ref[h]
            m_new = jnp.maximum(m_old, mx_ref[h])
            alpha = jnp.exp(m_old - m_new)
            p = jnp.exp(s_ref[h] - m_new)
            l_ref[h] = alpha * l_ref[h] + jnp.sum(p, axis=0, keepdims=True)
            p_ref[h] = p.astype(BF16)
            al_ref[h] = alpha
            m_ref[h] = m_new
        for h in range(N_HEADS):
            pv = _dot(vt_ref[h * HEAD_DIM:(h + 1) * HEAD_DIM, pl.ds(off, tile)], p_ref[h])
            acc_ref[h] = al_ref[h] * acc_ref[h] + pv

    all_far = first + n_tiles <= qb - 1

    @pl.when(all_far)
    def _():
        scores(0, slot_a, None)

        def pair(i, _):
            scores(2 * i + 1, slot_b, None)
            accumulate(2 * i, slot_a)
            scores(jnp.minimum(2 * i + 2, n_tiles - 1), slot_a, None)
            accumulate(2 * i + 1, slot_b)
            return 0
        lax.fori_loop(0, n_tiles // 2, pair, 0)

    @pl.when(jnp.logical_not(all_far))
    def _():
        def far_body(j, _):
            scores(j, slot_a, None)
            accumulate(j, slot_a)
            return 0
        lax.fori_loop(0, jnp.clip(qb - 1 - first, 0, n_tiles), far_body, 0)
        for near, j in ((1, qb - 1 - first), (0, qb - first)):
            @pl.when(jnp.logical_and(j >= 0, j < n_tiles))
            def _():
                scores(j, slot_a, near)
                accumulate(j, slot_a)

    @pl.when(pl.program_id(1) == pl.num_programs(1) - 1)
    def _():
        for h in range(N_HEADS):
            ot_ref[h * HEAD_DIM:(h + 1) * HEAD_DIM, :] = (acc_ref[h] / l_ref[h]).astype(ot_ref.dtype)


def _prompt_attention(q_t, k_bf, v_t, mask, rel_bias, *, tile, chunk):
    seq = k_bf.shape[0]
    chunk = min(chunk, seq)
    key = np.arange(tile)[:, None]
    qry = np.arange(tile)[None, :]
    tab = _bias_by_distance(rel_bias, np.stack([qry - key, tile + qry - key]))
    tab = jnp.moveaxis(tab, 0, 1)
    last_chunk = lambda i: (i * tile + tile - 1) // chunk
    return pl.pallas_call(
        functools.partial(_pattn_kernel, tile=tile, chunk=chunk),
        grid=(seq // tile, seq // chunk),
        in_specs=[pl.BlockSpec((N_HEADS * LANES, tile), lambda i, c: (0, i)),
                  pl.BlockSpec((chunk, A_W), lambda i, c: (jnp.minimum(c, last_chunk(i)), 0)),
                  pl.BlockSpec((A_W, chunk), lambda i, c: (0, jnp.minimum(c, last_chunk(i)))),
                  pl.BlockSpec((None, chunk, tile), lambda i, c: (i, jnp.minimum(c, last_chunk(i)), 0)),
                  pl.BlockSpec(tab.shape, lambda i, c: (0, 0, 0, 0), pipeline_mode=pl.Buffered(1))],
        out_specs=pl.BlockSpec((A_W, tile), lambda i, c: (0, i)),
        out_shape=jax.ShapeDtypeStruct((A_W, seq), BF16),
        scratch_shapes=[pltpu.VMEM((N_HEADS, 1, tile), F32), pltpu.VMEM((N_HEADS, 1, tile), F32),
                        pltpu.VMEM((N_HEADS, HEAD_DIM, tile), F32)]
                       + [pltpu.VMEM((N_HEADS, tile, tile), F32), pltpu.VMEM((N_HEADS, 1, tile), F32),
                          pltpu.VMEM((N_HEADS, tile, tile), BF16), pltpu.VMEM((N_HEADS, 1, tile), F32)] * 2,
        compiler_params=_cp(("arbitrary", "arbitrary")),
        name="prompt_attention",
    )(q_t, k_bf, v_t, mask, tab)


def _cumsum_rows(x):
    c = x.shape[0]
    tri = jnp.where(lax.broadcasted_iota(I32, (c, c), 0) >= lax.broadcasted_iota(I32, (c, c), 1),
                    1.0, 0.0).astype(BF16)
    hi = x.astype(BF16)
    r1 = x - hi.astype(F32)
    mid = r1.astype(BF16)
    lo = (r1 - mid.astype(F32)).astype(BF16)
    return _dot(tri, hi) + (_dot(tri, mid) + _dot(tri, lo))


def _gla_kernel(q_ref, k_ref, v_ref, g_ref, gate_ref, s0_ref, on_ref, o_ref, sfin_ref,
                st_ref, b_ref, oi_ref, *, chunk):
    @pl.when(pl.program_id(1) == 0)
    def _():
        st_ref[...] = s0_ref[...]

    w = q_ref.shape[-1]
    b = _cumsum_rows(g_ref[...])
    b_ref[...] = b
    k = k_ref[...]
    v = v_ref[...]
    seg = _seg_matrix(w, HEAD_DIM, BF16)
    st = st_ref[...]
    o_inter = _dot_nt((q_ref[...] * jnp.exp(b)).astype(BF16), st.astype(BF16))
    for grp in range(chunk // 8):
        n = 8 * (grp + 1)
        rows = lax.broadcasted_iota(I32, (n, w), 0)
        out_rows = []
        for t in range(8 * grp, n):
            dec = jnp.where(rows <= t, jnp.exp(jnp.minimum(b_ref[t:t + 1, :] - b_ref[0:n, :], 0.0)), 0.0)
            prod = (q_ref[t:t + 1, :] * dec * k_ref[0:n, :]).astype(BF16)
            out_rows.append(jnp.sum(_dot(prod, seg) * v_ref[0:n, :], axis=0, keepdims=True))
        oi_ref[8 * grp:n, :] = jnp.concatenate(out_rows, axis=0)

    o = o_inter + oi_ref[...]
    ms = _dot((o * o).astype(BF16), seg) * (1.0 / HEAD_DIM)
    o_ref[...] = (o * lax.rsqrt(ms + EPS) * on_ref[...] * _silu(gate_ref[...])).astype(o_ref.dtype)

    b_last = b[chunk - 1:chunk, :]
    kd = (k * jnp.exp(b_last - b)).astype(BF16)
    upd = _dot(v.T.astype(BF16), kd)
    blockdiag = (lax.broadcasted_iota(I32, (w, w), 0) // HEAD_DIM
                 == lax.broadcasted_iota(I32, (w, w), 1) // HEAD_DIM)
    st_new = st * jnp.exp(b_last) + jnp.where(blockdiag, upd, 0.0)
    st_ref[...] = st_new
    sfin_ref[...] = st_new


def _gla(gq, gk, gv, glog, gate, s0_t, onorm_tiled, *, batch, chunk):
    n, w = gq.shape
    nc = n // batch // chunk
    row = lambda b, c: (b * nc + c, 0)
    tile = pl.BlockSpec((chunk, w), row)
    return pl.pallas_call(
        functools.partial(_gla_kernel, chunk=chunk),
        grid=(batch, nc),
        in_specs=[tile] * 5 + [pl.BlockSpec((None, w, w), lambda b, c: (b, 0, 0)),
                               pl.BlockSpec((1, w), lambda b, c: (0, 0))],
        out_specs=[tile, pl.BlockSpec((None, w, w), lambda b, c: (b, 0, 0))],
        out_shape=[jax.ShapeDtypeStruct((n, w), BF16), jax.ShapeDtypeStruct((batch, w, w), F32)],
        scratch_shapes=[pltpu.VMEM((w, w), F32), pltpu.VMEM((chunk, w), F32), pltpu.VMEM((chunk, w), F32)],
        compiler_params=_cp(("arbitrary", "arbitrary")),
        name="hgrn2_gla",
    )(gq, gk, gv, glog, gate, s0_t, onorm_tiled)


def _state_to_blockdiag_t(s0):
    b = s0.shape[0]
    eye = jnp.eye(N_HEADS, dtype=s0.dtype)
    st = jnp.einsum('bhkv,hg->bhvgk', s0, eye)
    return st.reshape(b, A_W, A_W)


def _blockdiag_t_to_state(st):
    b = st.shape[0]
    s5 = st.reshape(b, N_HEADS, HEAD_DIM, N_HEADS, HEAD_DIM)
    idx = jnp.arange(N_HEADS)
    diag = s5[:, idx, :, idx, :]
    return jnp.transpose(diag, (1, 0, 3, 2))


def _even_tail_kernel(x_ref, a_ref, b_ref, g1_ref, sc_ref, sh_ref, g2_ref, ng_ref,
                      wa_ref, wb_ref, w1_ref, w3_ref, w2_ref, o_ref, *, ff_split):
    mix = _dot(a_ref[...], wa_ref[...]) + _dot(b_ref[...], wb_ref[...])
    x1 = x_ref[...] + g1_ref[...] * mix
    h = _norm_mod(x1, ng_ref[...], sc_ref[...], sh_ref[...]).astype(BF16)
    ff = jnp.zeros(x1.shape, F32)
    step = w1_ref.shape[1] // ff_split
    for j in range(ff_split):
        a = _dot(h, w1_ref[:, j * step:(j + 1) * step])
        g = _dot(h, w3_ref[:, j * step:(j + 1) * step])
        ff = ff + _dot((_silu(a) * g).astype(BF16), w2_ref[j * step:(j + 1) * step, :])
    o_ref[...] = x1 + g2_ref[...] * ff


def _mod_spec(m, tm, d):
    return (pl.BlockSpec((1, d), lambda i: (0, 0)) if m.shape[0] == 1
            else pl.BlockSpec((tm, d), lambda i: (i, 0)))


def _even_tail(x, a_out, b_out, g1, sc2, sh2, g2, ng2, wa, wb, w1, w3, w2, *, tm):
    n, d = x.shape
    row = lambda i: (i, 0)
    return pl.pallas_call(
        functools.partial(_even_tail_kernel, ff_split=2),
        grid=(n // tm,),
        in_specs=[pl.BlockSpec((tm, d), row), pl.BlockSpec((tm, A_W), row), pl.BlockSpec((tm, A_W), row),
                  _mod_spec(g1, tm, d), _mod_spec(sc2, tm, d), _mod_spec(sh2, tm, d), _mod_spec(g2, tm, d),
                  _resident((1, d)), _resident(wa.shape), _resident(wb.shape),
                  _resident(w1.shape), _resident(w3.shape), _resident(w2.shape)],
        out_specs=pl.BlockSpec((tm, d), row),
        out_shape=jax.ShapeDtypeStruct((n, d), F32),
        compiler_params=_cp(("arbitrary",)),
        name="even_tail",
    )(x, a_out, b_out, g1, sc2, sh2, g2, ng2, wa, wb, w1, w3, w2)


S5_LG = S5_W // LANES
S5_LG_PER_BLK = S5_LG // S5_NBLK


def _s5_prep_kernel(lr_ref, li_ref, ldt_ref, br_ref, bi_ref, ar_o, ai_o, bbr_o, bbi_o):
    lr = jnp.minimum(lr_ref[...], -1e-4)
    li = li_ref[...]
    dt = jnp.exp(ldt_ref[...])
    mag = jnp.exp(lr * dt)
    a_re = mag * jnp.cos(li * dt)
    a_im = mag * jnp.sin(li * dt)
    den = lr * lr + li * li
    nr = a_re - 1.0
    coef_re = (nr * lr + a_im * li) / den
    coef_im = (a_im * lr - nr * li) / den
    ar_o[...] = a_re
    ai_o[...] = a_im
    br = br_ref[...]
    bi = bi_ref[...]
    bbr_o[...] = coef_re * br - coef_im * bi
    bbi_o[...] = coef_re * bi + coef_im * br


def _s5_prep(lam_re, lam_im, log_dt, b_re, b_im):
    g, p = lam_re.shape
    v3 = lambda a: a.reshape(g, 1, p)
    bt = lambda b: jnp.transpose(b, (0, 2, 1))
    sds = jax.ShapeDtypeStruct
    return pl.pallas_call(
        _s5_prep_kernel,
        out_shape=[sds((g, 1, p), F32), sds((g, 1, p), F32), sds((g, S5_CH, p), F32), sds((g, S5_CH, p), F32)],
        name="s5_prep",
    )(v3(lam_re), v3(lam_im), log_dt.reshape(g, 1, 1), bt(b_re), bt(b_im))


def _s5_blockdiag_in(bb_t):
    x = bb_t.reshape(S5_NBLK, S5_BLK_GROUPS, S5_CH, S5_STATE)
    bd = jnp.einsum('bgcp,gh->bgchp', x, jnp.eye(S5_BLK_GROUPS, dtype=x.dtype))
    return bd.reshape(S5_NBLK, S5_BLK_GROUPS * S5_CH, S5_BLK_GROUPS * S5_STATE)


def _s5_blockdiag_out(c):
    x = c.reshape(S5_NBLK, S5_BLK_GROUPS, S5_CH, S5_STATE)
    bd = jnp.einsum('bgcp,gh->bgphc', x, jnp.eye(S5_BLK_GROUPS, dtype=x.dtype))
    return bd.reshape(S5_NBLK, S5_BLK_GROUPS * S5_STATE, S5_BLK_GROUPS * S5_CH)


def _s5_project_in(u, bbr_hi, bbr_lo, bbi_hi, bbi_lo, xre_s, xim_s):
    for blk in range(S5_NBLK):
        uh, ul = _split2(u[:, blk * LANES:(blk + 1) * LANES])
        for hi, lo, dst in ((bbr_hi, bbr_lo, xre_s), (bbi_hi, bbi_lo, xim_s)):
            r = _dot(uh, hi[blk]) + (_dot(uh, lo[blk]) + _dot(ul, hi[blk]))
            for j in range(S5_LG_PER_BLK):
                dst[blk * S5_LG_PER_BLK + j] = r[:, j * LANES:(j + 1) * LANES]


def _s5_scan(xre_s, xim_s, st_re, st_im, ar_ref, ai_ref, *, n_steps, r, store):
    def blk_body(blk, _):
        base = blk * S5_LG_PER_BLK
        ar = [jnp.broadcast_to(ar_ref[base + j], (r, LANES)) for j in range(S5_LG_PER_BLK)]
        ai = [jnp.broadcast_to(ai_ref[base + j], (r, LANES)) for j in range(S5_LG_PER_BLK)]

        def step(s, carry):
            r0 = pl.multiple_of(s * r, r)
            new = []
            for j in range(S5_LG_PER_BLK):
                sr, si = carry[2 * j], carry[2 * j + 1]
                nr = ar[j] * sr - ai[j] * si + xre_s[base + j, pl.ds(r0, r), :]
                ni = ar[j] * si + ai[j] * sr + xim_s[base + j, pl.ds(r0, r), :]
                if store:
                    xre_s[base + j, pl.ds(r0, r), :] = nr
                    xim_s[base + j, pl.ds(r0, r), :] = ni
                new += [nr, ni]
            return tuple(new)

        init = []
        for j in range(S5_LG_PER_BLK):
            init += [st_re[base + j], st_im[base + j]]
        fin = lax.fori_loop(0, n_steps, step, tuple(init))
        for j in range(S5_LG_PER_BLK):
            st_re[base + j] = fin[2 * j]
            st_im[base + j] = fin[2 * j + 1]
        return 0
    lax.fori_loop(0, S5_NBLK, blk_body, 0)


def _s5_local_kernel(x_ref, sc_ref, sh_ref, ng_ref, bbr_hi, bbr_lo, bbi_hi, bbi_lo, ar_ref, ai_ref,
                     lre_o, lim_o, xre_s, xim_s, st_re, st_im, *, n_steps, r):
    @pl.when(pl.program_id(0) == 0)
    def _():
        st_re[...] = jnp.zeros(st_re.shape, F32)
        st_im[...] = jnp.zeros(st_im.shape, F32)
    u = _norm_mod(x_ref[...], ng_ref[...], sc_ref[...], sh_ref[...])
    _s5_project_in(u, bbr_hi, bbr_lo, bbi_hi, bbi_lo, xre_s, xim_s)
    _s5_scan(xre_s, xim_s, st_re, st_im, ar_ref, ai_ref, n_steps=n_steps, r=r, store=False)
    lre_o[...] = st_re[...]
    lim_o[...] = st_im[...]


def _cpow(ar, ai, n):
    rr, ri = None, None
    br, bi = ar, ai
    while n:
        if n & 1:
            if rr is None:
                rr, ri = br, bi
            else:
                rr, ri = rr * br - ri * bi, rr * bi + ri * br
        n >>= 1
        if n:
            br, bi = br * br - bi * bi, 2.0 * br * bi
    return rr, ri


def _s5_full_kernel(x_ref, sc_ref, sh_ref, ng_ref, g1_ref, bbr_hi, bbr_lo, bbi_hi, bbi_lo, ar_ref, ai_ref,
                    cre_ref, cim_ref, d_ref, h0re_ref, h0im_ref, lre_ref, lim_ref, gluw_ref,
                    o_ref, fre_o, fim_o, xre_s, xim_s, st_re, st_im, *, n_steps, r, seg_len):
    @pl.when(pl.program_id(0) == 0)
    def _():
        if seg_len is None:
            st_re[...] = h0re_ref[...]
            st_im[...] = h0im_ref[...]
        else:
            pr, pi = _cpow(ar_ref[...], ai_ref[...], seg_len)
            sr = h0re_ref[:, 0:1, :]
            si = h0im_ref[:, 0:1, :]
            st_re[:, 0:1, :] = sr
            st_im[:, 0:1, :] = si
            for row in range(1, r):
                sr, si = (lre_ref[:, row - 1:row, :] + (pr * sr - pi * si),
                          lim_ref[:, row - 1:row, :] + (pr * si + pi * sr))
                st_re[:, row:row + 1, :] = sr
                st_im[:, row:row + 1, :] = si

    x = x_ref[...]
    u = _norm_mod(x, ng_ref[...], sc_ref[...], sh_ref[...])
    _s5_project_in(u, bbr_hi, bbr_lo, bbi_hi, bbi_lo, xre_s, xim_s)
    _s5_scan(xre_s, xim_s, st_re, st_im, ar_ref, ai_ref, n_steps=n_steps, r=r, store=True)
    fre_o[...] = st_re[...]
    fim_o[...] = st_im[...]
    ys = []
    for blk in range(S5_NBLK):
        acc = None
        for j in range(S5_LG_PER_BLK):
            lg = blk * S5_LG_PER_BLK + j
            t = (_dot(xre_s[lg].astype(BF16), cre_ref[blk, j * LANES:(j + 1) * LANES, :])
                 - _dot(xim_s[lg].astype(BF16), cim_ref[blk, j * LANES:(j + 1) * LANES, :]))
            acc = t if acc is None else acc + t
        ys.append(acc)
    y = jnp.concatenate(ys, axis=-1) + d_ref[...] * u
    z = _dot(_gelu_tanh(y).astype(BF16), gluw_ref[...])
    dm = x.shape[-1]
    mix = z[:, :dm] * _sigmoid(z[:, dm:])
    o_ref[...] = x + g1_ref[...] * mix


def _s5_weight_specs(ws):
    return [_resident(w.shape) for w in ws]


def _s5_local(x, sc, sh, ng, bb, a3, *, r, rows):
    n, d = x.shape
    n_steps = rows // r
    sds = jax.ShapeDtypeStruct((S5_LG, r, LANES), F32)
    return pl.pallas_call(
        functools.partial(_s5_local_kernel, n_steps=n_steps, r=r),
        grid=(n // rows,),
        in_specs=[pl.BlockSpec((rows, d), lambda i: (i, 0)), _mod_spec(sc, rows, d), _mod_spec(sh, rows, d),
                  _resident((1, d))] + _s5_weight_specs(bb + a3),
        out_specs=[_resident(sds.shape)] * 2,
        out_shape=[sds, sds],
        scratch_shapes=[pltpu.VMEM((S5_LG, rows, LANES), F32)] * 2 + [pltpu.VMEM((S5_LG, r, LANES), F32)] * 2,
        compiler_params=_cp(("arbitrary",)),
        name="s5_local_scan",
    )(x, sc, sh, ng, *bb, *a3)


def _s5_full(x, sc, sh, ng, g1, bb, a3, cre, cim, dskip, h0re, h0im, lre, lim, gluw, *, r, rows, seg_len):
    n, d = x.shape
    n_steps = rows // r
    st = jax.ShapeDtypeStruct((S5_LG, r, LANES), F32)
    return pl.pallas_call(
        functools.partial(_s5_full_kernel, n_steps=n_steps, r=r, seg_len=seg_len),
        grid=(n // rows,),
        in_specs=[pl.BlockSpec((rows, d), lambda i: (i, 0)), _mod_spec(sc, rows, d), _mod_spec(sh, rows, d),
                  _resident((1, d)), _mod_spec(g1, rows, d)]
                 + _s5_weight_specs(bb + a3 + [cre, cim, dskip, h0re, h0im, lre, lim, gluw]),
        out_specs=[pl.BlockSpec((rows, d), lambda i: (i, 0)), _resident(st.shape), _resident(st.shape)],
        out_shape=[jax.ShapeDtypeStruct((n, d), F32), st, st],
        scratch_shapes=[pltpu.VMEM((S5_LG, rows, LANES), F32)] * 2 + [pltpu.VMEM((S5_LG, r, LANES), F32)] * 2,
        compiler_params=_cp(("arbitrary",)),
        name="s5_scan_glu",
    )(x, sc, sh, ng, g1, *bb, *a3, cre, cim, dskip, h0re, h0im, lre, lim, gluw)


def _to_lane_groups(s):
    r = s.shape[0]
    return jnp.transpose(s.reshape(r, S5_LG, LANES), (1, 0, 2))


def _from_lane_groups(s):
    r = s.shape[1]
    return jnp.transpose(s, (1, 0, 2)).reshape(r, S5_W)


def _moe_kernel(x_ref, sc_ref, sh_ref, g2_ref, ng_ref, rw_hi, rw_lo, rb_ref, w1_ref, w3_ref, w2_ref,
                o_ref, h_s, gate_s, acc_s):
    e = pl.program_id(1)
    lane = lax.broadcasted_iota(I32, gate_s.shape, 1)

    @pl.when(e == 0)
    def _():
        h = _norm_mod(x_ref[...], ng_ref[...], sc_ref[...], sh_ref[...])
        h_s[...] = h.astype(BF16)
        hh, hl = _split2(h)
        logits = _dot(hh, rw_hi[...]) + (_dot(hh, rw_lo[...]) + _dot(hl, rw_hi[...])) + rb_ref[...]
        logits = jnp.where(lane < N_EXPERTS, logits, -jnp.inf)
        m1 = jnp.max(logits, axis=-1, keepdims=True)
        i1 = jnp.min(jnp.where(logits == m1, lane, LANES), axis=-1, keepdims=True)
        rest = jnp.where(lane == i1, -jnp.inf, logits)
        m2 = jnp.max(rest, axis=-1, keepdims=True)
        i2 = jnp.min(jnp.where(rest == m2, lane, LANES), axis=-1, keepdims=True)
        e2 = jnp.exp(m2 - m1)
        den = 1.0 + e2
        gate_s[...] = jnp.where(lane == i1, 1.0 / den, 0.0) + jnp.where(lane == i2, e2 / den, 0.0)
        acc_s[...] = jnp.zeros(acc_s.shape, F32)

    h = h_s[...]
    ge = jnp.sum(jnp.where(lane == e, gate_s[...], 0.0), axis=-1, keepdims=True)
    y = _dot((_silu(_dot(h, w1_ref[...])) * _dot(h, w3_ref[...])).astype(BF16), w2_ref[...])
    acc_s[...] += ge * y

    @pl.when(e == N_EXPERTS - 1)
    def _():
        o_ref[...] = x_ref[...] + g2_ref[...] * acc_s[...]


def _moe(x, sc, sh, g2, ng, rw_hi, rw_lo, rb, w1, w3, w2, *, tm):
    n, d = x.shape
    ne, _, ff = w1.shape
    mod = lambda m: (pl.BlockSpec((1, d), lambda i, e: (0, 0)) if m.shape[0] == 1
                     else pl.BlockSpec((tm, d), lambda i, e: (i, 0)))
    const = lambda shape: pl.BlockSpec(shape, lambda i, e: (0,) * len(shape), pipeline_mode=pl.Buffered(1))
    return pl.pallas_call(
        _moe_kernel,
        grid=(n // tm, ne),
        in_specs=[pl.BlockSpec((tm, d), lambda i, e: (i, 0)), mod(sc), mod(sh), mod(g2), const((1, d)),
                  const(rw_hi.shape), const(rw_lo.shape), const(rb.shape),
                  pl.BlockSpec((None, d, ff), lambda i, e: (e, 0, 0)),
                  pl.BlockSpec((None, d, ff), lambda i, e: (e, 0, 0)),
                  pl.BlockSpec((None, ff, d), lambda i, e: (e, 0, 0))],
        out_specs=pl.BlockSpec((tm, d), lambda i, e: (i, 0)),
        out_shape=jax.ShapeDtypeStruct((n, d), F32),
        scratch_shapes=[pltpu.VMEM((tm, d), BF16), pltpu.VMEM((tm, LANES), F32), pltpu.VMEM((tm, d), F32)],
        compiler_params=_cp(("arbitrary", "arbitrary")),
        name="moe_dense",
    )(x, sc, sh, g2, ng, rw_hi, rw_lo, rb, w1, w3, w2)


def _sidx_kernel(pt_ref, qi_ref, w_ref, *rest, n_pages, group, topk, t_new):
    del pt_ref
    page_refs, (ikn_ref, bias_ref, sc_ref) = rest[:group], rest[group:]
    step = pl.program_id(1)
    rows = t_new

    def page_scores(keys):
        r = jnp.maximum(_dot_x3(qi_ref[...], keys, nt=True), 0.0) * w_ref[...]
        acc = r[0:rows]
        for h in range(1, N_HEADS):
            acc = acc + r[h * rows:(h + 1) * rows]
        return acc

    for g in range(group):
        sc_ref[step * group + g] = page_scores(page_refs[g][...])

    @pl.when(step == n_pages // group - 1)
    def _():
        row = lax.broadcasted_iota(I32, (rows, PAGE), 0)
        col = lax.broadcasted_iota(I32, (rows, PAGE), 1)
        sc_ref[n_pages] = jnp.where(col <= row, page_scores(ikn_ref[...]), -jnp.inf)

        def counter(cmp):
            def count(thr):
                def body(j, c):
                    return c + jnp.where(cmp(sc_ref[j], thr), 1.0, 0.0)
                c = lax.fori_loop(0, n_pages + 1, body, jnp.zeros((rows, PAGE), F32))
                return jnp.sum(c, axis=-1, keepdims=True)
            return count

        thr = _kth_largest(counter(lambda s, t: s >= t), (rows, 1), topk)
        need = topk - counter(lambda s, t: s > t)(thr)
        upper = jnp.where(lax.broadcasted_iota(I32, (PAGE, PAGE), 0) < lax.broadcasted_iota(I32, (PAGE, PAGE), 1),
                          1.0, 0.0).astype(BF16)

        def select(j, ties_before):
            s = sc_ref[j]
            eq = jnp.where(s == thr, 1.0, 0.0)
            rank = _dot(eq.astype(BF16), upper) + ties_before
            keep = jnp.where(s > thr, 0.0, jnp.where(s == thr, jnp.where(rank < need, 0.0, NEG), NEG))
            bias_ref[j] = jnp.where(s == -jnp.inf, NEG, keep)
            return ties_before + jnp.sum(eq, axis=-1, keepdims=True)
        lax.fori_loop(0, n_pages + 1, select, jnp.zeros((rows, 1), F32))


def _page_spec(width, group, g):
    return pl.BlockSpec((None, PAGE, width), lambda b, s, pt: (pt[b, s * group + g], 0, 0))


def _sample_indexer(page_table, qi_rows, w_rows, cache_ik, ki_new, *, topk, t_new, group):
    bd, n_pages = page_table.shape
    hq = qi_rows.shape[1]
    group = min(group, n_pages)
    grid_spec = pltpu.PrefetchScalarGridSpec(
        num_scalar_prefetch=1,
        grid=(bd, n_pages // group),
        in_specs=[pl.BlockSpec((None, hq, HEAD_DIM), lambda b, s, pt: (b, 0, 0)),
                  pl.BlockSpec((None, hq, 1), lambda b, s, pt: (b, 0, 0))]
                 + [_page_spec(HEAD_DIM, group, g) for g in range(group)]
                 + [pl.BlockSpec((None, PAGE, HEAD_DIM), lambda b, s, pt: (b, 0, 0))],
        out_specs=pl.BlockSpec((None, n_pages + 1, t_new, PAGE), lambda b, s, pt: (b, 0, 0, 0)),
        scratch_shapes=[pltpu.VMEM((n_pages + 1, t_new, PAGE), F32)])
    return pl.pallas_call(
        functools.partial(_sidx_kernel, n_pages=n_pages, group=group, topk=topk, t_new=t_new),
        grid_spec=grid_spec,
        out_shape=jax.ShapeDtypeStruct((bd, n_pages + 1, t_new, PAGE), F32),
        compiler_params=_cp(("arbitrary", "arbitrary")),
        name="sample_indexer",
    )(page_table, qi_rows, w_rows, *([cache_ik] * group), ki_new)


def _sattn_kernel(pt_ref, q_ref, *rest, n_pages, group, t_new):
    del pt_ref
    k_refs, v_refs = rest[:group], rest[group:2 * group]
    kn_ref, vn_ref, mb_ref, tabl_ref, tabn_ref, o_ref, m_s, l_s, acc_s = rest[2 * group:]
    step = pl.program_id(1)
    n_steps = n_pages // group

    @pl.when(step == 0)
    def _():
        m_s[...] = jnp.full(m_s.shape, NEG, F32)
        l_s[...] = jnp.zeros(l_s.shape, F32)
        acc_s[...] = jnp.zeros(acc_s.shape, F32)

    def logits(keys, mask_row, table):
        s = _dot_nt(q_ref[...], keys) + jnp.concatenate([mb_ref[mask_row]] * N_HEADS, axis=0)
        return s if table is None else s + table

    def update(last):
        ks = [r[...] for r in k_refs] + ([kn_ref[...].astype(BF16)] if last else [])
        vs = [r[...] for r in v_refs] + ([vn_ref[...].astype(BF16)] if last else [])
        ss = [logits(ks[g], step * group + g, tabl_ref[...] if (last and g == group - 1) else None)
              for g in range(group)]
        if last:
            ss.append(logits(ks[group], n_pages, tabn_ref[...]))
        m_old = m_s[...]
        m_new = m_old
        for s in ss:
            m_new = jnp.maximum(m_new, jnp.max(s, axis=-1, keepdims=True))
        alpha = jnp.exp(m_old - m_new)
        l_new = alpha * l_s[...]
        acc = alpha * acc_s[...]
        for s, v in zip(ss, vs):
            pexp = jnp.exp(s - m_new)
            l_new = l_new + jnp.sum(pexp, axis=-1, keepdims=True)
            acc = acc + _dot(pexp.astype(BF16), v)
        m_s[...] = m_new
        l_s[...] = l_new
        acc_s[...] = acc

    @pl.when(step < n_steps - 1)
    def _():
        update(False)

    @pl.when(step == n_steps - 1)
    def _():
        update(True)
        r = acc_s[...] / l_s[...]
        lane_head = lax.broadcasted_iota(I32, (t_new, A_W), 1) // HEAD_DIM
        out = jnp.zeros((t_new, A_W), F32)
        for h in range(N_HEADS):
            out = out + jnp.where(lane_head == h, r[h * t_new:(h + 1) * t_new], 0.0)
        o_ref[...] = out.astype(o_ref.dtype)


def _sample_attention(page_table, q_bd, cache_k, cache_v, k_new, v_new, mask, tab_last, tab_new, *, t_new, group):
    bd, n_pages = page_table.shape
    hq = q_bd.shape[1]
    group = min(group, n_pages)
    per_b = lambda b, s, pt: (b, 0, 0)
    const2 = lambda b, s, pt: (0, 0)
    pages = [_page_spec(A_W, group, g) for g in range(group)]
    grid_spec = pltpu.PrefetchScalarGridSpec(
        num_scalar_prefetch=1,
        grid=(bd, n_pages // group),
        in_specs=[pl.BlockSpec((None, hq, A_W), per_b)] + pages + pages
                 + [pl.BlockSpec((None, PAGE, A_W), per_b), pl.BlockSpec((None, PAGE, A_W), per_b),
                    pl.BlockSpec((None, n_pages + 1, t_new, PAGE), lambda b, s, pt: (b, 0, 0, 0)),
                    pl.BlockSpec((hq, PAGE), const2), pl.BlockSpec((hq, PAGE), const2)],
        out_specs=pl.BlockSpec((None, t_new, A_W), per_b),
        scratch_shapes=[pltpu.VMEM((hq, 1), F32), pltpu.VMEM((hq, 1), F32), pltpu.VMEM((hq, A_W), F32)])
    return pl.pallas_call(
        functools.partial(_sattn_kernel, n_pages=n_pages, group=group, t_new=t_new),
        grid_spec=grid_spec,
        out_shape=jax.ShapeDtypeStruct((bd, t_new, A_W), BF16),
        compiler_params=_cp(("arbitrary", "arbitrary")),
        name="sample_attention",
    )(page_table, q_bd, *([cache_k] * group), *([cache_v] * group), k_new, v_new, mask, tab_last, tab_new)


ATTN_TILE = 256
ATTN_CHUNK = 2048
IDX_PAGE_GROUP = 16
ATTN_PAGE_GROUP = 8
S5_ROWS = 256
S5_SEGMENTS = 8


def _hi_lo(w):
    hi = w.astype(BF16)
    return hi, (w - hi.astype(F32)).astype(BF16)


def _even_layer_front(x, sc1, sh1, p, *, tm, qi_dtype):
    return _inproj(x, sc1, sh1, p['ng0'], p['w_in'], p['gq_pad'], p['gk_tiled'], p['gamma'],
                   tm=tm, qi_dtype=qi_dtype)


def _odd_layer(x, m, p, h0re, h0im, *, r, seg_len):
    sh1, sc1, g1, sh2, sc2, g2 = m
    zeros = jnp.zeros((S5_LG, r, LANES), F32)
    if seg_len is None:
        lre, lim = zeros, zeros
    else:
        lre, lim = _s5_local(x, sc1, sh1, p['ng1'], p['bb'], p['a3'], r=r, rows=S5_ROWS)
    x, fre, fim = _s5_full(x, sc1, sh1, p['ng1'], g1, p['bb'], p['a3'], p['cre'], p['cim'], p['dskip'],
                           h0re, h0im, lre, lim, p['gluw'], r=r, rows=S5_ROWS, seg_len=seg_len)
    x = _moe(x, sc2, sh2, g2, p['ng1b'], p['rw_hi'], p['rw_lo'], p['rb'], p['mw1'], p['mw3'], p['mw2'],
             tm=min(512, x.shape[0]))
    return x, _from_lane_groups(fre), _from_lane_groups(fim)


def kernel(x_prompt, x_sample, c_prompt, c_sample, cache_k, cache_v, cache_idx_k, state_hgrn, state_s5_re, state_s5_im, page_table, rel_bias, ada_w, ada_b, norm_g, w_in, qk_norm_g, hgrn_gamma, hgrn_onorm_g, w_out, ffn_w1, ffn_w3, ffn_w2, s5_lambda_re, s5_lambda_im, s5_log_dt, s5_b_re, s5_b_im, s5_c_re, s5_c_im, s5_d, s5_glu_w, moe_router_w, moe_router_b, moe_w1, moe_w3, moe_w2):
    bp, seq, d = x_prompt.shape
    bd, t_new, _ = x_sample.shape
    n_dec = bd * t_new
    n_phys = cache_k.shape[1]
    past = page_table.shape[1] * PAGE
    assert bp == 1 and d == D_MODEL and seq % 512 == 0
    assert n_dec == S5_ROWS and t_new == 8

    g_q, g_k = qk_norm_g[0, 0], qk_norm_g[0, 1]
    ar, ai, bbr_t, bbi_t = _s5_prep(s5_lambda_re[0], s5_lambda_im[0], s5_log_dt[0], s5_b_re[0], s5_b_im[0])
    bb = []
    for w in (_s5_blockdiag_in(bbr_t), _s5_blockdiag_in(bbi_t)):
        bb += list(_hi_lo(w))
    rw = jnp.pad(moe_router_w[0], ((0, 0), (0, LANES - N_EXPERTS)))
    rw_hi, rw_lo = _hi_lo(rw)
    p = dict(
        ng0=norm_g[0, 0][None], ng0b=norm_g[0, 1][None], ng1=norm_g[1, 0][None], ng1b=norm_g[1, 1][None],
        w_in=_pack_w_in(w_in[0]), gq_pad=jnp.tile(g_q, 2 * N_HEADS)[None], gk_tiled=jnp.tile(g_k, N_HEADS)[None],
        gamma=hgrn_gamma, onorm=jnp.tile(hgrn_onorm_g[0], N_HEADS)[None],
        wa=w_out[0, :A_W].astype(BF16), wb=w_out[0, A_W:].astype(BF16),
        w1=ffn_w1[0].astype(BF16), w3=ffn_w3[0].astype(BF16), w2=ffn_w2[0].astype(BF16),
        bb=bb, a3=[ar.reshape(S5_LG, 1, LANES), ai.reshape(S5_LG, 1, LANES)],
        cre=_s5_blockdiag_out(s5_c_re[0]).astype(BF16), cim=_s5_blockdiag_out(s5_c_im[0]).astype(BF16),
        dskip=s5_d[0][None], gluw=s5_glu_w[0].astype(BF16),
        rw_hi=rw_hi, rw_lo=rw_lo, rb=jnp.pad(moe_router_b[0], (0, LANES - N_EXPERTS))[None],
        mw1=moe_w1[0].astype(BF16), mw3=moe_w3[0].astype(BF16), mw2=moe_w2[0].astype(BF16),
    )

    c_rows = bp + bd
    c_all = jnp.concatenate([c_prompt, c_sample, jnp.zeros((-c_rows % 8, d), F32)], axis=0)
    mod = _ada(c_all, ada_w, ada_b)

    def mods(layer, rows, expand):
        m = mod[layer, rows]
        return tuple(expand(m[:, i * d:(i + 1) * d]) for i in range(6))

    xp = x_prompt.reshape(seq, d)
    sh1, sc1, g1, sh2, sc2, g2 = mods(0, slice(0, 1), lambda a: a)
    (q_pad, k_f, k_b, v_f, v_b, qi, kw, gq, gk, gl, gi, gg) = _even_layer_front(
        xp, sc1, sh1, p, tm=512, qi_dtype=BF16)
    ki_p = kw[:, :HEAD_DIM]
    mask = _prompt_indexer(qi.T, kw[:, HEAD_DIM:HEAD_DIM + N_HEADS].T, ki_p.astype(BF16),
                           tq=ATTN_TILE, tk=ATTN_TILE, topk=min(TOPK_MAX, seq // 4))
    a_out = _prompt_attention(q_pad.T, k_b, v_b.T, mask, rel_bias, tile=ATTN_TILE, chunk=ATTN_CHUNK).T
    b_out, st_p = _gla(gq, gk, gi, gl, gg, jnp.zeros((1, A_W, A_W), F32), p['onorm'], batch=1, chunk=64)
    xp = _even_tail(xp, a_out, b_out, g1, sc2, sh2, g2, p['ng0b'], p['wa'], p['wb'], p['w1'], p['w3'], p['w2'],
                    tm=512)
    seg_len = seq // S5_SEGMENTS
    to_seg = lambda a: a.reshape(S5_SEGMENTS, seg_len, d).transpose(1, 0, 2).reshape(seq, d)
    zero_state = jnp.zeros((S5_LG, S5_SEGMENTS, LANES), F32)
    xp, fre_p, fim_p = _odd_layer(to_seg(xp), mods(1, slice(0, 1), lambda a: a), p, zero_state, zero_state,
                                  r=S5_SEGMENTS, seg_len=seg_len)
    y_prompt = xp.reshape(seg_len, S5_SEGMENTS, d).transpose(1, 0, 2).reshape(bp, seq, d)

    xs = x_sample.reshape(n_dec, d)
    per_token = lambda a: jnp.repeat(a, t_new, axis=0)
    sh1, sc1, g1, sh2, sc2, g2 = mods(0, slice(bp, bp + bd), per_token)
    (q_pad_s, k_fs, _, v_fs, _, qi_s, kw_s, gq, gk, gl, gi, gg) = _even_layer_front(
        xs, sc1, sh1, p, tm=n_dec, qi_dtype=F32)
    heads_first = lambda a: jnp.transpose(a, (0, 2, 1, 3)).reshape(bd, N_HEADS * t_new, a.shape[-1])
    qi_rows = heads_first(qi_s.reshape(bd, t_new, N_HEADS, HEAD_DIM))
    w_rows = heads_first(kw_s[:, HEAD_DIM:HEAD_DIM + N_HEADS].reshape(bd, t_new, N_HEADS, 1))
    pad_new = lambda a: jnp.pad(a.reshape(bd, t_new, -1), ((0, 0), (0, PAGE - t_new), (0, 0)))
    ki_s = kw_s[:, :HEAD_DIM]
    mask_s = _sample_indexer(page_table, qi_rows, w_rows, cache_idx_k[0], pad_new(ki_s),
                             topk=min(TOPK_MAX, (past + t_new) // 4), t_new=t_new, group=IDX_PAGE_GROUP)
    q4 = q_pad_s.reshape(bd, t_new, N_HEADS, LANES)
    even = (jnp.arange(N_HEADS) % 2 == 0)[None, None, :, None]
    q_nat = jnp.where(even, q4[..., :HEAD_DIM], q4[..., HEAD_DIM:])
    q_bd = jnp.einsum('bthd,hg->bhtgd', q_nat, jnp.eye(N_HEADS, dtype=q_nat.dtype))
    q_bd = q_bd.reshape(bd, N_HEADS * t_new, A_W)
    qpos = np.arange(t_new)[:, None]
    col = np.arange(PAGE)[None, :]
    rows_hq = lambda t: t.reshape(N_HEADS * t_new, PAGE)
    tab_last = rows_hq(_bias_by_distance(rel_bias, PAGE + qpos - col))
    tab_new = rows_hq(_bias_by_distance(rel_bias, qpos - col))
    rows_bf16 = lambda c: c[0].reshape(n_phys, PAGE, A_W).astype(BF16)
    a_out_s = _sample_attention(page_table, q_bd, rows_bf16(cache_k), rows_bf16(cache_v),
                                pad_new(k_fs), pad_new(v_fs), mask_s, tab_last, tab_new,
                                t_new=t_new, group=ATTN_PAGE_GROUP).reshape(n_dec, A_W)
    gla_chunk = 64
    pad_t = lambda a: jnp.pad(a.reshape(bd, t_new, A_W), ((0, 0), (0, gla_chunk - t_new), (0, 0))).reshape(-1, A_W)
    b_out_s, st_s = _gla(pad_t(gq), pad_t(gk), pad_t(gi), pad_t(gl), pad_t(gg),
                         _state_to_blockdiag_t(state_hgrn[0]), p['onorm'], batch=bd, chunk=gla_chunk)
    b_out_s = b_out_s.reshape(bd, gla_chunk, A_W)[:, :t_new].reshape(n_dec, A_W)
    xs = _even_tail(xs, a_out_s, b_out_s, g1, sc2, sh2, g2, p['ng0b'], p['wa'], p['wb'],
                    p['w1'], p['w3'], p['w2'], tm=n_dec)
    step_major = lambda a: a.reshape(bd, t_new, d).transpose(1, 0, 2).reshape(n_dec, d)
    xs, fre_s, fim_s = _odd_layer(step_major(xs), mods(1, slice(bp, bp + bd), lambda a: jnp.tile(a, (t_new, 1))), p,
                                  _to_lane_groups(state_s5_re[0].reshape(bd, S5_W)),
                                  _to_lane_groups(state_s5_im[0].reshape(bd, S5_W)), r=bd, seg_len=None)
    y_sample = xs.reshape(t_new, bd, d).transpose(1, 0, 2)

    heads = lambda a, b, t: a.reshape(1, b, t, N_HEADS, HEAD_DIM)
    s5_state = lambda f: f.reshape(1, -1, S5_GROUPS, S5_STATE)
    return (y_prompt, y_sample,
            heads(k_f, bp, seq), heads(v_f, bp, seq), ki_p.reshape(1, bp, seq, HEAD_DIM),
            _blockdiag_t_to_state(st_p)[None], s5_state(fre_p[S5_SEGMENTS - 1:]), s5_state(fim_p[S5_SEGMENTS - 1:]),
            heads(k_fs, bd, t_new), heads(v_fs, bd, t_new), ki_s.reshape(1, bd, t_new, HEAD_DIM),
            _blockdiag_t_to_state(st_s)[None], s5_state(fre_s), s5_state(fim_s))
```

```python
import functools
import math

import numpy as np
import jax
import jax.numpy as jnp
from jax import lax
from jax.experimental import pallas as pl
from jax.experimental.pallas import tpu as pltpu

F32 = jnp.float32
BF16 = jnp.bfloat16
I32 = jnp.int32

D_MODEL = 1024
N_HEADS = 8
HEAD_DIM = 64
A_W = N_HEADS * HEAD_DIM
LANES = 128
TOPK_MAX = 256
PAGE = 128
REL_BUCKETS = 32
REL_MAX_DIST = 128
S5_GROUPS = 64
S5_CH = 16
S5_STATE = 64
S5_W = S5_GROUPS * S5_STATE
S5_BLK_GROUPS = 8
S5_NBLK = S5_GROUPS // S5_BLK_GROUPS
D_FF = 2816
N_EXPERTS = 8
MOE_FF = 1408
EPS = 1e-6
NEG = -1e30
INT_MIN = -2 ** 31
VMEM_LIMIT = 56 * 2 ** 20


def _cp(sem, vmem=VMEM_LIMIT):
    return pltpu.CompilerParams(dimension_semantics=sem, vmem_limit_bytes=vmem)


def _resident(shape):
    n = len(shape)
    return pl.BlockSpec(shape, lambda *_: (0,) * n, pipeline_mode=pl.Buffered(1))


def _dot(a, b):
    return jnp.dot(a, b, preferred_element_type=F32)


def _dot_nt(a, b):
    return lax.dot_general(a, b, (((1,), (1,)), ((), ())), preferred_element_type=F32)


def _split2(x):
    hi = x.astype(BF16)
    lo = (x - hi.astype(F32)).astype(BF16)
    return hi, lo


def _dot_x3(a, b, nt=False):
    d = _dot_nt if nt else _dot
    ah, al = _split2(a)
    bh, bl = _split2(b)
    return d(ah, bh) + (d(ah, bl) + d(al, bh))


def _sigmoid(x):
    return 1.0 / (1.0 + jnp.exp(-x))


def _silu(x):
    return x * _sigmoid(x)


def _gelu_tanh(x):
    return 0.5 * x * (1.0 + jnp.tanh(math.sqrt(2.0 / math.pi) * (x + 0.044715 * (x * x * x))))


def _norm_mod(x, g, sc, sh):
    ms = jnp.mean(x * x, axis=-1, keepdims=True)
    return (x * lax.rsqrt(ms + EPS) * g) * (1.0 + sc) + sh


def _seg_matrix(n, seg, dtype, scale=1.0):
    r = lax.broadcasted_iota(I32, (n, n), 0) // seg
    c = lax.broadcasted_iota(I32, (n, n), 1) // seg
    return jnp.where(r == c, scale, 0.0).astype(dtype)


def _ada_kernel(c_ref, w_ref, b_ref, o_ref):
    c = c_ref[...]
    o_ref[...] = _dot_x3(_silu(c), w_ref[...]) + b_ref[...]


def _ada(c_all, ada_w, ada_b):
    depth, d, n6 = ada_w.shape
    rows = c_all.shape[0]
    tn = 1536
    return pl.pallas_call(
        _ada_kernel,
        grid=(depth, n6 // tn),
        in_specs=[pl.BlockSpec((rows, d), lambda l, j: (0, 0)),
                  pl.BlockSpec((None, d, tn), lambda l, j: (l, 0, j)),
                  pl.BlockSpec((None, 1, tn), lambda l, j: (l, 0, j))],
        out_specs=pl.BlockSpec((None, rows, tn), lambda l, j: (l, 0, j)),
        out_shape=jax.ShapeDtypeStruct((depth, rows, n6), F32),
        compiler_params=_cp(("arbitrary", "arbitrary")),
        name="ada_mod",
    )(c_all, ada_w, ada_b.reshape(depth, 1, n6))


W_OFF_Q, W_OFF_K, W_OFF_V, W_OFF_QI, W_OFF_KW, W_OFF_B = 0, 1024, 1536, 2048, 2560, 2688


def _pack_w_in(w):
    d = w.shape[0]
    z64 = jnp.zeros((d, N_HEADS, HEAD_DIM), w.dtype)
    q = w[:, 0:512].reshape(d, N_HEADS, HEAD_DIM)
    even = (jnp.arange(N_HEADS) % 2 == 0)[None, :, None]
    q_pad = jnp.concatenate([jnp.where(even, q, z64), jnp.where(even, z64, q)], axis=-1).reshape(d, 1024)
    kw = jnp.concatenate([w[:, 2048:2120], jnp.zeros((d, 56), w.dtype)], axis=-1)
    return jnp.concatenate([q_pad, w[:, 512:2048], kw, w[:, 2120:4168]], axis=-1).astype(BF16)


def _inproj_kernel(x_ref, sc_ref, sh_ref, ng_ref, w_ref, gq_ref, gk_ref, gam_ref,
                   q_ref, kf_ref, kb_ref, vf_ref, vb_ref, qi_ref, kw_ref,
                   gq_o, gk_o, gl_o, gi_o, gg_o):
    h = _norm_mod(x_ref[...], ng_ref[...], sc_ref[...], sh_ref[...]).astype(BF16)
    ones_seg = jnp.full((LANES, LANES), 1.0 / HEAD_DIM, BF16)
    pair_seg = _seg_matrix(LANES, HEAD_DIM, BF16, 1.0 / HEAD_DIM)

    def proj(off, n):
        return _dot(h, w_ref[:, off:off + n])

    pq = proj(W_OFF_Q, 1024)
    for j in range(8):
        blk = pq[:, j * LANES:(j + 1) * LANES]
        ms = _dot((blk * blk).astype(BF16), ones_seg)
        q_ref[:, j * LANES:(j + 1) * LANES] = (
            blk * lax.rsqrt(ms + EPS) * gq_ref[:, j * LANES:(j + 1) * LANES] * 0.125).astype(q_ref.dtype)
    pk = proj(W_OFF_K, 512)
    for j in range(4):
        blk = pk[:, j * LANES:(j + 1) * LANES]
        ms = _dot((blk * blk).astype(BF16), pair_seg)
        kn = blk * lax.rsqrt(ms + EPS) * gk_ref[:, j * LANES:(j + 1) * LANES]
        kf_ref[:, j * LANES:(j + 1) * LANES] = kn
        kb_ref[:, j * LANES:(j + 1) * LANES] = kn.astype(BF16)
    pv = proj(W_OFF_V, 512)
    vf_ref[...] = pv
    vb_ref[...] = pv.astype(BF16)
    qi_ref[...] = (proj(W_OFF_QI, 512) * 0.125).astype(qi_ref.dtype)
    lane = lax.broadcasted_iota(I32, (1, LANES), 1)
    kw_ref[...] = proj(W_OFF_KW, LANES) * jnp.where(lane < HEAD_DIM, 1.0, N_HEADS ** -0.5)
    gq_o[...] = proj(W_OFF_B, 512) * 0.125
    gam = gam_ref[...]
    gmax = jnp.max(gam, axis=0, keepdims=True)
    ge = jnp.exp(gam - gmax)
    lb = ge[0:1, :] / jnp.sum(ge, axis=0, keepdims=True)
    f = lb + (1.0 - lb) * _sigmoid(proj(W_OFF_B + 512, 512))
    gk_o[...] = 1.0 - f
    gl_o[...] = jnp.log(f)
    gi_o[...] = proj(W_OFF_B + 1024, 512)
    gg_o[...] = proj(W_OFF_B + 1536, 512)


def _inproj(x, sc, sh, ng, w_packed, gq_pad, gk_tiled, gamma, *, tm, qi_dtype):
    n, d = x.shape
    row = lambda i: (i, 0)
    mod_spec = (pl.BlockSpec((1, d), lambda i: (0, 0)) if sc.shape[0] == 1
                else pl.BlockSpec((tm, d), row))
    outs = [((n, 1024), BF16), ((n, 512), F32), ((n, 512), BF16), ((n, 512), F32), ((n, 512), BF16),
            ((n, 512), qi_dtype), ((n, LANES), F32)] + [((n, 512), F32)] * 5
    return pl.pallas_call(
        _inproj_kernel,
        grid=(n // tm,),
        in_specs=[pl.BlockSpec((tm, d), row), mod_spec, mod_spec,
                  _resident((1, d)), _resident(w_packed.shape), _resident((1, 1024)),
                  _resident((1, 512)), _resident(gamma.shape)],
        out_specs=[pl.BlockSpec((tm, s[1]), row) for s, _ in outs],
        out_shape=[jax.ShapeDtypeStruct(s, dt) for s, dt in outs],
        compiler_params=_cp(("arbitrary",)),
        name="in_proj",
    )(x, sc, sh, ng, w_packed, gq_pad, gk_tiled, gamma)


def _key_to_f32(key):
    neg = key < 0
    mag = jnp.where(neg, -key, key)
    bits = jnp.where(neg, mag | jnp.int32(INT_MIN), mag)
    f = lax.bitcast_convert_type(bits, F32)
    return jnp.where(key == jnp.int32(INT_MIN), -jnp.inf, f)


def _f32_to_key(x):
    bits = lax.bitcast_convert_type(x, I32)
    return jnp.where(bits < 0, -(bits & jnp.int32(0x7FFFFFFF)), bits)


def _kth_largest(count_ge, shape, topk, bounds=None):
    if bounds is None:
        first_bit, base0 = jnp.int32(0), jnp.full(shape, INT_MIN, I32)
    else:
        lo_u = _f32_to_key(bounds[0]) ^ jnp.int32(INT_MIN)
        hi_u = _f32_to_key(bounds[1]) ^ jnp.int32(INT_MIN)
        first_bit = jnp.min(lax.clz(lo_u ^ hi_u))
        low = jnp.where(first_bit == 0, jnp.int32(-1),
                        lax.shift_left(jnp.int32(1), jnp.maximum(32 - first_bit, 0) & 31) - 1)
        low = jnp.where(first_bit == 32, jnp.int32(0), low)
        base0 = (hi_u & ~low) ^ jnp.int32(INT_MIN)

    def cond(state):
        i, _, done, _ = state
        return jnp.logical_and(i < 32, jnp.min(done) == 0)

    def body(state):
        i, base, done, thr = state
        cand = base + lax.shift_left(jnp.int32(1), 31 - i)
        cand_f = _key_to_f32(cand)
        cnt = count_ge(cand_f)
        hit = jnp.logical_and(cnt == topk, done == 0)
        thr = jnp.where(hit, cand_f, thr)
        done = jnp.where(hit, 1, done)
        return i + 1, jnp.where(cnt >= topk, cand, base), done, thr

    init = (first_bit, base0, jnp.zeros(shape, I32), jnp.zeros(shape, F32))
    _, base, done, thr = lax.while_loop(cond, body, init)
    return jnp.where(done == 1, thr, _key_to_f32(base))


def _pidx_kernel(qit_ref, wt_ref, ki_ref, bias_ref, sc_ref, cm_ref, *, tq, tk, topk, seq):
    assert topk <= tk
    q0 = pl.program_id(0) * tq
    n_kt = (q0 + tq + tk - 1) // tk
    key = lax.broadcasted_iota(I32, (tk, tq), 0)
    qry = lax.broadcasted_iota(I32, (tk, tq), 1) + q0

    def tile_off(kt):
        return pl.multiple_of(kt * tk, tk)

    def scores(kt, _):
        off = tile_off(kt)
        ks = ki_ref[pl.ds(off, tk), :]
        acc = jnp.zeros((tk, tq), F32)
        for h in range(N_HEADS):
            x = _dot(ks, qit_ref[h * HEAD_DIM:(h + 1) * HEAD_DIM, :])
            acc = acc + wt_ref[h:h + 1, :] * jnp.maximum(x, 0.0)
        acc = jnp.where(key + off <= qry, acc, -jnp.inf)
        sc_ref[pl.ds(off, tk), :] = acc
        cm_ref[...] = jnp.maximum(cm_ref[...], acc)
        return 0
    cm_ref[...] = jnp.full((tk, tq), -jnp.inf, F32)
    lax.fori_loop(0, n_kt, scores, 0)
    bounds = (jnp.min(cm_ref[...], axis=0, keepdims=True), jnp.max(cm_ref[...], axis=0, keepdims=True))

    def counter(cmp):
        def count(thr):
            def body(kt, c):
                hit = jnp.where(cmp(sc_ref[pl.ds(tile_off(kt), tk), :], thr), 1.0, 0.0)
                return c + jnp.sum(hit.reshape(tk // 8, 8, tq), axis=0)
            c = lax.fori_loop(0, n_kt, body, jnp.zeros((8, tq), F32))
            return jnp.sum(c, axis=0, keepdims=True)
        return count

    thr = _kth_largest(counter(lambda s, t: s >= t), (1, tq), topk, bounds)
    need = topk - counter(lambda s, t: s > t)(thr)
    lower = jnp.where(lax.broadcasted_iota(I32, (tk, tk), 0) > lax.broadcasted_iota(I32, (tk, tk), 1),
                      1.0, 0.0).astype(BF16)

    def select(kt, ties_before):
        off = tile_off(kt)
        s = sc_ref[pl.ds(off, tk), :]
        eq = jnp.where(s == thr, 1.0, 0.0)
        rank = _dot(lower, eq.astype(BF16)) + ties_before
        keep = jnp.where(s > thr, 0.0, jnp.where(s == thr, jnp.where(rank < need, 0.0, NEG), NEG))
        bias_ref[pl.ds(off, tk), :] = jnp.where(key + off <= qry, keep, NEG).astype(BF16)
        return ties_before + jnp.sum(eq, axis=0, keepdims=True)
    lax.fori_loop(0, n_kt, select, jnp.zeros((1, tq), F32))

    def fill(kt, _):
        bias_ref[pl.ds(tile_off(kt), tk), :] = jnp.full((tk, tq), NEG, BF16)
        return 0
    lax.fori_loop(n_kt, seq // tk, fill, 0)


def _prompt_indexer(qi_t, w_t, ki, *, tq, tk, topk):
    seq = ki.shape[0]
    return pl.pallas_call(
        functools.partial(_pidx_kernel, tq=tq, tk=tk, topk=topk, seq=seq),
        grid=(seq // tq,),
        in_specs=[pl.BlockSpec((A_W, tq), lambda i: (0, i)),
                  pl.BlockSpec((N_HEADS, tq), lambda i: (0, i)),
                  _resident(ki.shape)],
        out_specs=pl.BlockSpec((None, seq, tq), lambda i: (i, 0, 0)),
        out_shape=jax.ShapeDtypeStruct((seq // tq, seq, tq), BF16),
        scratch_shapes=[pltpu.VMEM((seq, tq), F32), pltpu.VMEM((tk, tq), F32)],
        compiler_params=_cp(("arbitrary",)),
        name="prompt_indexer",
    )(qi_t, w_t, ki)


def _t5_bucket_table():
    n = np.arange(REL_MAX_DIST, dtype=np.int64)
    max_exact = REL_BUCKETS // 2
    nf = np.maximum(n, 1).astype(np.float32)
    large = max_exact + (np.log(nf / np.float32(max_exact)) / np.float32(math.log(REL_MAX_DIST / max_exact))
                         * np.float32(REL_BUCKETS - max_exact)).astype(np.int32)
    large = np.minimum(large, REL_BUCKETS - 1)
    return np.where(n < max_exact, n, large).astype(np.int32)


def _bias_by_distance(rel_bias, dist):
    table = _t5_bucket_table()
    bucket = np.where(dist >= REL_MAX_DIST, REL_BUCKETS - 1, table[np.clip(dist, 0, REL_MAX_DIST - 1)])
    b = jnp.moveaxis(rel_bias[bucket], -1, 0)
    far = rel_bias[REL_BUCKETS - 1].reshape((N_HEADS,) + (1,) * dist.ndim)
    return (b - far).astype(F32)


def _pattn_kernel(qt_ref, k_ref, vt_ref, mask_ref, tab_ref, ot_ref, m_ref, l_ref, acc_ref, *slots, tile, chunk):
    slot_a, slot_b = slots[:4], slots[4:]
    qb = pl.program_id(0)
    first = pl.program_id(1) * (chunk // tile)
    n_tiles = chunk // tile

    @pl.when(pl.program_id(1) == 0)
    def _():
        m_ref[...] = jnp.full(m_ref.shape, NEG, F32)
        l_ref[...] = jnp.zeros(l_ref.shape, F32)
        acc_ref[...] = jnp.zeros(acc_ref.shape, F32)

    def scores(j, slot, near):
        s_ref, mx_ref, _, _ = slot
        off = pl.multiple_of(j * tile, tile)
        mb = mask_ref[pl.ds(off, tile), :].astype(F32)
        for h in range(N_HEADS):
            pr = h // 2
            s = _dot(k_ref[pl.ds(off, tile), pr * LANES:(pr + 1) * LANES], qt_ref[h * LANES:(h + 1) * LANES, :])
            s = s + mb
            if near is not None:
                s = s + tab_ref[near, h]
            s_ref[h] = s
            mx_ref[h] = jnp.max(s, axis=0, keepdims=True)

    def accumulate(j, slot):
        s_ref, mx_ref, p_ref, al_ref = slot
        off = pl.multiple_of(j * tile, tile)
        for h in range(N_HEADS):
            m_old = m_ref[h]
            m_new = jnp.maximum(m_old, mx_ref[h])
            alpha = jnp.exp(m_old - m_new)
            p = jnp.exp(s_ref[h] - m_new)
            l_ref[h] = alpha * l_ref[h] + jnp.sum(p, axis=0, keepdims=True)
            p_ref[h] = p.astype(BF16)
            al_ref[h] = alpha
            m_ref[h] = m_new
        for h in range(N_HEADS):
            pv = _dot(vt_ref[h * HEAD_DIM:(h + 1) * HEAD_DIM, pl.ds(off, tile)], p_ref[h])
            acc_ref[h] = al_ref[h] * acc_ref[h] + pv

    all_far = first + n_tiles <= qb - 1

    @pl.when(all_far)
    def _():
        scores(0, slot_a, None)

        def pair(i, _):
            scores(2 * i + 1, slot_b, None)
            accumulate(2 * i, slot_a)
            scores(jnp.minimum(2 * i + 2, n_tiles - 1), slot_a, None)
            accumulate(2 * i + 1, slot_b)
            return 0
        lax.fori_loop(0, n_tiles // 2, pair, 0)

    @pl.when(jnp.logical_not(all_far))
    def _():
        def far_body(j, _):
            scores(j, slot_a, None)
            accumulate(j, slot_a)
            return 0
        lax.fori_loop(0, jnp.clip(qb - 1 - first, 0, n_tiles), far_body, 0)
        for near, j in ((1, qb - 1 - first), (0, qb - first)):
            @pl.when(jnp.logical_and(j >= 0, j < n_tiles))
            def _():
                scores(j, slot_a, near)
                accumulate(j, slot_a)

    @pl.when(pl.program_id(1) == pl.num_programs(1) - 1)
    def _():
        for h in range(N_HEADS):
            ot_ref[h * HEAD_DIM:(h + 1) * HEAD_DIM, :] = (acc_ref[h] / l_ref[h]).astype(ot_ref.dtype)


def _prompt_attention(q_t, k_bf, v_t, mask, rel_bias, *, tile, chunk):
    seq = k_bf.shape[0]
    chunk = min(chunk, seq)
    key = np.arange(tile)[:, None]
    qry = np.arange(tile)[None, :]
    tab = _bias_by_distance(rel_bias, np.stack([qry - key, tile + qry - key]))
    tab = jnp.moveaxis(tab, 0, 1)
    last_chunk = lambda i: (i * tile + tile - 1) // chunk
    return pl.pallas_call(
        functools.partial(_pattn_kernel, tile=tile, chunk=chunk),
        grid=(seq // tile, seq // chunk),
        in_specs=[pl.BlockSpec((N_HEADS * LANES, tile), lambda i, c: (0, i)),
                  pl.BlockSpec((chunk, A_W), lambda i, c: (jnp.minimum(c, last_chunk(i)), 0)),
                  pl.BlockSpec((A_W, chunk), lambda i, c: (0, jnp.minimum(c, last_chunk(i)))),
                  pl.BlockSpec((None, chunk, tile), lambda i, c: (i, jnp.minimum(c, last_chunk(i)), 0)),
                  pl.BlockSpec(tab.shape, lambda i, c: (0, 0, 0, 0), pipeline_mode=pl.Buffered(1))],
        out_specs=pl.BlockSpec((A_W, tile), lambda i, c: (0, i)),
        out_shape=jax.ShapeDtypeStruct((A_W, seq), BF16),
        scratch_shapes=[pltpu.VMEM((N_HEADS, 1, tile), F32), pltpu.VMEM((N_HEADS, 1, tile), F32),
                        pltpu.VMEM((N_HEADS, HEAD_DIM, tile), F32)]
                       + [pltpu.VMEM((N_HEADS, tile, tile), F32), pltpu.VMEM((N_HEADS, 1, tile), F32),
                          pltpu.VMEM((N_HEADS, tile, tile), BF16), pltpu.VMEM((N_HEADS, 1, tile), F32)] * 2,
        compiler_params=_cp(("arbitrary", "arbitrary")),
        name="prompt_attention",
    )(q_t, k_bf, v_t, mask, tab)


def _cumsum_rows(x):
    c = x.shape[0]
    tri = jnp.where(lax.broadcasted_iota(I32, (c, c), 0) >= lax.broadcasted_iota(I32, (c, c), 1),
                    1.0, 0.0).astype(BF16)
    hi = x.astype(BF16)
    r1 = x - hi.astype(F32)
    mid = r1.astype(BF16)
    lo = (r1 - mid.astype(F32)).astype(BF16)
    return _dot(tri, hi) + (_dot(tri, mid) + _dot(tri, lo))


def _gla_kernel(q_ref, k_ref, v_ref, g_ref, gate_ref, s0_ref, on_ref, o_ref, sfin_ref,
                st_ref, b_ref, oi_ref, *, chunk):
    @pl.when(pl.program_id(1) == 0)
    def _():
        st_ref[...] = jnp.zeros(st_ref.shape, F32)
        for h in range(N_HEADS):
            st_ref[h * HEAD_DIM:(h + 1) * HEAD_DIM, h * HEAD_DIM:(h + 1) * HEAD_DIM] = s0_ref[h]

    w = q_ref.shape[-1]
    b = _cumsum_rows(g_ref[...])
    b_ref[...] = b
    k = k_ref[...]
    v = v_ref[...]
    seg = _seg_matrix(w, HEAD_DIM, BF16)
    st = st_ref[...]
    o_inter = _dot_nt((q_ref[...] * jnp.exp(b)).astype(BF16), st.astype(BF16))
    for grp in range(chunk // 8):
        n = 8 * (grp + 1)
        rows = lax.broadcasted_iota(I32, (n, w), 0)
        out_rows = []
        for t in range(8 * grp, n):
            dec = jnp.where(rows <= t, jnp.exp(jnp.minimum(b_ref[t:t + 1, :] - b_ref[0:n, :], 0.0)), 0.0)
            prod = (q_ref[t:t + 1, :] * dec * k_ref[0:n, :]).astype(BF16)
            out_rows.append(jnp.sum(_dot(prod, seg) * v_ref[0:n, :], axis=0, keepdims=True))
        oi_ref[8 * grp:n, :] = jnp.concatenate(out_rows, axis=0)

    o = o_inter + oi_ref[...]
    ms = _dot((o * o).astype(BF16), seg) * (1.0 / HEAD_DIM)
    o_ref[...] = (o * lax.rsqrt(ms + EPS) * on_ref[...] * _silu(gate_ref[...])).astype(o_ref.dtype)

    b_last = b[chunk - 1:chunk, :]
    kd = (k * jnp.exp(b_last - b)).astype(BF16)
    upd = _dot(v.T.astype(BF16), kd)
    blockdiag = (lax.broadcasted_iota(I32, (w, w), 0) // HEAD_DIM
                 == lax.broadcasted_iota(I32, (w, w), 1) // HEAD_DIM)
    st_new = st * jnp.exp(b_last) + jnp.where(blockdiag, upd, 0.0)
    st_ref[...] = st_new

    @pl.when(pl.program_id(1) == pl.num_programs(1) - 1)
    def _():
        for h in range(N_HEADS):
            sfin_ref[h] = st_new[h * HEAD_DIM:(h + 1) * HEAD_DIM, h * HEAD_DIM:(h + 1) * HEAD_DIM]


def _gla(gq, gk, gv, glog, gate, s0, onorm_tiled, *, batch, chunk):
    n, w = gq.shape
    nc = n // batch // chunk
    row = lambda b, c: (b * nc + c, 0)
    tile = pl.BlockSpec((chunk, w), row)
    state = pl.BlockSpec((None, N_HEADS, HEAD_DIM, HEAD_DIM), lambda b, c: (b, 0, 0, 0))
    o, s_fin = pl.pallas_call(
        functools.partial(_gla_kernel, chunk=chunk),
        grid=(batch, nc),
        in_specs=[tile] * 5 + [state, pl.BlockSpec((1, w), lambda b, c: (0, 0))],
        out_specs=[tile, state],
        out_shape=[jax.ShapeDtypeStruct((n, w), BF16),
                   jax.ShapeDtypeStruct((batch, N_HEADS, HEAD_DIM, HEAD_DIM), F32)],
        scratch_shapes=[pltpu.VMEM((w, w), F32), pltpu.VMEM((chunk, w), F32), pltpu.VMEM((chunk, w), F32)],
        compiler_params=_cp(("arbitrary", "arbitrary")),
        name="hgrn2_gla",
    )(gq, gk, gv, glog, gate, jnp.swapaxes(s0, 2, 3), onorm_tiled)
    return o, jnp.swapaxes(s_fin, 2, 3)


def _even_tail_kernel(x_ref, a_ref, b_ref, g1_ref, sc_ref, sh_ref, g2_ref, ng_ref,
                      wa_ref, wb_ref, w1_ref, w3_ref, w2_ref, o_ref, *, ff_split):
    mix = _dot(a_ref[...], wa_ref[...]) + _dot(b_ref[...], wb_ref[...])
    x1 = x_ref[...] + g1_ref[...] * mix
    h = _norm_mod(x1, ng_ref[...], sc_ref[...], sh_ref[...]).astype(BF16)
    ff = jnp.zeros(x1.shape, F32)
    step = w1_ref.shape[1] // ff_split
    for j in range(ff_split):
        a = _dot(h, w1_ref[:, j * step:(j + 1) * step])
        g = _dot(h, w3_ref[:, j * step:(j + 1) * step])
        ff = ff + _dot((_silu(a) * g).astype(BF16), w2_ref[j * step:(j + 1) * step, :])
    o_ref[...] = x1 + g2_ref[...] * ff


def _mod_spec(m, tm, d):
    return (pl.BlockSpec((1, d), lambda i: (0, 0)) if m.shape[0] == 1
            else pl.BlockSpec((tm, d), lambda i: (i, 0)))


def _even_tail(x, a_out, b_out, g1, sc2, sh2, g2, ng2, wa, wb, w1, w3, w2, *, tm):
    n, d = x.shape
    row = lambda i: (i, 0)
    return pl.pallas_call(
        functools.partial(_even_tail_kernel, ff_split=2),
        grid=(n // tm,),
        in_specs=[pl.BlockSpec((tm, d), row), pl.BlockSpec((tm, A_W), row), pl.BlockSpec((tm, A_W), row),
                  _mod_spec(g1, tm, d), _mod_spec(sc2, tm, d), _mod_spec(sh2, tm, d), _mod_spec(g2, tm, d),
                  _resident((1, d)), _resident(wa.shape), _resident(wb.shape),
                  _resident(w1.shape), _resident(w3.shape), _resident(w2.shape)],
        out_specs=pl.BlockSpec((tm, d), row),
        out_shape=jax.ShapeDtypeStruct((n, d), F32),
        compiler_params=_cp(("arbitrary",)),
        name="even_tail",
    )(x, a_out, b_out, g1, sc2, sh2, g2, ng2, wa, wb, w1, w3, w2)


S5_LG = S5_W // LANES
S5_LG_PER_BLK = S5_LG // S5_NBLK


def _s5_prep_kernel(lr_ref, li_ref, ldt_ref, br_ref, bi_ref, ar_o, ai_o, bbr_o, bbi_o):
    lr = jnp.minimum(lr_ref[...], -1e-4)
    li = li_ref[...]
    dt = jnp.exp(ldt_ref[...])
    mag = jnp.exp(lr * dt)
    a_re = mag * jnp.cos(li * dt)
    a_im = mag * jnp.sin(li * dt)
    den = lr * lr + li * li
    nr = a_re - 1.0
    coef_re = (nr * lr + a_im * li) / den
    coef_im = (a_im * lr - nr * li) / den
    ar_o[...] = a_re
    ai_o[...] = a_im
    br = br_ref[...]
    bi = bi_ref[...]
    bbr_o[...] = coef_re * br - coef_im * bi
    bbi_o[...] = coef_re * bi + coef_im * br


def _s5_prep(lam_re, lam_im, log_dt, b_re, b_im):
    g, p = lam_re.shape
    v3 = lambda a: a.reshape(g, 1, p)
    bt = lambda b: jnp.transpose(b, (0, 2, 1))
    sds = jax.ShapeDtypeStruct
    return pl.pallas_call(
        _s5_prep_kernel,
        out_shape=[sds((g, 1, p), F32), sds((g, 1, p), F32), sds((g, S5_CH, p), F32), sds((g, S5_CH, p), F32)],
        name="s5_prep",
    )(v3(lam_re), v3(lam_im), log_dt.reshape(g, 1, 1), bt(b_re), bt(b_im))


def _s5_blockdiag_in(bb_t):
    x = bb_t.reshape(S5_NBLK, S5_BLK_GROUPS, S5_CH, S5_STATE)
    bd = jnp.einsum('bgcp,gh->bgchp', x, jnp.eye(S5_BLK_GROUPS, dtype=x.dtype))
    return bd.reshape(S5_NBLK, S5_BLK_GROUPS * S5_CH, S5_BLK_GROUPS * S5_STATE)


def _s5_blockdiag_out(c):
    x = c.reshape(S5_NBLK, S5_BLK_GROUPS, S5_CH, S5_STATE)
    bd = jnp.einsum('bgcp,gh->bgphc', x, jnp.eye(S5_BLK_GROUPS, dtype=x.dtype))
    return bd.reshape(S5_NBLK, S5_BLK_GROUPS * S5_STATE, S5_BLK_GROUPS * S5_CH)


def _s5_project_in(u, bbr_hi, bbr_lo, bbi_hi, bbi_lo, xre_s, xim_s):
    for blk in range(S5_NBLK):
        uh, ul = _split2(u[:, blk * LANES:(blk + 1) * LANES])
        for hi, lo, dst in ((bbr_hi, bbr_lo, xre_s), (bbi_hi, bbi_lo, xim_s)):
            r = _dot(uh, hi[blk]) + (_dot(uh, lo[blk]) + _dot(ul, hi[blk]))
            for j in range(S5_LG_PER_BLK):
                dst[blk * S5_LG_PER_BLK + j] = r[:, j * LANES:(j + 1) * LANES]


def _s5_scan(xre_s, xim_s, st_re, st_im, ar_ref, ai_ref, *, n_steps, r, store):
    def blk_body(blk, _):
        base = blk * S5_LG_PER_BLK
        ar = [jnp.broadcast_to(ar_ref[base + j], (r, LANES)) for j in range(S5_LG_PER_BLK)]
        ai = [jnp.broadcast_to(ai_ref[base + j], (r, LANES)) for j in range(S5_LG_PER_BLK)]

        def step(s, carry):
            r0 = pl.multiple_of(s * r, r)
            new = []
            for j in range(S5_LG_PER_BLK):
                sr, si = carry[2 * j], carry[2 * j + 1]
                nr = ar[j] * sr - ai[j] * si + xre_s[base + j, pl.ds(r0, r), :]
                ni = ar[j] * si + ai[j] * sr + xim_s[base + j, pl.ds(r0, r), :]
                if store:
                    xre_s[base + j, pl.ds(r0, r), :] = nr
                    xim_s[base + j, pl.ds(r0, r), :] = ni
                new += [nr, ni]
            return tuple(new)

        init = []
        for j in range(S5_LG_PER_BLK):
            init += [st_re[base + j], st_im[base + j]]
        fin = lax.fori_loop(0, n_steps, step, tuple(init))
        for j in range(S5_LG_PER_BLK):
            st_re[base + j] = fin[2 * j]
            st_im[base + j] = fin[2 * j + 1]
        return 0
    lax.fori_loop(0, S5_NBLK, blk_body, 0)


def _s5_local_kernel(x_ref, sc_ref, sh_ref, ng_ref, bbr_hi, bbr_lo, bbi_hi, bbi_lo, ar_ref, ai_ref,
                     lre_o, lim_o, xre_s, xim_s, st_re, st_im, *, n_steps, r):
    @pl.when(pl.program_id(0) == 0)
    def _():
        st_re[...] = jnp.zeros(st_re.shape, F32)
        st_im[...] = jnp.zeros(st_im.shape, F32)
    u = _norm_mod(x_ref[...], ng_ref[...], sc_ref[...], sh_ref[...])
    _s5_project_in(u, bbr_hi, bbr_lo, bbi_hi, bbi_lo, xre_s, xim_s)
    _s5_scan(xre_s, xim_s, st_re, st_im, ar_ref, ai_ref, n_steps=n_steps, r=r, store=False)
    lre_o[...] = st_re[...]
    lim_o[...] = st_im[...]


def _cpow(ar, ai, n):
    rr, ri = None, None
    br, bi = ar, ai
    while n:
        if n & 1:
            if rr is None:
                rr, ri = br, bi
            else:
                rr, ri = rr * br - ri * bi, rr * bi + ri * br
        n >>= 1
        if n:
            br, bi = br * br - bi * bi, 2.0 * br * bi
    return rr, ri


def _s5_full_kernel(x_ref, sc_ref, sh_ref, ng_ref, g1_ref, bbr_hi, bbr_lo, bbi_hi, bbi_lo, ar_ref, ai_ref,
                    cre_ref, cim_ref, d_ref, h0re_ref, h0im_ref, lre_ref, lim_ref, gluw_ref,
                    o_ref, fre_o, fim_o, xre_s, xim_s, st_re, st_im, *, n_steps, r, seg_len):
    @pl.when(pl.program_id(0) == 0)
    def _():
        if seg_len is None:
            st_re[...] = h0re_ref[...]
            st_im[...] = h0im_ref[...]
        else:
            pr, pi = _cpow(ar_ref[...], ai_ref[...], seg_len)
            sr = h0re_ref[:, 0:1, :]
            si = h0im_ref[:, 0:1, :]
            st_re[:, 0:1, :] = sr
            st_im[:, 0:1, :] = si
            for row in range(1, r):
                sr, si = (lre_ref[:, row - 1:row, :] + (pr * sr - pi * si),
                          lim_ref[:, row - 1:row, :] + (pr * si + pi * sr))
                st_re[:, row:row + 1, :] = sr
                st_im[:, row:row + 1, :] = si

    x = x_ref[...]
    u = _norm_mod(x, ng_ref[...], sc_ref[...], sh_ref[...])
    _s5_project_in(u, bbr_hi, bbr_lo, bbi_hi, bbi_lo, xre_s, xim_s)
    _s5_scan(xre_s, xim_s, st_re, st_im, ar_ref, ai_ref, n_steps=n_steps, r=r, store=True)
    fre_o[...] = st_re[...]
    fim_o[...] = st_im[...]
    ys = []
    for blk in range(S5_NBLK):
        acc = None
        for j in range(S5_LG_PER_BLK):
            lg = blk * S5_LG_PER_BLK + j
            t = (_dot(xre_s[lg].astype(BF16), cre_ref[blk, j * LANES:(j + 1) * LANES, :])
                 - _dot(xim_s[lg].astype(BF16), cim_ref[blk, j * LANES:(j + 1) * LANES, :]))
            acc = t if acc is None else acc + t
        ys.append(acc)
    y = jnp.concatenate(ys, axis=-1) + d_ref[...] * u
    z = _dot(_gelu_tanh(y).astype(BF16), gluw_ref[...])
    dm = x.shape[-1]
    mix = z[:, :dm] * _sigmoid(z[:, dm:])
    o_ref[...] = x + g1_ref[...] * mix


def _s5_weight_specs(ws):
    return [_resident(w.shape) for w in ws]


def _s5_local(x, sc, sh, ng, bb, a3, *, r, rows):
    n, d = x.shape
    n_steps = rows // r
    sds = jax.ShapeDtypeStruct((S5_LG, r, LANES), F32)
    return pl.pallas_call(
        functools.partial(_s5_local_kernel, n_steps=n_steps, r=r),
        grid=(n // rows,),
        in_specs=[pl.BlockSpec((rows, d), lambda i: (i, 0)), _mod_spec(sc, rows, d), _mod_spec(sh, rows, d),
                  _resident((1, d))] + _s5_weight_specs(bb + a3),
        out_specs=[_resident(sds.shape)] * 2,
        out_shape=[sds, sds],
        scratch_shapes=[pltpu.VMEM((S5_LG, rows, LANES), F32)] * 2 + [pltpu.VMEM((S5_LG, r, LANES), F32)] * 2,
        compiler_params=_cp(("arbitrary",)),
        name="s5_local_scan",
    )(x, sc, sh, ng, *bb, *a3)


def _s5_full(x, sc, sh, ng, g1, bb, a3, cre, cim, dskip, h0re, h0im, lre, lim, gluw, *, r, rows, seg_len):
    n, d = x.shape
    n_steps = rows // r
    st = jax.ShapeDtypeStruct((S5_LG, r, LANES), F32)
    return pl.pallas_call(
        functools.partial(_s5_full_kernel, n_steps=n_steps, r=r, seg_len=seg_len),
        grid=(n // rows,),
        in_specs=[pl.BlockSpec((rows, d), lambda i: (i, 0)), _mod_spec(sc, rows, d), _mod_spec(sh, rows, d),
                  _resident((1, d)), _mod_spec(g1, rows, d)]
                 + _s5_weight_specs(bb + a3 + [cre, cim, dskip, h0re, h0im, lre, lim, gluw]),
        out_specs=[pl.BlockSpec((rows, d), lambda i: (i, 0)), _resident(st.shape), _resident(st.shape)],
        out_shape=[jax.ShapeDtypeStruct((n, d), F32), st, st],
        scratch_shapes=[pltpu.VMEM((S5_LG, rows, LANES), F32)] * 2 + [pltpu.VMEM((S5_LG, r, LANES), F32)] * 2,
        compiler_params=_cp(("arbitrary",)),
        name="s5_scan_glu",
    )(x, sc, sh, ng, g1, *bb, *a3, cre, cim, dskip, h0re, h0im, lre, lim, gluw)


def _to_lane_groups(s):
    r = s.shape[0]
    return jnp.transpose(s.reshape(r, S5_LG, LANES), (1, 0, 2))


def _from_lane_groups(s):
    r = s.shape[1]
    return jnp.transpose(s, (1, 0, 2)).reshape(r, S5_W)


def _moe_kernel(x_ref, sc_ref, sh_ref, g2_ref, ng_ref, rw_hi, rw_lo, rb_ref, w1_ref, w3_ref, w2_ref,
                o_ref, h_s, gate_s, acc_s):
    e = pl.program_id(1)
    lane = lax.broadcasted_iota(I32, gate_s.shape, 1)

    @pl.when(e == 0)
    def _():
        h = _norm_mod(x_ref[...], ng_ref[...], sc_ref[...], sh_ref[...])
        h_s[...] = h.astype(BF16)
        hh, hl = _split2(h)
        logits = _dot(hh, rw_hi[...]) + (_dot(hh, rw_lo[...]) + _dot(hl, rw_hi[...])) + rb_ref[...]
        logits = jnp.where(lane < N_EXPERTS, logits, -jnp.inf)
        m1 = jnp.max(logits, axis=-1, keepdims=True)
        i1 = jnp.min(jnp.where(logits == m1, lane, LANES), axis=-1, keepdims=True)
        rest = jnp.where(lane == i1, -jnp.inf, logits)
        m2 = jnp.max(rest, axis=-1, keepdims=True)
        i2 = jnp.min(jnp.where(rest == m2, lane, LANES), axis=-1, keepdims=True)
        e2 = jnp.exp(m2 - m1)
        den = 1.0 + e2
        gate_s[...] = jnp.where(lane == i1, 1.0 / den, 0.0) + jnp.where(lane == i2, e2 / den, 0.0)
        acc_s[...] = jnp.zeros(acc_s.shape, F32)

    h = h_s[...]
    ge = jnp.sum(jnp.where(lane == e, gate_s[...], 0.0), axis=-1, keepdims=True)
    y = _dot((_silu(_dot(h, w1_ref[...])) * _dot(h, w3_ref[...])).astype(BF16), w2_ref[...])
    acc_s[...] += ge * y

    @pl.when(e == N_EXPERTS - 1)
    def _():
        o_ref[...] = x_ref[...] + g2_ref[...] * acc_s[...]


def _moe(x, sc, sh, g2, ng, rw_hi, rw_lo, rb, w1, w3, w2, *, tm):
    n, d = x.shape
    ne, _, ff = w1.shape
    mod = lambda m: (pl.BlockSpec((1, d), lambda i, e: (0, 0)) if m.shape[0] == 1
                     else pl.BlockSpec((tm, d), lambda i, e: (i, 0)))
    const = lambda shape: pl.BlockSpec(shape, lambda i, e: (0,) * len(shape), pipeline_mode=pl.Buffered(1))
    return pl.pallas_call(
        _moe_kernel,
        grid=(n // tm, ne),
        in_specs=[pl.BlockSpec((tm, d), lambda i, e: (i, 0)), mod(sc), mod(sh), mod(g2), const((1, d)),
                  const(rw_hi.shape), const(rw_lo.shape), const(rb.shape),
                  pl.BlockSpec((None, d, ff), lambda i, e: (e, 0, 0)),
                  pl.BlockSpec((None, d, ff), lambda i, e: (e, 0, 0)),
                  pl.BlockSpec((None, ff, d), lambda i, e: (e, 0, 0))],
        out_specs=pl.BlockSpec((tm, d), lambda i, e: (i, 0)),
        out_shape=jax.ShapeDtypeStruct((n, d), F32),
        scratch_shapes=[pltpu.VMEM((tm, d), BF16), pltpu.VMEM((tm, LANES), F32), pltpu.VMEM((tm, d), F32)],
        compiler_params=_cp(("arbitrary", "arbitrary")),
        name="moe_dense",
    )(x, sc, sh, g2, ng, rw_hi, rw_lo, rb, w1, w3, w2)


def _sidx_kernel(pt_ref, qi_ref, w_ref, *rest, n_pages, group, topk, t_new):
    del pt_ref
    page_refs, (ikn_ref, bias_ref, sc_ref) = rest[:group], rest[group:]
    step = pl.program_id(1)
    rows = t_new

    def page_scores(keys):
        r = jnp.maximum(_dot_nt(qi_ref[...], keys.astype(BF16)), 0.0) * w_ref[...]
        acc = r[0:rows]
        for h in range(1, N_HEADS):
            acc = acc + r[h * rows:(h + 1) * rows]
        return acc

    for g in range(group):
        sc_ref[step * group + g] = page_scores(page_refs[g][...])

    @pl.when(step == n_pages // group - 1)
    def _():
        row = lax.broadcasted_iota(I32, (rows, PAGE), 0)
        col = lax.broadcasted_iota(I32, (rows, PAGE), 1)
        sc_ref[n_pages] = jnp.where(col <= row, page_scores(ikn_ref[...]), -jnp.inf)

        def counter(cmp):
            def count(thr):
                def body(j, c):
                    return c + jnp.where(cmp(sc_ref[j], thr), 1.0, 0.0)
                c = lax.fori_loop(0, n_pages + 1, body, jnp.zeros((rows, PAGE), F32))
                return jnp.sum(c, axis=-1, keepdims=True)
            return count

        thr = _kth_largest(counter(lambda s, t: s >= t), (rows, 1), topk)
        need = topk - counter(lambda s, t: s > t)(thr)
        upper = jnp.where(lax.broadcasted_iota(I32, (PAGE, PAGE), 0) < lax.broadcasted_iota(I32, (PAGE, PAGE), 1),
                          1.0, 0.0).astype(BF16)

        def select(j, ties_before):
            s = sc_ref[j]
            eq = jnp.where(s == thr, 1.0, 0.0)
            rank = _dot(eq.astype(BF16), upper) + ties_before
            keep = jnp.where(s > thr, 0.0, jnp.where(s == thr, jnp.where(rank < need, 0.0, NEG), NEG))
            bias_ref[j] = jnp.where(s == -jnp.inf, NEG, keep)
            return ties_before + jnp.sum(eq, axis=-1, keepdims=True)
        lax.fori_loop(0, n_pages + 1, select, jnp.zeros((rows, 1), F32))


def _page_spec(width, group, g):
    return pl.BlockSpec((None, PAGE, width), lambda b, s, pt: (pt[b, s * group + g], 0, 0))


def _sample_indexer(page_table, qi_rows, w_rows, cache_ik, ki_new, *, topk, t_new, group):
    bd, n_pages = page_table.shape
    hq = qi_rows.shape[1]
    group = min(group, n_pages)
    grid_spec = pltpu.PrefetchScalarGridSpec(
        num_scalar_prefetch=1,
        grid=(bd, n_pages // group),
        in_specs=[pl.BlockSpec((None, hq, HEAD_DIM), lambda b, s, pt: (b, 0, 0)),
                  pl.BlockSpec((None, hq, 1), lambda b, s, pt: (b, 0, 0))]
                 + [_page_spec(HEAD_DIM, group, g) for g in range(group)]
                 + [pl.BlockSpec((None, PAGE, HEAD_DIM), lambda b, s, pt: (b, 0, 0))],
        out_specs=pl.BlockSpec((None, n_pages + 1, t_new, PAGE), lambda b, s, pt: (b, 0, 0, 0)),
        scratch_shapes=[pltpu.VMEM((n_pages + 1, t_new, PAGE), F32)])
    return pl.pallas_call(
        functools.partial(_sidx_kernel, n_pages=n_pages, group=group, topk=topk, t_new=t_new),
        grid_spec=grid_spec,
        out_shape=jax.ShapeDtypeStruct((bd, n_pages + 1, t_new, PAGE), F32),
        compiler_params=_cp(("arbitrary", "arbitrary")),
        name="sample_indexer",
    )(page_table, qi_rows, w_rows, *([cache_ik] * group), ki_new)


def _sattn_kernel(pt_ref, q_ref, *rest, n_pages, group, t_new):
    del pt_ref
    k_refs, v_refs = rest[:group], rest[group:2 * group]
    kn_ref, vn_ref, mb_ref, tabl_ref, tabn_ref, o_ref, m_s, l_s, acc_s = rest[2 * group:]
    step = pl.program_id(1)
    n_steps = n_pages // group

    @pl.when(step == 0)
    def _():
        m_s[...] = jnp.full(m_s.shape, NEG, F32)
        l_s[...] = jnp.zeros(l_s.shape, F32)
        acc_s[...] = jnp.zeros(acc_s.shape, F32)

    def logits(keys, mask_row, table):
        s = _dot_nt(q_ref[...], keys) + jnp.concatenate([mb_ref[mask_row]] * N_HEADS, axis=0)
        return s if table is None else s + table

    def page_rows(ref):
        parts = [ref[pl.ds(j, PAGE, stride=A_W // LANES), :] for j in range(A_W // LANES)]
        return jnp.concatenate(parts, axis=1).astype(BF16)

    def update(last):
        ks = [page_rows(r) for r in k_refs] + ([kn_ref[...].astype(BF16)] if last else [])
        vs = [page_rows(r) for r in v_refs] + ([vn_ref[...].astype(BF16)] if last else [])
        ss = [logits(ks[g], step * group + g, tabl_ref[...] if (last and g == group - 1) else None)
              for g in range(group)]
        if last:
            ss.append(logits(ks[group], n_pages, tabn_ref[...]))
        m_old = m_s[...]
        m_new = m_old
        for s in ss:
            m_new = jnp.maximum(m_new, jnp.max(s, axis=-1, keepdims=True))
        alpha = jnp.exp(m_old - m_new)
        l_new = alpha * l_s[...]
        acc = alpha * acc_s[...]
        for s, v in zip(ss, vs):
            pexp = jnp.exp(s - m_new)
            l_new = l_new + jnp.sum(pexp, axis=-1, keepdims=True)
            acc = acc + _dot(pexp.astype(BF16), v)
        m_s[...] = m_new
        l_s[...] = l_new
        acc_s[...] = acc

    @pl.when(step < n_steps - 1)
    def _():
        update(False)

    @pl.when(step == n_steps - 1)
    def _():
        update(True)
        r = acc_s[...] / l_s[...]
        lane_head = lax.broadcasted_iota(I32, (t_new, A_W), 1) // HEAD_DIM
        out = jnp.zeros((t_new, A_W), F32)
        for h in range(N_HEADS):
            out = out + jnp.where(lane_head == h, r[h * t_new:(h + 1) * t_new], 0.0)
        o_ref[...] = out.astype(o_ref.dtype)


def _sample_attention(page_table, q_bd, cache_k, cache_v, k_new, v_new, mask, tab_last, tab_new, *, t_new, group):
    bd, n_pages = page_table.shape
    hq = q_bd.shape[1]
    group = min(group, n_pages)
    per_b = lambda b, s, pt: (b, 0, 0)
    const2 = lambda b, s, pt: (0, 0)
    page_rows = PAGE * A_W // LANES
    pages = [pl.BlockSpec((page_rows, LANES), functools.partial(lambda b, s, pt, g: (pt[b, s * group + g], 0), g=g))
             for g in range(group)]
    grid_spec = pltpu.PrefetchScalarGridSpec(
        num_scalar_prefetch=1,
        grid=(bd, n_pages // group),
        in_specs=[pl.BlockSpec((None, hq, A_W), per_b)] + pages + pages
                 + [pl.BlockSpec((None, PAGE, A_W), per_b), pl.BlockSpec((None, PAGE, A_W), per_b),
                    pl.BlockSpec((None, n_pages + 1, t_new, PAGE), lambda b, s, pt: (b, 0, 0, 0)),
                    pl.BlockSpec((hq, PAGE), const2), pl.BlockSpec((hq, PAGE), const2)],
        out_specs=pl.BlockSpec((None, t_new, A_W), per_b),
        scratch_shapes=[pltpu.VMEM((hq, 1), F32), pltpu.VMEM((hq, 1), F32), pltpu.VMEM((hq, A_W), F32)])
    return pl.pallas_call(
        functools.partial(_sattn_kernel, n_pages=n_pages, group=group, t_new=t_new),
        grid_spec=grid_spec,
        out_shape=jax.ShapeDtypeStruct((bd, t_new, A_W), BF16),
        compiler_params=_cp(("arbitrary", "arbitrary")),
        name="sample_attention",
    )(page_table, q_bd, *([cache_k] * group), *([cache_v] * group), k_new, v_new, mask, tab_last, tab_new)


ATTN_TILE = 256
ATTN_CHUNK = 2048
IDX_PAGE_GROUP = 16
ATTN_PAGE_GROUP = 8
S5_ROWS = 256
S5_SEGMENTS = 8


def _hi_lo(w):
    hi = w.astype(BF16)
    return hi, (w - hi.astype(F32)).astype(BF16)


def _even_layer_front(x, sc1, sh1, p, *, tm, qi_dtype):
    return _inproj(x, sc1, sh1, p['ng0'], p['w_in'], p['gq_pad'], p['gk_tiled'], p['gamma'],
                   tm=tm, qi_dtype=qi_dtype)


def _odd_layer(x, m, p, h0re, h0im, *, r, seg_len):
    sh1, sc1, g1, sh2, sc2, g2 = m
    zeros = jnp.zeros((S5_LG, r, LANES), F32)
    if seg_len is None:
        lre, lim = zeros, zeros
    else:
        lre, lim = _s5_local(x, sc1, sh1, p['ng1'], p['bb'], p['a3'], r=r, rows=S5_ROWS)
    x, fre, fim = _s5_full(x, sc1, sh1, p['ng1'], g1, p['bb'], p['a3'], p['cre'], p['cim'], p['dskip'],
                           h0re, h0im, lre, lim, p['gluw'], r=r, rows=S5_ROWS, seg_len=seg_len)
    x = _moe(x, sc2, sh2, g2, p['ng1b'], p['rw_hi'], p['rw_lo'], p['rb'], p['mw1'], p['mw3'], p['mw2'],
             tm=min(512, x.shape[0]))
    return x, _from_lane_groups(fre), _from_lane_groups(fim)


def kernel(x_prompt, x_sample, c_prompt, c_sample, cache_k, cache_v, cache_idx_k, state_hgrn, state_s5_re, state_s5_im, page_table, rel_bias, ada_w, ada_b, norm_g, w_in, qk_norm_g, hgrn_gamma, hgrn_onorm_g, w_out, ffn_w1, ffn_w3, ffn_w2, s5_lambda_re, s5_lambda_im, s5_log_dt, s5_b_re, s5_b_im, s5_c_re, s5_c_im, s5_d, s5_glu_w, moe_router_w, moe_router_b, moe_w1, moe_w3, moe_w2):
    bp, seq, d = x_prompt.shape
    bd, t_new, _ = x_sample.shape
    n_dec = bd * t_new
    n_phys = cache_k.shape[1]
    past = page_table.shape[1] * PAGE
    assert bp == 1 and d == D_MODEL and seq % 512 == 0
    assert n_dec == S5_ROWS and t_new == 8

    g_q, g_k = qk_norm_g[0, 0], qk_norm_g[0, 1]
    ar, ai, bbr_t, bbi_t = _s5_prep(s5_lambda_re[0], s5_lambda_im[0], s5_log_dt[0], s5_b_re[0], s5_b_im[0])
    bb = []
    for w in (_s5_blockdiag_in(bbr_t), _s5_blockdiag_in(bbi_t)):
        bb += list(_hi_lo(w))
    rw = jnp.pad(moe_router_w[0], ((0, 0), (0, LANES - N_EXPERTS)))
    rw_hi, rw_lo = _hi_lo(rw)
    p = dict(
        ng0=norm_g[0, 0][None], ng0b=norm_g[0, 1][None], ng1=norm_g[1, 0][None], ng1b=norm_g[1, 1][None],
        w_in=_pack_w_in(w_in[0]), gq_pad=jnp.tile(g_q, 2 * N_HEADS)[None], gk_tiled=jnp.tile(g_k, N_HEADS)[None],
        gamma=hgrn_gamma, onorm=jnp.tile(hgrn_onorm_g[0], N_HEADS)[None],
        wa=w_out[0, :A_W].astype(BF16), wb=w_out[0, A_W:].astype(BF16),
        w1=ffn_w1[0].astype(BF16), w3=ffn_w3[0].astype(BF16), w2=ffn_w2[0].astype(BF16),
        bb=bb, a3=[ar.reshape(S5_LG, 1, LANES), ai.reshape(S5_LG, 1, LANES)],
        cre=_s5_blockdiag_out(s5_c_re[0]).astype(BF16), cim=_s5_blockdiag_out(s5_c_im[0]).astype(BF16),
        dskip=s5_d[0][None], gluw=s5_glu_w[0].astype(BF16),
        rw_hi=rw_hi, rw_lo=rw_lo, rb=jnp.pad(moe_router_b[0], (0, LANES - N_EXPERTS))[None],
        mw1=moe_w1[0].astype(BF16), mw3=moe_w3[0].astype(BF16), mw2=moe_w2[0].astype(BF16),
    )

    c_rows = bp + bd
    c_all = jnp.concatenate([c_prompt, c_sample, jnp.zeros((-c_rows % 8, d), F32)], axis=0)
    mod = _ada(c_all, ada_w, ada_b)

    def mods(layer, rows, expand):
        m = mod[layer, rows]
        return tuple(expand(m[:, i * d:(i + 1) * d]) for i in range(6))

    xp = x_prompt.reshape(seq, d)
    sh1, sc1, g1, sh2, sc2, g2 = mods(0, slice(0, 1), lambda a: a)
    (q_pad, k_f, k_b, v_f, v_b, qi, kw, gq, gk, gl, gi, gg) = _even_layer_front(
        xp, sc1, sh1, p, tm=512, qi_dtype=BF16)
    ki_p = kw[:, :HEAD_DIM]
    mask = _prompt_indexer(qi.T, kw[:, HEAD_DIM:HEAD_DIM + N_HEADS].T, ki_p.astype(BF16),
                           tq=ATTN_TILE, tk=ATTN_TILE, topk=min(TOPK_MAX, seq // 4))
    a_out = _prompt_attention(q_pad.T, k_b, v_b.T, mask, rel_bias, tile=ATTN_TILE, chunk=ATTN_CHUNK).T
    b_out, st_p = _gla(gq, gk, gi, gl, gg, jnp.zeros((1, N_HEADS, HEAD_DIM, HEAD_DIM), F32), p['onorm'],
                       batch=1, chunk=64)
    xp = _even_tail(xp, a_out, b_out, g1, sc2, sh2, g2, p['ng0b'], p['wa'], p['wb'], p['w1'], p['w3'], p['w2'],
                    tm=512)
    seg_len = seq // S5_SEGMENTS
    to_seg = lambda a: a.reshape(S5_SEGMENTS, seg_len, d).transpose(1, 0, 2).reshape(seq, d)
    zero_state = jnp.zeros((S5_LG, S5_SEGMENTS, LANES), F32)
    xp, fre_p, fim_p = _odd_layer(to_seg(xp), mods(1, slice(0, 1), lambda a: a), p, zero_state, zero_state,
                                  r=S5_SEGMENTS, seg_len=seg_len)
    y_prompt = xp.reshape(seg_len, S5_SEGMENTS, d).transpose(1, 0, 2).reshape(bp, seq, d)

    xs = x_sample.reshape(n_dec, d)
    per_token = lambda a: jnp.repeat(a, t_new, axis=0)
    sh1, sc1, g1, sh2, sc2, g2 = mods(0, slice(bp, bp + bd), per_token)
    (q_pad_s, k_fs, _, v_fs, _, qi_s, kw_s, gq, gk, gl, gi, gg) = _even_layer_front(
        xs, sc1, sh1, p, tm=n_dec, qi_dtype=BF16)
    heads_first = lambda a: jnp.transpose(a, (0, 2, 1, 3)).reshape(bd, N_HEADS * t_new, a.shape[-1])
    qi_rows = heads_first(qi_s.reshape(bd, t_new, N_HEADS, HEAD_DIM))
    w_rows = heads_first(kw_s[:, HEAD_DIM:HEAD_DIM + N_HEADS].reshape(bd, t_new, N_HEADS, 1))
    pad_new = lambda a: jnp.pad(a.reshape(bd, t_new, -1), ((0, 0), (0, PAGE - t_new), (0, 0)))
    ki_s = kw_s[:, :HEAD_DIM]
    mask_s = _sample_indexer(page_table, qi_rows, w_rows, cache_idx_k[0], pad_new(ki_s),
                             topk=min(TOPK_MAX, (past + t_new) // 4), t_new=t_new, group=IDX_PAGE_GROUP)
    q4 = q_pad_s.reshape(bd, t_new, N_HEADS, LANES)
    even = (jnp.arange(N_HEADS) % 2 == 0)[None, None, :, None]
    q_nat = jnp.where(even, q4[..., :HEAD_DIM], q4[..., HEAD_DIM:])
    q_bd = jnp.einsum('bthd,hg->bhtgd', q_nat, jnp.eye(N_HEADS, dtype=q_nat.dtype))
    q_bd = q_bd.reshape(bd, N_HEADS * t_new, A_W)
    qpos = np.arange(t_new)[:, None]
    col = np.arange(PAGE)[None, :]
    rows_hq = lambda t: t.reshape(N_HEADS * t_new, PAGE)
    tab_last = rows_hq(_bias_by_distance(rel_bias, PAGE + qpos - col))
    tab_new = rows_hq(_bias_by_distance(rel_bias, qpos - col))
    flat = lambda c: c[0].reshape(n_phys * PAGE * A_W // LANES, LANES)
    a_out_s = _sample_attention(page_table, q_bd, flat(cache_k), flat(cache_v),
                                pad_new(k_fs), pad_new(v_fs), mask_s, tab_last, tab_new,
                                t_new=t_new, group=ATTN_PAGE_GROUP).reshape(n_dec, A_W)
    gla_chunk = 64
    pad_t = lambda a: jnp.pad(a.reshape(bd, t_new, A_W), ((0, 0), (0, gla_chunk - t_new), (0, 0))).reshape(-1, A_W)
    b_out_s, st_s = _gla(pad_t(gq), pad_t(gk), pad_t(gi), pad_t(gl), pad_t(gg),
                         state_hgrn[0], p['onorm'], batch=bd, chunk=gla_chunk)
    b_out_s = b_out_s.reshape(bd, gla_chunk, A_W)[:, :t_new].reshape(n_dec, A_W)
    xs = _even_tail(xs, a_out_s, b_out_s, g1, sc2, sh2, g2, p['ng0b'], p['wa'], p['wb'],
                    p['w1'], p['w3'], p['w2'], tm=n_dec)
    step_major = lambda a: a.reshape(bd, t_new, d).transpose(1, 0, 2).reshape(n_dec, d)
    xs, fre_s, fim_s = _odd_layer(step_major(xs), mods(1, slice(bp, bp + bd), lambda a: jnp.tile(a, (t_new, 1))), p,
                                  _to_lane_groups(state_s5_re[0].reshape(bd, S5_W)),
                                  _to_lane_groups(state_s5_im[0].reshape(bd, S5_W)), r=bd, seg_len=None)
    y_sample = xs.reshape(t_new, bd, d).transpose(1, 0, 2)

    heads = lambda a, b, t: a.reshape(1, b, t, N_HEADS, HEAD_DIM)
    s5_state = lambda f: f.reshape(1, -1, S5_GROUPS, S5_STATE)
    return (y_prompt, y_sample,
            heads(k_f, bp, seq), heads(v_f, bp, seq), ki_p.reshape(1, bp, seq, HEAD_DIM),
            st_p[None], s5_state(fre_p[S5_SEGMENTS - 1:]), s5_state(fim_p[S5_SEGMENTS - 1:]),
            heads(k_fs, bd, t_new), heads(v_fs, bd, t_new), ki_s.reshape(1, bd, t_new, HEAD_DIM),
            st_s[None], s5_state(fre_s), s5_state(fim_s))
```

```python
import functools
import math

import numpy as np
import jax
import jax.numpy as jnp
from jax import lax
from jax.experimental import pallas as pl
from jax.experimental.pallas import tpu as pltpu

F32 = jnp.float32
BF16 = jnp.bfloat16
I32 = jnp.int32

D_MODEL = 1024
N_HEADS = 8
HEAD_DIM = 64
A_W = N_HEADS * HEAD_DIM
LANES = 128
TOPK_MAX = 256
PAGE = 128
REL_BUCKETS = 32
REL_MAX_DIST = 128
S5_GROUPS = 64
S5_CH = 16
S5_STATE = 64
S5_W = S5_GROUPS * S5_STATE
S5_BLK_GROUPS = 8
S5_NBLK = S5_GROUPS // S5_BLK_GROUPS
D_FF = 2816
N_EXPERTS = 8
MOE_FF = 1408
EPS = 1e-6
NEG = -1e30
INT_MIN = -2 ** 31
VMEM_LIMIT = 56 * 2 ** 20


def _cp(sem, vmem=VMEM_LIMIT):
    return pltpu.CompilerParams(dimension_semantics=sem, vmem_limit_bytes=vmem)


def _resident(shape):
    n = len(shape)
    return pl.BlockSpec(shape, lambda *_: (0,) * n, pipeline_mode=pl.Buffered(1))


def _dot(a, b):
    return jnp.dot(a, b, preferred_element_type=F32)


def _dot_nt(a, b):
    return lax.dot_general(a, b, (((1,), (1,)), ((), ())), preferred_element_type=F32)


def _split2(x):
    hi = x.astype(BF16)
    lo = (x - hi.astype(F32)).astype(BF16)
    return hi, lo


def _dot_x3(a, b, nt=False):
    d = _dot_nt if nt else _dot
    ah, al = _split2(a)
    bh, bl = _split2(b)
    return d(ah, bh) + (d(ah, bl) + d(al, bh))


def _sigmoid(x):
    return 1.0 / (1.0 + jnp.exp(-x))


def _silu(x):
    return x * _sigmoid(x)


def _gelu_tanh(x):
    return 0.5 * x * (1.0 + jnp.tanh(math.sqrt(2.0 / math.pi) * (x + 0.044715 * (x * x * x))))


def _norm_mod(x, g, sc, sh):
    ms = jnp.mean(x * x, axis=-1, keepdims=True)
    return (x * lax.rsqrt(ms + EPS) * g) * (1.0 + sc) + sh


def _seg_matrix(n, seg, dtype, scale=1.0):
    r = lax.broadcasted_iota(I32, (n, n), 0) // seg
    c = lax.broadcasted_iota(I32, (n, n), 1) // seg
    return jnp.where(r == c, scale, 0.0).astype(dtype)


def _ada_kernel(c_ref, w_ref, b_ref, o_ref):
    c = c_ref[...]
    o_ref[...] = _dot_x3(_silu(c), w_ref[...]) + b_ref[...]


def _ada(c_all, ada_w, ada_b):
    depth, d, n6 = ada_w.shape
    rows = c_all.shape[0]
    tn = 1536
    return pl.pallas_call(
        _ada_kernel,
        grid=(depth, n6 // tn),
        in_specs=[pl.BlockSpec((rows, d), lambda l, j: (0, 0)),
                  pl.BlockSpec((None, d, tn), lambda l, j: (l, 0, j)),
                  pl.BlockSpec((None, 1, tn), lambda l, j: (l, 0, j))],
        out_specs=pl.BlockSpec((None, rows, tn), lambda l, j: (l, 0, j)),
        out_shape=jax.ShapeDtypeStruct((depth, rows, n6), F32),
        compiler_params=_cp(("arbitrary", "arbitrary")),
        name="ada_mod",
    )(c_all, ada_w, ada_b.reshape(depth, 1, n6))


W_OFF_Q, W_OFF_K, W_OFF_V, W_OFF_QI, W_OFF_KW, W_OFF_B = 0, 1024, 1536, 2048, 2560, 2688


def _pack_w_in(w):
    d = w.shape[0]
    z64 = jnp.zeros((d, N_HEADS, HEAD_DIM), w.dtype)
    q = w[:, 0:512].reshape(d, N_HEADS, HEAD_DIM)
    even = (jnp.arange(N_HEADS) % 2 == 0)[None, :, None]
    q_pad = jnp.concatenate([jnp.where(even, q, z64), jnp.where(even, z64, q)], axis=-1).reshape(d, 1024)
    kw = jnp.concatenate([w[:, 2048:2120], jnp.zeros((d, 56), w.dtype)], axis=-1)
    return jnp.concatenate([q_pad, w[:, 512:2048], kw, w[:, 2120:4168]], axis=-1).astype(BF16)


def _inproj_kernel(x_ref, sc_ref, sh_ref, ng_ref, w_ref, gq_ref, gk_ref, gam_ref,
                   q_ref, kf_ref, kb_ref, vf_ref, vb_ref, qi_ref, kw_ref,
                   gq_o, gk_o, gl_o, gi_o, gg_o):
    h = _norm_mod(x_ref[...], ng_ref[...], sc_ref[...], sh_ref[...]).astype(BF16)
    ones_seg = jnp.full((LANES, LANES), 1.0 / HEAD_DIM, BF16)
    pair_seg = _seg_matrix(LANES, HEAD_DIM, BF16, 1.0 / HEAD_DIM)

    def proj(off, n):
        return _dot(h, w_ref[:, off:off + n])

    pq = proj(W_OFF_Q, 1024)
    for j in range(8):
        blk = pq[:, j * LANES:(j + 1) * LANES]
        ms = _dot((blk * blk).astype(BF16), ones_seg)
        q_ref[:, j * LANES:(j + 1) * LANES] = (
            blk * lax.rsqrt(ms + EPS) * gq_ref[:, j * LANES:(j + 1) * LANES] * 0.125).astype(q_ref.dtype)
    pk = proj(W_OFF_K, 512)
    for j in range(4):
        blk = pk[:, j * LANES:(j + 1) * LANES]
        ms = _dot((blk * blk).astype(BF16), pair_seg)
        kn = blk * lax.rsqrt(ms + EPS) * gk_ref[:, j * LANES:(j + 1) * LANES]
        kf_ref[:, j * LANES:(j + 1) * LANES] = kn
        kb_ref[:, j * LANES:(j + 1) * LANES] = kn.astype(BF16)
    pv = proj(W_OFF_V, 512)
    vf_ref[...] = pv
    vb_ref[...] = pv.astype(BF16)
    qi_ref[...] = (proj(W_OFF_QI, 512) * 0.125).astype(qi_ref.dtype)
    lane = lax.broadcasted_iota(I32, (1, LANES), 1)
    kw_ref[...] = proj(W_OFF_KW, LANES) * jnp.where(lane < HEAD_DIM, 1.0, N_HEADS ** -0.5)
    gq_o[...] = proj(W_OFF_B, 512) * 0.125
    gam = gam_ref[...]
    gmax = jnp.max(gam, axis=0, keepdims=True)
    ge = jnp.exp(gam - gmax)
    lb = ge[0:1, :] / jnp.sum(ge, axis=0, keepdims=True)
    f = lb + (1.0 - lb) * _sigmoid(proj(W_OFF_B + 512, 512))
    gk_o[...] = 1.0 - f
    gl_o[...] = jnp.log(f)
    gi_o[...] = proj(W_OFF_B + 1024, 512)
    gg_o[...] = proj(W_OFF_B + 1536, 512)


def _inproj(x, sc, sh, ng, w_packed, gq_pad, gk_tiled, gamma, *, tm, qi_dtype):
    n, d = x.shape
    row = lambda i: (i, 0)
    mod_spec = (pl.BlockSpec((1, d), lambda i: (0, 0)) if sc.shape[0] == 1
                else pl.BlockSpec((tm, d), row))
    outs = [((n, 1024), BF16), ((n, 512), F32), ((n, 512), BF16), ((n, 512), F32), ((n, 512), BF16),
            ((n, 512), qi_dtype), ((n, LANES), F32)] + [((n, 512), F32)] * 5
    return pl.pallas_call(
        _inproj_kernel,
        grid=(n // tm,),
        in_specs=[pl.BlockSpec((tm, d), row), mod_spec, mod_spec,
                  _resident((1, d)), _resident(w_packed.shape), _resident((1, 1024)),
                  _resident((1, 512)), _resident(gamma.shape)],
        out_specs=[pl.BlockSpec((tm, s[1]), row) for s, _ in outs],
        out_shape=[jax.ShapeDtypeStruct(s, dt) for s, dt in outs],
        compiler_params=_cp(("arbitrary",)),
        name="in_proj",
    )(x, sc, sh, ng, w_packed, gq_pad, gk_tiled, gamma)


def _key_to_f32(key):
    neg = key < 0
    mag = jnp.where(neg, -key, key)
    return lax.bitcast_convert_type(jnp.where(neg, mag | jnp.int32(INT_MIN), mag), F32)


def _f32_to_key(x):
    bits = lax.bitcast_convert_type(x, I32)
    return jnp.where(bits < 0, -(bits & jnp.int32(0x7FFFFFFF)), bits)


def _kth_largest(count_ge, count_gt, shape, topk, bounds=None):
    if bounds is None:
        bounds = (jnp.full(shape, -jnp.inf, F32), jnp.full(shape, jnp.inf, F32))

    def cond(state):
        lo, hi, done, _ = state
        return jnp.max(jnp.where(jnp.logical_and(lo != hi, done == 0), 1, 0)) == 1

    def body(state):
        lo, hi, done, thr = state
        mid = (lo | hi) - ((lo ^ hi) >> 1)
        mid_f = _key_to_f32(mid)
        cnt = count_ge(mid_f)
        enough = cnt >= topk
        hit = jnp.logical_and(cnt == topk, done == 0)
        return (jnp.where(enough, mid, lo), jnp.where(enough, hi, mid - 1),
                jnp.where(hit, 1, done), jnp.where(hit, mid_f, thr))

    lo, hi = _f32_to_key(bounds[0]), _f32_to_key(bounds[1])
    zero = jnp.zeros(shape, F32)
    n_pos, n_nonneg = count_gt(zero), count_ge(zero)
    done = jnp.where(jnp.logical_and(n_pos < topk, n_nonneg >= topk), 1, 0)
    lo = jnp.where(n_pos >= topk, jnp.maximum(lo, 1), lo)
    hi = jnp.where(n_nonneg < topk, jnp.minimum(hi, -1), hi)
    lo, _, done, thr = lax.while_loop(cond, body, (lo, hi, done, jnp.zeros(shape, F32)))
    return jnp.where(done == 1, thr, _key_to_f32(lo))


def _pidx_kernel(qit_ref, wt_ref, ki_ref, bias_ref, sc_ref, cm_ref, *, tq, tk, topk, seq):
    assert topk <= tk
    q0 = pl.program_id(0) * tq
    n_kt = (q0 + tq + tk - 1) // tk
    key = lax.broadcasted_iota(I32, (tk, tq), 0)
    qry = lax.broadcasted_iota(I32, (tk, tq), 1) + q0

    def tile_off(kt):
        return pl.multiple_of(kt * tk, tk)

    def scores(kt, _):
        off = tile_off(kt)
        ks = ki_ref[pl.ds(off, tk), :]
        acc = jnp.zeros((tk, tq), F32)
        for h in range(N_HEADS):
            x = _dot(ks, qit_ref[h * HEAD_DIM:(h + 1) * HEAD_DIM, :])
            acc = acc + wt_ref[h:h + 1, :] * jnp.maximum(x, 0.0)
        acc = jnp.where(key + off <= qry, acc, -jnp.inf)
        sc_ref[pl.ds(off, tk), :] = acc
        cm_ref[...] = jnp.maximum(cm_ref[...], acc)
        return 0
    cm_ref[...] = jnp.full((tk, tq), -jnp.inf, F32)
    n_pairs = (n_kt + 1) // 2
    lax.fori_loop(0, 2 * n_pairs, scores, 0)
    bounds = (jnp.min(cm_ref[...], axis=0, keepdims=True), jnp.max(cm_ref[...], axis=0, keepdims=True))

    def counter(cmp):
        def count(thr):
            def body(kp, c):
                off = pl.multiple_of(kp * (2 * tk), 2 * tk)
                hit = jnp.where(cmp(sc_ref[pl.ds(off, 2 * tk), :], thr), 1.0, 0.0)
                return c + jnp.sum(hit.reshape(2 * tk // 64, 8, 8, tq), axis=0)
            c = lax.fori_loop(0, n_pairs, body, jnp.zeros((8, 8, tq), F32))
            return jnp.sum(jnp.sum(c, axis=0), axis=0, keepdims=True)
        return count

    count_ge, count_gt = counter(lambda s, t: s >= t), counter(lambda s, t: s > t)
    thr = _kth_largest(count_ge, count_gt, (1, tq), topk, bounds)
    need = topk - count_gt(thr)
    lower = jnp.where(lax.broadcasted_iota(I32, (tk, tk), 0) > lax.broadcasted_iota(I32, (tk, tk), 1),
                      1.0, 0.0).astype(BF16)

    def select(kt, ties_before):
        off = tile_off(kt)
        s = sc_ref[pl.ds(off, tk), :]
        eq = jnp.where(s == thr, 1.0, 0.0)
        rank = _dot(lower, eq.astype(BF16)) + ties_before
        keep = jnp.where(s > thr, 0.0, jnp.where(s == thr, jnp.where(rank < need, 0.0, NEG), NEG))
        bias_ref[pl.ds(off, tk), :] = jnp.where(key + off <= qry, keep, NEG).astype(BF16)
        return ties_before + jnp.sum(eq, axis=0, keepdims=True)
    lax.fori_loop(0, n_kt, select, jnp.zeros((1, tq), F32))

    def fill(kt, _):
        bias_ref[pl.ds(tile_off(kt), tk), :] = jnp.full((tk, tq), NEG, BF16)
        return 0
    lax.fori_loop(n_kt, seq // tk, fill, 0)


def _prompt_indexer(qi_t, w_t, ki, *, tq, tk, topk):
    seq = ki.shape[0]
    return pl.pallas_call(
        functools.partial(_pidx_kernel, tq=tq, tk=tk, topk=topk, seq=seq),
        grid=(seq // tq,),
        in_specs=[pl.BlockSpec((A_W, tq), lambda i: (0, i)),
                  pl.BlockSpec((N_HEADS, tq), lambda i: (0, i)),
                  _resident(ki.shape)],
        out_specs=pl.BlockSpec((None, seq, tq), lambda i: (i, 0, 0)),
        out_shape=jax.ShapeDtypeStruct((seq // tq, seq, tq), BF16),
        scratch_shapes=[pltpu.VMEM((seq, tq), F32), pltpu.VMEM((tk, tq), F32)],
        compiler_params=_cp(("arbitrary",)),
        name="prompt_indexer",
    )(qi_t, w_t, ki)


def _t5_bucket_table():
    n = np.arange(REL_MAX_DIST, dtype=np.int64)
    max_exact = REL_BUCKETS // 2
    nf = np.maximum(n, 1).astype(np.float32)
    large = max_exact + (np.log(nf / np.float32(max_exact)) / np.float32(math.log(REL_MAX_DIST / max_exact))
                         * np.float32(REL_BUCKETS - max_exact)).astype(np.int32)
    large = np.minimum(large, REL_BUCKETS - 1)
    return np.where(n < max_exact, n, large).astype(np.int32)


def _bias_by_distance(rel_bias, dist):
    table = _t5_bucket_table()
    bucket = np.where(dist >= REL_MAX_DIST, REL_BUCKETS - 1, table[np.clip(dist, 0, REL_MAX_DIST - 1)])
    b = jnp.moveaxis(rel_bias[bucket], -1, 0)
    far = rel_bias[REL_BUCKETS - 1].reshape((N_HEADS,) + (1,) * dist.ndim)
    return (b - far).astype(F32)


def _pattn_kernel(qt_ref, k_ref, vt_ref, mask_ref, tab_ref, ot_ref, m_ref, l_ref, acc_ref, *slots, tile, chunk):
    slot_a, slot_b = slots[:4], slots[4:]
    qb = pl.program_id(0)
    first = pl.program_id(1) * (chunk // tile)
    n_tiles = chunk // tile

    @pl.when(pl.program_id(1) == 0)
    def _():
        m_ref[...] = jnp.full(m_ref.shape, NEG, F32)
        l_ref[...] = jnp.zeros(l_ref.shape, F32)
        acc_ref[...] = jnp.zeros(acc_ref.shape, F32)

    def scores(j, slot, near):
        s_ref, mx_ref, _, _ = slot
        off = pl.multiple_of(j * tile, tile)
        mb = mask_ref[pl.ds(off, tile), :].astype(F32)
        for h in range(N_HEADS):
            pr = h // 2
            s = _dot(k_ref[pl.ds(off, tile), pr * LANES:(pr + 1) * LANES], qt_ref[h * LANES:(h + 1) * LANES, :])
            s = s + mb
            if near is not None:
                s = s + tab_ref[near, h]
            s_ref[h] = s
            mx_ref[h] = jnp.max(s, axis=0, keepdims=True)

    def accumulate(j, slot):
        s_ref, mx_ref, p_ref, al_ref = slot
        off = pl.multiple_of(j * tile, tile)
        for h in range(N_HEADS):
            m_old = m_ref[h]
            m_new = jnp.maximum(m_old, mx_ref[h])
            alpha = jnp.exp(m_old - m_new)
            p = jnp.exp(s_ref[h] - m_new)
            l_ref[h] = alpha * l_ref[h] + jnp.sum(p, axis=0, keepdims=True)
            p_ref[h] = p.astype(BF16)
            al_ref[h] = alpha
            m_ref[h] = m_new
        for h in range(N_HEADS):
            pv = _dot(vt_ref[h * HEAD_DIM:(h + 1) * HEAD_DIM, pl.ds(off, tile)], p_ref[h])
            acc_ref[h] = al_ref[h] * acc_ref[h] + pv

    all_far = first + n_tiles <= qb - 1

    @pl.when(all_far)
    def _():
        scores(0, slot_a, None)

        def pair(i, _):
            scores(2 * i + 1, slot_b, None)
            accumulate(2 * i, slot_a)
            scores(jnp.minimum(2 * i + 2, n_tiles - 1), slot_a, None)
            accumulate(2 * i + 1, slot_b)
            return 0
        lax.fori_loop(0, n_tiles // 2, pair, 0)

    @pl.when(jnp.logical_not(all_far))
    def _():
        def far_body(j, _):
            scores(j, slot_a, None)
            accumulate(j, slot_a)
            return 0
        lax.fori_loop(0, jnp.clip(qb - 1 - first, 0, n_tiles), far_body, 0)
        for near, j in ((1, qb - 1 - first), (0, qb - first)):
            @pl.when(jnp.logical_and(j >= 0, j < n_tiles))
            def _():
                scores(j, slot_a, near)
                accumulate(j, slot_a)

    @pl.when(pl.program_id(1) == pl.num_programs(1) - 1)
    def _():
        for h in range(N_HEADS):
            ot_ref[h * HEAD_DIM:(h + 1) * HEAD_DIM, :] = (acc_ref[h] / l_ref[h]).astype(ot_ref.dtype)


def _prompt_attention(q_t, k_bf, v_t, mask, rel_bias, *, tile, chunk):
    seq = k_bf.shape[0]
    chunk = min(chunk, seq)
    key = np.arange(tile)[:, None]
    qry = np.arange(tile)[None, :]
    tab = _bias_by_distance(rel_bias, np.stack([qry - key, tile + qry - key]))
    tab = jnp.moveaxis(tab, 0, 1)
    last_chunk = lambda i: (i * tile + tile - 1) // chunk
    return pl.pallas_call(
        functools.partial(_pattn_kernel, tile=tile, chunk=chunk),
        grid=(seq // tile, seq // chunk),
        in_specs=[pl.BlockSpec((N_HEADS * LANES, tile), lambda i, c: (0, i)),
                  pl.BlockSpec((chunk, A_W), lambda i, c: (jnp.minimum(c, last_chunk(i)), 0)),
                  pl.BlockSpec((A_W, chunk), lambda i, c: (0, jnp.minimum(c, last_chunk(i)))),
                  pl.BlockSpec((None, chunk, tile), lambda i, c: (i, jnp.minimum(c, last_chunk(i)), 0)),
                  pl.BlockSpec(tab.shape, lambda i, c: (0, 0, 0, 0), pipeline_mode=pl.Buffered(1))],
        out_specs=pl.BlockSpec((A_W, tile), lambda i, c: (0, i)),
        out_shape=jax.ShapeDtypeStruct((A_W, seq), BF16),
        scratch_shapes=[pltpu.VMEM((N_HEADS, 1, tile), F32), pltpu.VMEM((N_HEADS, 1, tile), F32),
                        pltpu.VMEM((N_HEADS, HEAD_DIM, tile), F32)]
                       + [pltpu.VMEM((N_HEADS, tile, tile), F32), pltpu.VMEM((N_HEADS, 1, tile), F32),
                          pltpu.VMEM((N_HEADS, tile, tile), BF16), pltpu.VMEM((N_HEADS, 1, tile), F32)] * 2,
        compiler_params=_cp(("arbitrary", "arbitrary")),
        name="prompt_attention",
    )(q_t, k_bf, v_t, mask, tab)


def _cumsum_rows(x):
    c = x.shape[0]
    tri = jnp.where(lax.broadcasted_iota(I32, (c, c), 0) >= lax.broadcasted_iota(I32, (c, c), 1),
                    1.0, 0.0).astype(BF16)
    hi = x.astype(BF16)
    r1 = x - hi.astype(F32)
    mid = r1.astype(BF16)
    lo = (r1 - mid.astype(F32)).astype(BF16)
    return _dot(tri, hi) + (_dot(tri, mid) + _dot(tri, lo))


def _gla_kernel(q_ref, k_ref, v_ref, g_ref, gate_ref, s0_ref, on_ref, o_ref, sfin_ref,
                st_ref, b_ref, oi_ref, *, chunk):
    @pl.when(pl.program_id(1) == 0)
    def _():
        st_ref[...] = jnp.zeros(st_ref.shape, F32)
        for h in range(N_HEADS):
            st_ref[h * HEAD_DIM:(h + 1) * HEAD_DIM, h * HEAD_DIM:(h + 1) * HEAD_DIM] = s0_ref[h]

    w = q_ref.shape[-1]
    b = _cumsum_rows(g_ref[...])
    b_ref[...] = b
    k = k_ref[...]
    v = v_ref[...]
    seg = _seg_matrix(w, HEAD_DIM, BF16)
    st = st_ref[...]
    o_inter = _dot_nt((q_ref[...] * jnp.exp(b)).astype(BF16), st.astype(BF16))
    for grp in range(chunk // 8):
        n = 8 * (grp + 1)
        rows = lax.broadcasted_iota(I32, (n, w), 0)
        out_rows = []
        for t in range(8 * grp, n):
            dec = jnp.where(rows <= t, jnp.exp(jnp.minimum(b_ref[t:t + 1, :] - b_ref[0:n, :], 0.0)), 0.0)
            prod = (q_ref[t:t + 1, :] * dec * k_ref[0:n, :]).astype(BF16)
            out_rows.append(jnp.sum(_dot(prod, seg) * v_ref[0:n, :], axis=0, keepdims=True))
        oi_ref[8 * grp:n, :] = jnp.concatenate(out_rows, axis=0)

    o = o_inter + oi_ref[...]
    ms = _dot((o * o).astype(BF16), seg) * (1.0 / HEAD_DIM)
    o_ref[...] = (o * lax.rsqrt(ms + EPS) * on_ref[...] * _silu(gate_ref[...])).astype(o_ref.dtype)

    b_last = b[chunk - 1:chunk, :]
    kd = (k * jnp.exp(b_last - b)).astype(BF16)
    upd = _dot(v.T.astype(BF16), kd)
    blockdiag = (lax.broadcasted_iota(I32, (w, w), 0) // HEAD_DIM
                 == lax.broadcasted_iota(I32, (w, w), 1) // HEAD_DIM)
    st_new = st * jnp.exp(b_last) + jnp.where(blockdiag, upd, 0.0)
    st_ref[...] = st_new

    @pl.when(pl.program_id(1) == pl.num_programs(1) - 1)
    def _():
        for h in range(N_HEADS):
            sfin_ref[h] = st_new[h * HEAD_DIM:(h + 1) * HEAD_DIM, h * HEAD_DIM:(h + 1) * HEAD_DIM]


def _gla(gq, gk, gv, glog, gate, s0, onorm_tiled, *, batch, chunk):
    n, w = gq.shape
    nc = n // batch // chunk
    row = lambda b, c: (b * nc + c, 0)
    tile = pl.BlockSpec((chunk, w), row)
    state = pl.BlockSpec((None, N_HEADS, HEAD_DIM, HEAD_DIM), lambda b, c: (b, 0, 0, 0))
    o, s_fin = pl.pallas_call(
        functools.partial(_gla_kernel, chunk=chunk),
        grid=(batch, nc),
        in_specs=[tile] * 5 + [state, pl.BlockSpec((1, w), lambda b, c: (0, 0))],
        out_specs=[tile, state],
        out_shape=[jax.ShapeDtypeStruct((n, w), BF16),
                   jax.ShapeDtypeStruct((batch, N_HEADS, HEAD_DIM, HEAD_DIM), F32)],
        scratch_shapes=[pltpu.VMEM((w, w), F32), pltpu.VMEM((chunk, w), F32), pltpu.VMEM((chunk, w), F32)],
        compiler_params=_cp(("arbitrary", "arbitrary")),
        name="hgrn2_gla",
    )(gq, gk, gv, glog, gate, jnp.swapaxes(s0, 2, 3), onorm_tiled)
    return o, jnp.swapaxes(s_fin, 2, 3)


def _even_tail_kernel(x_ref, a_ref, b_ref, g1_ref, sc_ref, sh_ref, g2_ref, ng_ref,
                      wa_ref, wb_ref, w1_ref, w3_ref, w2_ref, o_ref, *, ff_split):
    mix = _dot(a_ref[...], wa_ref[...]) + _dot(b_ref[...], wb_ref[...])
    x1 = x_ref[...] + g1_ref[...] * mix
    h = _norm_mod(x1, ng_ref[...], sc_ref[...], sh_ref[...]).astype(BF16)
    ff = jnp.zeros(x1.shape, F32)
    step = w1_ref.shape[1] // ff_split
    for j in range(ff_split):
        a = _dot(h, w1_ref[:, j * step:(j + 1) * step])
        g = _dot(h, w3_ref[:, j * step:(j + 1) * step])
        ff = ff + _dot((_silu(a) * g).astype(BF16), w2_ref[j * step:(j + 1) * step, :])
    o_ref[...] = x1 + g2_ref[...] * ff


def _mod_spec(m, tm, d):
    return (pl.BlockSpec((1, d), lambda i: (0, 0)) if m.shape[0] == 1
            else pl.BlockSpec((tm, d), lambda i: (i, 0)))


def _even_tail(x, a_out, b_out, g1, sc2, sh2, g2, ng2, wa, wb, w1, w3, w2, *, tm):
    n, d = x.shape
    row = lambda i: (i, 0)
    return pl.pallas_call(
        functools.partial(_even_tail_kernel, ff_split=2),
        grid=(n // tm,),
        in_specs=[pl.BlockSpec((tm, d), row), pl.BlockSpec((tm, A_W), row), pl.BlockSpec((tm, A_W), row),
                  _mod_spec(g1, tm, d), _mod_spec(sc2, tm, d), _mod_spec(sh2, tm, d), _mod_spec(g2, tm, d),
                  _resident((1, d)), _resident(wa.shape), _resident(wb.shape),
                  _resident(w1.shape), _resident(w3.shape), _resident(w2.shape)],
        out_specs=pl.BlockSpec((tm, d), row),
        out_shape=jax.ShapeDtypeStruct((n, d), F32),
        compiler_params=_cp(("arbitrary",)),
        name="even_tail",
    )(x, a_out, b_out, g1, sc2, sh2, g2, ng2, wa, wb, w1, w3, w2)


S5_LG = S5_W // LANES
S5_LG_PER_BLK = S5_LG // S5_NBLK


def _s5_prep_kernel(lr_ref, li_ref, ldt_ref, br_ref, bi_ref, ar_o, ai_o, bbr_o, bbi_o):
    lr = jnp.minimum(lr_ref[...], -1e-4)
    li = li_ref[...]
    dt = jnp.exp(ldt_ref[...])
    mag = jnp.exp(lr * dt)
    a_re = mag * jnp.cos(li * dt)
    a_im = mag * jnp.sin(li * dt)
    den = lr * lr + li * li
    nr = a_re - 1.0
    coef_re = (nr * lr + a_im * li) / den
    coef_im = (a_im * lr - nr * li) / den
    ar_o[...] = a_re
    ai_o[...] = a_im
    br = br_ref[...]
    bi = bi_ref[...]
    bbr_o[...] = coef_re * br - coef_im * bi
    bbi_o[...] = coef_re * bi + coef_im * br


def _s5_prep(lam_re, lam_im, log_dt, b_re, b_im):
    g, p = lam_re.shape
    v3 = lambda a: a.reshape(g, 1, p)
    bt = lambda b: jnp.transpose(b, (0, 2, 1))
    sds = jax.ShapeDtypeStruct
    return pl.pallas_call(
        _s5_prep_kernel,
        out_shape=[sds((g, 1, p), F32), sds((g, 1, p), F32), sds((g, S5_CH, p), F32), sds((g, S5_CH, p), F32)],
        name="s5_prep",
    )(v3(lam_re), v3(lam_im), log_dt.reshape(g, 1, 1), bt(b_re), bt(b_im))


def _s5_blockdiag_in(bb_t):
    x = bb_t.reshape(S5_NBLK, S5_BLK_GROUPS, S5_CH, S5_STATE)
    bd = jnp.einsum('bgcp,gh->bgchp', x, jnp.eye(S5_BLK_GROUPS, dtype=x.dtype))
    return bd.reshape(S5_NBLK, S5_BLK_GROUPS * S5_CH, S5_BLK_GROUPS * S5_STATE)


def _s5_blockdiag_out(c):
    x = c.reshape(S5_NBLK, S5_BLK_GROUPS, S5_CH, S5_STATE)
    bd = jnp.einsum('bgcp,gh->bgphc', x, jnp.eye(S5_BLK_GROUPS, dtype=x.dtype))
    return bd.reshape(S5_NBLK, S5_BLK_GROUPS * S5_STATE, S5_BLK_GROUPS * S5_CH)


def _s5_project_in(u, bbr_hi, bbr_lo, bbi_hi, bbi_lo, xre_s, xim_s):
    for blk in range(S5_NBLK):
        uh, ul = _split2(u[:, blk * LANES:(blk + 1) * LANES])
        for hi, lo, dst in ((bbr_hi, bbr_lo, xre_s), (bbi_hi, bbi_lo, xim_s)):
            r = _dot(uh, hi[blk]) + (_dot(uh, lo[blk]) + _dot(ul, hi[blk]))
            for j in range(S5_LG_PER_BLK):
                dst[blk * S5_LG_PER_BLK + j] = r[:, j * LANES:(j + 1) * LANES]


def _s5_scan(xre_s, xim_s, st_re, st_im, ar_ref, ai_ref, *, n_steps, r, store):
    def blk_body(blk, _):
        base = blk * S5_LG_PER_BLK
        ar = [jnp.broadcast_to(ar_ref[base + j], (r, LANES)) for j in range(S5_LG_PER_BLK)]
        ai = [jnp.broadcast_to(ai_ref[base + j], (r, LANES)) for j in range(S5_LG_PER_BLK)]

        def step(s, carry):
            r0 = pl.multiple_of(s * r, r)
            new = []
            for j in range(S5_LG_PER_BLK):
                sr, si = carry[2 * j], carry[2 * j + 1]
                nr = ar[j] * sr - ai[j] * si + xre_s[base + j, pl.ds(r0, r), :]
                ni = ar[j] * si + ai[j] * sr + xim_s[base + j, pl.ds(r0, r), :]
                if store:
                    xre_s[base + j, pl.ds(r0, r), :] = nr
                    xim_s[base + j, pl.ds(r0, r), :] = ni
                new += [nr, ni]
            return tuple(new)

        init = []
        for j in range(S5_LG_PER_BLK):
            init += [st_re[base + j], st_im[base + j]]
        fin = lax.fori_loop(0, n_steps, step, tuple(init))
        for j in range(S5_LG_PER_BLK):
            st_re[base + j] = fin[2 * j]
            st_im[base + j] = fin[2 * j + 1]
        return 0
    lax.fori_loop(0, S5_NBLK, blk_body, 0)


def _s5_local_kernel(x_ref, sc_ref, sh_ref, ng_ref, bbr_hi, bbr_lo, bbi_hi, bbi_lo, ar_ref, ai_ref,
                     lre_o, lim_o, xre_s, xim_s, st_re, st_im, *, n_steps, r):
    @pl.when(pl.program_id(0) == 0)
    def _():
        st_re[...] = jnp.zeros(st_re.shape, F32)
        st_im[...] = jnp.zeros(st_im.shape, F32)
    u = _norm_mod(x_ref[...], ng_ref[...], sc_ref[...], sh_ref[...])
    _s5_project_in(u, bbr_hi, bbr_lo, bbi_hi, bbi_lo, xre_s, xim_s)
    _s5_scan(xre_s, xim_s, st_re, st_im, ar_ref, ai_ref, n_steps=n_steps, r=r, store=False)
    lre_o[...] = st_re[...]
    lim_o[...] = st_im[...]


def _cpow(ar, ai, n):
    rr, ri = None, None
    br, bi = ar, ai
    while n:
        if n & 1:
            if rr is None:
                rr, ri = br, bi
            else:
                rr, ri = rr * br - ri * bi, rr * bi + ri * br
        n >>= 1
        if n:
            br, bi = br * br - bi * bi, 2.0 * br * bi
    return rr, ri


def _s5_full_kernel(x_ref, sc_ref, sh_ref, ng_ref, g1_ref, bbr_hi, bbr_lo, bbi_hi, bbi_lo, ar_ref, ai_ref,
                    cre_ref, cim_ref, d_ref, h0re_ref, h0im_ref, lre_ref, lim_ref, gluw_ref,
                    o_ref, fre_o, fim_o, xre_s, xim_s, st_re, st_im, *, n_steps, r, seg_len):
    @pl.when(pl.program_id(0) == 0)
    def _():
        if seg_len is None:
            st_re[...] = h0re_ref[...]
            st_im[...] = h0im_ref[...]
        else:
            pr, pi = _cpow(ar_ref[...], ai_ref[...], seg_len)
            sr = h0re_ref[:, 0:1, :]
            si = h0im_ref[:, 0:1, :]
            st_re[:, 0:1, :] = sr
            st_im[:, 0:1, :] = si
            for row in range(1, r):
                sr, si = (lre_ref[:, row - 1:row, :] + (pr * sr - pi * si),
                          lim_ref[:, row - 1:row, :] + (pr * si + pi * sr))
                st_re[:, row:row + 1, :] = sr
                st_im[:, row:row + 1, :] = si

    x = x_ref[...]
    u = _norm_mod(x, ng_ref[...], sc_ref[...], sh_ref[...])
    _s5_project_in(u, bbr_hi, bbr_lo, bbi_hi, bbi_lo, xre_s, xim_s)
    _s5_scan(xre_s, xim_s, st_re, st_im, ar_ref, ai_ref, n_steps=n_steps, r=r, store=True)
    fre_o[...] = st_re[...]
    fim_o[...] = st_im[...]
    ys = []
    for blk in range(S5_NBLK):
        acc = None
        for j in range(S5_LG_PER_BLK):
            lg = blk * S5_LG_PER_BLK + j
            t = (_dot(xre_s[lg].astype(BF16), cre_ref[blk, j * LANES:(j + 1) * LANES, :])
                 - _dot(xim_s[lg].astype(BF16), cim_ref[blk, j * LANES:(j + 1) * LANES, :]))
            acc = t if acc is None else acc + t
        ys.append(acc)
    y = jnp.concatenate(ys, axis=-1) + d_ref[...] * u
    z = _dot(_gelu_tanh(y).astype(BF16), gluw_ref[...])
    dm = x.shape[-1]
    mix = z[:, :dm] * _sigmoid(z[:, dm:])
    o_ref[...] = x + g1_ref[...] * mix


def _s5_weight_specs(ws):
    return [_resident(w.shape) for w in ws]


def _s5_local(x, sc, sh, ng, bb, a3, *, r, rows):
    n, d = x.shape
    n_steps = rows // r
    sds = jax.ShapeDtypeStruct((S5_LG, r, LANES), F32)
    return pl.pallas_call(
        functools.partial(_s5_local_kernel, n_steps=n_steps, r=r),
        grid=(n // rows,),
        in_specs=[pl.BlockSpec((rows, d), lambda i: (i, 0)), _mod_spec(sc, rows, d), _mod_spec(sh, rows, d),
                  _resident((1, d))] + _s5_weight_specs(bb + a3),
        out_specs=[_resident(sds.shape)] * 2,
        out_shape=[sds, sds],
        scratch_shapes=[pltpu.VMEM((S5_LG, rows, LANES), F32)] * 2 + [pltpu.VMEM((S5_LG, r, LANES), F32)] * 2,
        compiler_params=_cp(("arbitrary",)),
        name="s5_local_scan",
    )(x, sc, sh, ng, *bb, *a3)


def _s5_full(x, sc, sh, ng, g1, bb, a3, cre, cim, dskip, h0re, h0im, lre, lim, gluw, *, r, rows, seg_len):
    n, d = x.shape
    n_steps = rows // r
    st = jax.ShapeDtypeStruct((S5_LG, r, LANES), F32)
    return pl.pallas_call(
        functools.partial(_s5_full_kernel, n_steps=n_steps, r=r, seg_len=seg_len),
        grid=(n // rows,),
        in_specs=[pl.BlockSpec((rows, d), lambda i: (i, 0)), _mod_spec(sc, rows, d), _mod_spec(sh, rows, d),
                  _resident((1, d)), _mod_spec(g1, rows, d)]
                 + _s5_weight_specs(bb + a3 + [cre, cim, dskip, h0re, h0im, lre, lim, gluw]),
        out_specs=[pl.BlockSpec((rows, d), lambda i: (i, 0)), _resident(st.shape), _resident(st.shape)],
        out_shape=[jax.ShapeDtypeStruct((n, d), F32), st, st],
        scratch_shapes=[pltpu.VMEM((S5_LG, rows, LANES), F32)] * 2 + [pltpu.VMEM((S5_LG, r, LANES), F32)] * 2,
        compiler_params=_cp(("arbitrary",)),
        name="s5_scan_glu",
    )(x, sc, sh, ng, g1, *bb, *a3, cre, cim, dskip, h0re, h0im, lre, lim, gluw)


def _to_lane_groups(s):
    r = s.shape[0]
    return jnp.transpose(s.reshape(r, S5_LG, LANES), (1, 0, 2))


def _from_lane_groups(s):
    r = s.shape[1]
    return jnp.transpose(s, (1, 0, 2)).reshape(r, S5_W)


def _moe_kernel(x_ref, sc_ref, sh_ref, g2_ref, ng_ref, rw_hi, rw_lo, rb_ref, w1_ref, w3_ref, w2_ref,
                o_ref, h_s, gate_s, acc_s):
    e = pl.program_id(1)
    lane = lax.broadcasted_iota(I32, gate_s.shape, 1)

    @pl.when(e == 0)
    def _():
        h = _norm_mod(x_ref[...], ng_ref[...], sc_ref[...], sh_ref[...])
        h_s[...] = h.astype(BF16)
        hh, hl = _split2(h)
        logits = _dot(hh, rw_hi[...]) + (_dot(hh, rw_lo[...]) + _dot(hl, rw_hi[...])) + rb_ref[...]
        logits = jnp.where(lane < N_EXPERTS, logits, -jnp.inf)
        m1 = jnp.max(logits, axis=-1, keepdims=True)
        i1 = jnp.min(jnp.where(logits == m1, lane, LANES), axis=-1, keepdims=True)
        rest = jnp.where(lane == i1, -jnp.inf, logits)
        m2 = jnp.max(rest, axis=-1, keepdims=True)
        i2 = jnp.min(jnp.where(rest == m2, lane, LANES), axis=-1, keepdims=True)
        e2 = jnp.exp(m2 - m1)
        den = 1.0 + e2
        gate_s[...] = jnp.where(lane == i1, 1.0 / den, 0.0) + jnp.where(lane == i2, e2 / den, 0.0)
        acc_s[...] = jnp.zeros(acc_s.shape, F32)

    h = h_s[...]
    ge = jnp.sum(jnp.where(lane == e, gate_s[...], 0.0), axis=-1, keepdims=True)
    y = _dot((_silu(_dot(h, w1_ref[...])) * _dot(h, w3_ref[...])).astype(BF16), w2_ref[...])
    acc_s[...] += ge * y

    @pl.when(e == N_EXPERTS - 1)
    def _():
        o_ref[...] = x_ref[...] + g2_ref[...] * acc_s[...]


def _moe(x, sc, sh, g2, ng, rw_hi, rw_lo, rb, w1, w3, w2, *, tm):
    n, d = x.shape
    ne, _, ff = w1.shape
    mod = lambda m: (pl.BlockSpec((1, d), lambda i, e: (0, 0)) if m.shape[0] == 1
                     else pl.BlockSpec((tm, d), lambda i, e: (i, 0)))
    const = lambda shape: pl.BlockSpec(shape, lambda i, e: (0,) * len(shape), pipeline_mode=pl.Buffered(1))
    return pl.pallas_call(
        _moe_kernel,
        grid=(n // tm, ne),
        in_specs=[pl.BlockSpec((tm, d), lambda i, e: (i, 0)), mod(sc), mod(sh), mod(g2), const((1, d)),
                  const(rw_hi.shape), const(rw_lo.shape), const(rb.shape),
                  pl.BlockSpec((None, d, ff), lambda i, e: (e, 0, 0)),
                  pl.BlockSpec((None, d, ff), lambda i, e: (e, 0, 0)),
                  pl.BlockSpec((None, ff, d), lambda i, e: (e, 0, 0))],
        out_specs=pl.BlockSpec((tm, d), lambda i, e: (i, 0)),
        out_shape=jax.ShapeDtypeStruct((n, d), F32),
        scratch_shapes=[pltpu.VMEM((tm, d), BF16), pltpu.VMEM((tm, LANES), F32), pltpu.VMEM((tm, d), F32)],
        compiler_params=_cp(("arbitrary", "arbitrary")),
        name="moe_dense",
    )(x, sc, sh, g2, ng, rw_hi, rw_lo, rb, w1, w3, w2)


def _sidx_kernel(pt_ref, qi_ref, w_ref, *rest, n_pages, group, topk, t_new):
    del pt_ref
    page_refs, (ikn_ref, bias_ref, sc_ref, tot_ref) = rest[:group], rest[group:]
    step = pl.program_id(1)
    rows = t_new

    def page_scores(keys_t):
        r = jnp.maximum(_dot(qi_ref[...], keys_t.astype(BF16)), 0.0) * w_ref[...]
        acc = r[0:rows]
        for h in range(1, N_HEADS):
            acc = acc + r[h * rows:(h + 1) * rows]
        return acc

    for g in range(group):
        sc_ref[step * group + g] = page_scores(page_refs[g][...])

    @pl.when(step == n_pages // group - 1)
    def _():
        row = lax.broadcasted_iota(I32, (rows, PAGE), 0)
        col = lax.broadcasted_iota(I32, (rows, PAGE), 1)
        sc_ref[n_pages] = jnp.where(col <= row, page_scores(ikn_ref[...]), -jnp.inf)

        def counter(cmp):
            def count(thr):
                hit = jnp.where(cmp(sc_ref[...], jnp.broadcast_to(thr, (rows, PAGE))[None]), 1.0, 0.0)
                return jnp.sum(jnp.sum(hit, axis=0), axis=-1, keepdims=True)
            return count

        count_ge, count_gt = counter(lambda s, t: s >= t), counter(lambda s, t: s > t)
        thr = _kth_largest(count_ge, count_gt, (rows, 1), topk)
        need = topk - count_gt(thr)
        n_slots = n_pages + 1
        upper = jnp.where(lax.broadcasted_iota(I32, (PAGE, PAGE), 0) < lax.broadcasted_iota(I32, (PAGE, PAGE), 1),
                          1.0, 0.0).astype(BF16)
        thr_b = jnp.broadcast_to(thr, (rows, PAGE))[None]
        s_all = sc_ref[...]
        eq = jnp.where(s_all == thr_b, 1.0, 0.0).reshape(n_slots * rows, PAGE).astype(BF16)
        within = _dot(eq, upper).reshape(n_slots, rows, PAGE)
        tot_ref[...] = _dot(eq, jnp.ones((PAGE, PAGE), BF16)).reshape(n_slots, rows, PAGE)

        def running(j, before):
            total = tot_ref[j]
            tot_ref[j] = before
            return before + total
        lax.fori_loop(0, n_slots, running, jnp.zeros((rows, PAGE), F32))
        rank = within + tot_ref[...]
        keep = jnp.where(s_all > thr_b, 0.0,
                         jnp.where(s_all == thr_b,
                                   jnp.where(rank < jnp.broadcast_to(need, (rows, PAGE))[None], 0.0, NEG), NEG))
        bias_ref[...] = jnp.where(s_all == -jnp.inf, NEG, keep)


def _page_spec(width, group, g):
    return pl.BlockSpec((None, width, PAGE), lambda b, s, pt: (pt[b, s * group + g], 0, 0))


def _sample_indexer(page_table, qi_rows, w_rows, cache_ik, ki_new, *, topk, t_new, group):
    bd, n_pages = page_table.shape
    hq = qi_rows.shape[1]
    group = min(group, n_pages)
    grid_spec = pltpu.PrefetchScalarGridSpec(
        num_scalar_prefetch=1,
        grid=(bd, n_pages // group),
        in_specs=[pl.BlockSpec((None, hq, HEAD_DIM), lambda b, s, pt: (b, 0, 0)),
                  pl.BlockSpec((None, hq, 1), lambda b, s, pt: (b, 0, 0))]
                 + [_page_spec(HEAD_DIM, group, g) for g in range(group)]
                 + [pl.BlockSpec((None, HEAD_DIM, PAGE), lambda b, s, pt: (b, 0, 0))],
        out_specs=pl.BlockSpec((None, n_pages + 1, t_new, PAGE), lambda b, s, pt: (b, 0, 0, 0)),
        scratch_shapes=[pltpu.VMEM((n_pages + 1, t_new, PAGE), F32)] * 2)
    return pl.pallas_call(
        functools.partial(_sidx_kernel, n_pages=n_pages, group=group, topk=topk, t_new=t_new),
        grid_spec=grid_spec,
        out_shape=jax.ShapeDtypeStruct((bd, n_pages + 1, t_new, PAGE), F32),
        compiler_params=_cp(("arbitrary", "arbitrary")),
        name="sample_indexer",
    )(page_table, qi_rows, w_rows, *([cache_ik] * group), ki_new)


def _sattn_kernel(pt_ref, q_ref, *rest, n_pages, group, t_new):
    del pt_ref
    k_refs, v_refs = rest[:group], rest[group:2 * group]
    kn_ref, vn_ref, mb_ref, tabl_ref, tabn_ref, o_ref, m_s, l_s, acc_s = rest[2 * group:]
    step = pl.program_id(1)
    n_steps = n_pages // group

    @pl.when(step == 0)
    def _():
        m_s[...] = jnp.full(m_s.shape, NEG, F32)
        l_s[...] = jnp.zeros(l_s.shape, F32)
        acc_s[...] = jnp.zeros(acc_s.shape, F32)

    def logits(keys_t, mask_row, table):
        s = _dot(q_ref[...], keys_t) + jnp.concatenate([mb_ref[mask_row]] * N_HEADS, axis=0)
        return s if table is None else s + table

    def update(last):
        ks = [r[...].astype(BF16) for r in k_refs] + ([kn_ref[...].astype(BF16)] if last else [])
        vs = [r[...].astype(BF16) for r in v_refs] + ([vn_ref[...].astype(BF16)] if last else [])
        ss = [logits(ks[g], step * group + g, tabl_ref[...] if (last and g == group - 1) else None)
              for g in range(group)]
        if last:
            ss.append(logits(ks[group], n_pages, tabn_ref[...]))
        m_old = m_s[...]
        m_new = m_old
        for s in ss:
            m_new = jnp.maximum(m_new, jnp.max(s, axis=-1, keepdims=True))
        alpha = jnp.exp(m_old - m_new)
        l_new = alpha * l_s[...]
        acc = alpha * acc_s[...]
        for s, v in zip(ss, vs):
            pexp = jnp.exp(s - m_new)
            l_new = l_new + jnp.sum(pexp, axis=-1, keepdims=True)
            acc = acc + _dot_nt(pexp.astype(BF16), v)
        m_s[...] = m_new
        l_s[...] = l_new
        acc_s[...] = acc

    @pl.when(step < n_steps - 1)
    def _():
        update(False)

    @pl.when(step == n_steps - 1)
    def _():
        update(True)
        r = acc_s[...] / l_s[...]
        lane_head = lax.broadcasted_iota(I32, (t_new, A_W), 1) // HEAD_DIM
        out = jnp.zeros((t_new, A_W), F32)
        for h in range(N_HEADS):
            out = out + jnp.where(lane_head == h, r[h * t_new:(h + 1) * t_new], 0.0)
        o_ref[...] = out.astype(o_ref.dtype)


def _sample_attention(page_table, q_bd, cache_k, cache_v, k_new, v_new, mask, tab_last, tab_new, *, t_new, group):
    bd, n_pages = page_table.shape
    hq = q_bd.shape[1]
    group = min(group, n_pages)
    per_b = lambda b, s, pt: (b, 0, 0)
    const2 = lambda b, s, pt: (0, 0)
    pages = [pl.BlockSpec((None, A_W, PAGE), functools.partial(lambda b, s, pt, g: (pt[b, s * group + g], 0, 0), g=g))
             for g in range(group)]
    grid_spec = pltpu.PrefetchScalarGridSpec(
        num_scalar_prefetch=1,
        grid=(bd, n_pages // group),
        in_specs=[pl.BlockSpec((None, hq, A_W), per_b)] + pages + pages
                 + [pl.BlockSpec((None, A_W, PAGE), per_b), pl.BlockSpec((None, A_W, PAGE), per_b),
                    pl.BlockSpec((None, n_pages + 1, t_new, PAGE), lambda b, s, pt: (b, 0, 0, 0)),
                    pl.BlockSpec((hq, PAGE), const2), pl.BlockSpec((hq, PAGE), const2)],
        out_specs=pl.BlockSpec((None, t_new, A_W), per_b),
        scratch_shapes=[pltpu.VMEM((hq, 1), F32), pltpu.VMEM((hq, 1), F32), pltpu.VMEM((hq, A_W), F32)])
    return pl.pallas_call(
        functools.partial(_sattn_kernel, n_pages=n_pages, group=group, t_new=t_new),
        grid_spec=grid_spec,
        out_shape=jax.ShapeDtypeStruct((bd, t_new, A_W), BF16),
        compiler_params=_cp(("arbitrary", "arbitrary")),
        name="sample_attention",
    )(page_table, q_bd, *([cache_k] * group), *([cache_v] * group), k_new, v_new, mask, tab_last, tab_new)


ATTN_TILE = 256
ATTN_CHUNK = 2048
IDX_PAGE_GROUP = 16
ATTN_PAGE_GROUP = 8
S5_ROWS = 256
S5_SEGMENTS = 8


def _hi_lo(w):
    hi = w.astype(BF16)
    return hi, (w - hi.astype(F32)).astype(BF16)


def _even_layer_front(x, sc1, sh1, p, *, tm, qi_dtype):
    return _inproj(x, sc1, sh1, p['ng0'], p['w_in'], p['gq_pad'], p['gk_tiled'], p['gamma'],
                   tm=tm, qi_dtype=qi_dtype)


def _odd_layer(x, m, p, h0re, h0im, *, r, seg_len):
    sh1, sc1, g1, sh2, sc2, g2 = m
    zeros = jnp.zeros((S5_LG, r, LANES), F32)
    if seg_len is None:
        lre, lim = zeros, zeros
    else:
        lre, lim = _s5_local(x, sc1, sh1, p['ng1'], p['bb'], p['a3'], r=r, rows=S5_ROWS)
    x, fre, fim = _s5_full(x, sc1, sh1, p['ng1'], g1, p['bb'], p['a3'], p['cre'], p['cim'], p['dskip'],
                           h0re, h0im, lre, lim, p['gluw'], r=r, rows=S5_ROWS, seg_len=seg_len)
    x = _moe(x, sc2, sh2, g2, p['ng1b'], p['rw_hi'], p['rw_lo'], p['rb'], p['mw1'], p['mw3'], p['mw2'],
             tm=min(512, x.shape[0]))
    return x, _from_lane_groups(fre), _from_lane_groups(fim)


def kernel(x_prompt, x_sample, c_prompt, c_sample, cache_k, cache_v, cache_idx_k, state_hgrn, state_s5_re, state_s5_im, page_table, rel_bias, ada_w, ada_b, norm_g, w_in, qk_norm_g, hgrn_gamma, hgrn_onorm_g, w_out, ffn_w1, ffn_w3, ffn_w2, s5_lambda_re, s5_lambda_im, s5_log_dt, s5_b_re, s5_b_im, s5_c_re, s5_c_im, s5_d, s5_glu_w, moe_router_w, moe_router_b, moe_w1, moe_w3, moe_w2):
    bp, seq, d = x_prompt.shape
    bd, t_new, _ = x_sample.shape
    n_dec = bd * t_new
    n_phys = cache_k.shape[1]
    past = page_table.shape[1] * PAGE
    assert bp == 1 and d == D_MODEL and seq % 512 == 0
    assert n_dec == S5_ROWS and t_new == 8

    g_q, g_k = qk_norm_g[0, 0], qk_norm_g[0, 1]
    ar, ai, bbr_t, bbi_t = _s5_prep(s5_lambda_re[0], s5_lambda_im[0], s5_log_dt[0], s5_b_re[0], s5_b_im[0])
    bb = []
    for w in (_s5_blockdiag_in(bbr_t), _s5_blockdiag_in(bbi_t)):
        bb += list(_hi_lo(w))
    rw = jnp.pad(moe_router_w[0], ((0, 0), (0, LANES - N_EXPERTS)))
    rw_hi, rw_lo = _hi_lo(rw)
    p = dict(
        ng0=norm_g[0, 0][None], ng0b=norm_g[0, 1][None], ng1=norm_g[1, 0][None], ng1b=norm_g[1, 1][None],
        w_in=_pack_w_in(w_in[0]), gq_pad=jnp.tile(g_q, 2 * N_HEADS)[None], gk_tiled=jnp.tile(g_k, N_HEADS)[None],
        gamma=hgrn_gamma, onorm=jnp.tile(hgrn_onorm_g[0], N_HEADS)[None],
        wa=w_out[0, :A_W].astype(BF16), wb=w_out[0, A_W:].astype(BF16),
        w1=ffn_w1[0].astype(BF16), w3=ffn_w3[0].astype(BF16), w2=ffn_w2[0].astype(BF16),
        bb=bb, a3=[ar.reshape(S5_LG, 1, LANES), ai.reshape(S5_LG, 1, LANES)],
        cre=_s5_blockdiag_out(s5_c_re[0]).astype(BF16), cim=_s5_blockdiag_out(s5_c_im[0]).astype(BF16),
        dskip=s5_d[0][None], gluw=s5_glu_w[0].astype(BF16),
        rw_hi=rw_hi, rw_lo=rw_lo, rb=jnp.pad(moe_router_b[0], (0, LANES - N_EXPERTS))[None],
        mw1=moe_w1[0].astype(BF16), mw3=moe_w3[0].astype(BF16), mw2=moe_w2[0].astype(BF16),
    )

    c_rows = bp + bd
    c_all = jnp.concatenate([c_prompt, c_sample, jnp.zeros((-c_rows % 8, d), F32)], axis=0)
    mod = _ada(c_all, ada_w, ada_b)

    def mods(layer, rows, expand):
        m = mod[layer, rows]
        return tuple(expand(m[:, i * d:(i + 1) * d]) for i in range(6))

    xp = x_prompt.reshape(seq, d)
    sh1, sc1, g1, sh2, sc2, g2 = mods(0, slice(0, 1), lambda a: a)
    (q_pad, k_f, k_b, v_f, v_b, qi, kw, gq, gk, gl, gi, gg) = _even_layer_front(
        xp, sc1, sh1, p, tm=512, qi_dtype=BF16)
    ki_p = kw[:, :HEAD_DIM]
    mask = _prompt_indexer(qi.T, kw[:, HEAD_DIM:HEAD_DIM + N_HEADS].T, ki_p.astype(BF16),
                           tq=ATTN_TILE, tk=ATTN_TILE, topk=min(TOPK_MAX, seq // 4))
    a_out = _prompt_attention(q_pad.T, k_b, v_b.T, mask, rel_bias, tile=ATTN_TILE, chunk=ATTN_CHUNK).T
    b_out, st_p = _gla(gq, gk, gi, gl, gg, jnp.zeros((1, N_HEADS, HEAD_DIM, HEAD_DIM), F32), p['onorm'],
                       batch=1, chunk=64)
    xp = _even_tail(xp, a_out, b_out, g1, sc2, sh2, g2, p['ng0b'], p['wa'], p['wb'], p['w1'], p['w3'], p['w2'],
                    tm=512)
    seg_len = seq // S5_SEGMENTS
    to_seg = lambda a: a.reshape(S5_SEGMENTS, seg_len, d).transpose(1, 0, 2).reshape(seq, d)
    zero_state = jnp.zeros((S5_LG, S5_SEGMENTS, LANES), F32)
    xp, fre_p, fim_p = _odd_layer(to_seg(xp), mods(1, slice(0, 1), lambda a: a), p, zero_state, zero_state,
                                  r=S5_SEGMENTS, seg_len=seg_len)
    y_prompt = xp.reshape(seg_len, S5_SEGMENTS, d).transpose(1, 0, 2).reshape(bp, seq, d)

    xs = x_sample.reshape(n_dec, d)
    per_token = lambda a: jnp.repeat(a, t_new, axis=0)
    sh1, sc1, g1, sh2, sc2, g2 = mods(0, slice(bp, bp + bd), per_token)
    (q_pad_s, k_fs, _, v_fs, _, qi_s, kw_s, gq, gk, gl, gi, gg) = _even_layer_front(
        xs, sc1, sh1, p, tm=n_dec, qi_dtype=BF16)
    heads_first = lambda a: jnp.transpose(a, (0, 2, 1, 3)).reshape(bd, N_HEADS * t_new, a.shape[-1])
    qi_rows = heads_first(qi_s.reshape(bd, t_new, N_HEADS, HEAD_DIM))
    w_rows = heads_first(kw_s[:, HEAD_DIM:HEAD_DIM + N_HEADS].reshape(bd, t_new, N_HEADS, 1))
    pad_new = lambda a: jnp.swapaxes(jnp.pad(a.reshape(bd, t_new, -1), ((0, 0), (0, PAGE - t_new), (0, 0))), 1, 2)
    ki_s = kw_s[:, :HEAD_DIM]
    mask_s = _sample_indexer(page_table, qi_rows, w_rows, jnp.swapaxes(cache_idx_k[0], 1, 2), pad_new(ki_s),
                             topk=min(TOPK_MAX, (past + t_new) // 4), t_new=t_new, group=IDX_PAGE_GROUP)
    q4 = q_pad_s.reshape(bd, t_new, N_HEADS, LANES)
    even = (jnp.arange(N_HEADS) % 2 == 0)[None, None, :, None]
    q_nat = jnp.where(even, q4[..., :HEAD_DIM], q4[..., HEAD_DIM:])
    q_bd = jnp.einsum('bthd,hg->bhtgd', q_nat, jnp.eye(N_HEADS, dtype=q_nat.dtype))
    q_bd = q_bd.reshape(bd, N_HEADS * t_new, A_W)
    qpos = np.arange(t_new)[:, None]
    col = np.arange(PAGE)[None, :]
    rows_hq = lambda t: t.reshape(N_HEADS * t_new, PAGE)
    tab_last = rows_hq(_bias_by_distance(rel_bias, PAGE + qpos - col))
    tab_new = rows_hq(_bias_by_distance(rel_bias, qpos - col))
    feature_major = lambda c: jnp.transpose(c[0], (0, 2, 3, 1)).reshape(n_phys, A_W, PAGE)
    a_out_s = _sample_attention(page_table, q_bd, feature_major(cache_k), feature_major(cache_v),
                                pad_new(k_fs), pad_new(v_fs), mask_s, tab_last, tab_new,
                                t_new=t_new, group=ATTN_PAGE_GROUP).reshape(n_dec, A_W)
    gla_chunk = 64
    pad_t = lambda a: jnp.pad(a.reshape(bd, t_new, A_W), ((0, 0), (0, gla_chunk - t_new), (0, 0))).reshape(-1, A_W)
    b_out_s, st_s = _gla(pad_t(gq), pad_t(gk), pad_t(gi), pad_t(gl), pad_t(gg),
                         state_hgrn[0], p['onorm'], batch=bd, chunk=gla_chunk)
    b_out_s = b_out_s.reshape(bd, gla_chunk, A_W)[:, :t_new].reshape(n_dec, A_W)
    xs = _even_tail(xs, a_out_s, b_out_s, g1, sc2, sh2, g2, p['ng0b'], p['wa'], p['wb'],
                    p['w1'], p['w3'], p['w2'], tm=n_dec)
    step_major = lambda a: a.reshape(bd, t_new, d).transpose(1, 0, 2).reshape(n_dec, d)
    xs, fre_s, fim_s = _odd_layer(step_major(xs), mods(1, slice(bp, bp + bd), lambda a: jnp.tile(a, (t_new, 1))), p,
                                  _to_lane_groups(state_s5_re[0].reshape(bd, S5_W)),
                                  _to_lane_groups(state_s5_im[0].reshape(bd, S5_W)), r=bd, seg_len=None)
    y_sample = xs.reshape(t_new, bd, d).transpose(1, 0, 2)

    heads = lambda a, b, t: a.reshape(1, b, t, N_HEADS, HEAD_DIM)
    s5_state = lambda f: f.reshape(1, -1, S5_GROUPS, S5_STATE)
    return (y_prompt, y_sample,
            heads(k_f, bp, seq), heads(v_f, bp, seq), ki_p.reshape(1, bp, seq, HEAD_DIM),
            st_p[None], s5_state(fre_p[S5_SEGMENTS - 1:]), s5_state(fim_p[S5_SEGMENTS - 1:]),
            heads(k_fs, bd, t_new), heads(v_fs, bd, t_new), ki_s.reshape(1, bd, t_new, HEAD_DIM),
            st_s[None], s5_state(fre_s), s5_state(fim_s))
```

```python
import functools
import math

import numpy as np
import jax
import jax.numpy as jnp
from jax import lax
from jax.experimental import pallas as pl
from jax.experimental.pallas import tpu as pltpu

F32 = jnp.float32
BF16 = jnp.bfloat16
I32 = jnp.int32

D_MODEL = 1024
N_HEADS = 8
HEAD_DIM = 64
A_W = N_HEADS * HEAD_DIM
LANES = 128
TOPK_MAX = 256
PAGE = 128
REL_BUCKETS = 32
REL_MAX_DIST = 128
S5_GROUPS = 64
S5_CH = 16
S5_STATE = 64
S5_W = S5_GROUPS * S5_STATE
S5_BLK_GROUPS = 8
S5_NBLK = S5_GROUPS // S5_BLK_GROUPS
D_FF = 2816
N_EXPERTS = 8
MOE_FF = 1408
EPS = 1e-6
NEG = -1e30
INT_MIN = -2 ** 31
VMEM_LIMIT = 56 * 2 ** 20


def _cp(sem, vmem=VMEM_LIMIT):
    return pltpu.CompilerParams(dimension_semantics=sem, vmem_limit_bytes=vmem)


def _resident(shape):
    n = len(shape)
    return pl.BlockSpec(shape, lambda *_: (0,) * n, pipeline_mode=pl.Buffered(1))


def _dot(a, b):
    return jnp.dot(a, b, preferred_element_type=F32)


def _dot_nt(a, b):
    return lax.dot_general(a, b, (((1,), (1,)), ((), ())), preferred_element_type=F32)


def _sigmoid(x):
    return 1.0 / (1.0 + jnp.exp(-x))


def _silu(x):
    return x * _sigmoid(x)


def _gelu_tanh(x):
    return 0.5 * x * (1.0 + jnp.tanh(math.sqrt(2.0 / math.pi) * (x + 0.044715 * (x * x * x))))


def _norm_mod(x, g, sc, sh):
    ms = jnp.mean(x * x, axis=-1, keepdims=True)
    return (x * lax.rsqrt(ms + EPS) * g) * (1.0 + sc) + sh


def _seg_matrix(n, seg, dtype, scale=1.0):
    r = lax.broadcasted_iota(I32, (n, n), 0) // seg
    c = lax.broadcasted_iota(I32, (n, n), 1) // seg
    return jnp.where(r == c, scale, 0.0).astype(dtype)


def _ada_kernel(c_ref, w_ref, b_ref, o_ref):
    c = c_ref[...]
    o_ref[...] = _dot(_silu(c).astype(BF16), w_ref[...]) + b_ref[...]


def _ada(c_all, ada_w, ada_b):
    depth, d, n6 = ada_w.shape
    rows = c_all.shape[0]
    tn = 1536
    return pl.pallas_call(
        _ada_kernel,
        grid=(depth, n6 // tn),
        in_specs=[pl.BlockSpec((rows, d), lambda l, j: (0, 0)),
                  pl.BlockSpec((None, d, tn), lambda l, j: (l, 0, j)),
                  pl.BlockSpec((None, 1, tn), lambda l, j: (l, 0, j))],
        out_specs=pl.BlockSpec((None, rows, tn), lambda l, j: (l, 0, j)),
        out_shape=jax.ShapeDtypeStruct((depth, rows, n6), F32),
        compiler_params=_cp(("arbitrary", "arbitrary")),
        name="ada_mod",
    )(c_all, ada_w, ada_b.reshape(depth, 1, n6))


W_OFF_Q, W_OFF_K, W_OFF_V, W_OFF_QI, W_OFF_KW, W_OFF_B = 0, 1024, 1536, 2048, 2560, 2688


def _pack_w_in(w):
    d = w.shape[0]
    z64 = jnp.zeros((d, N_HEADS, HEAD_DIM), w.dtype)
    q = w[:, 0:512].reshape(d, N_HEADS, HEAD_DIM)
    even = (jnp.arange(N_HEADS) % 2 == 0)[None, :, None]
    q_pad = jnp.concatenate([jnp.where(even, q, z64), jnp.where(even, z64, q)], axis=-1).reshape(d, 1024)
    kw = jnp.concatenate([w[:, 2048:2120], jnp.zeros((d, 56), w.dtype)], axis=-1)
    return jnp.concatenate([q_pad, w[:, 512:2048], kw, w[:, 2120:4168]], axis=-1).astype(BF16)


def _inproj_kernel(x_ref, sc_ref, sh_ref, ng_ref, w_ref, gq_ref, gk_ref, gam_ref,
                   q_ref, kf_ref, kb_ref, vf_ref, vb_ref, qi_ref, kw_ref,
                   gq_o, gk_o, gl_o, gi_o, gg_o):
    h = _norm_mod(x_ref[...], ng_ref[...], sc_ref[...], sh_ref[...]).astype(BF16)
    ones_seg = jnp.full((LANES, LANES), 1.0 / HEAD_DIM, BF16)
    pair_seg = _seg_matrix(LANES, HEAD_DIM, BF16, 1.0 / HEAD_DIM)

    def proj(off, n):
        return _dot(h, w_ref[:, off:off + n])

    pq = proj(W_OFF_Q, 1024)
    for j in range(8):
        blk = pq[:, j * LANES:(j + 1) * LANES]
        ms = _dot((blk * blk).astype(BF16), ones_seg)
        q_ref[:, j * LANES:(j + 1) * LANES] = (
            blk * lax.rsqrt(ms + EPS) * gq_ref[:, j * LANES:(j + 1) * LANES] * 0.125).astype(q_ref.dtype)
    pk = proj(W_OFF_K, 512)
    for j in range(4):
        blk = pk[:, j * LANES:(j + 1) * LANES]
        ms = _dot((blk * blk).astype(BF16), pair_seg)
        kn = blk * lax.rsqrt(ms + EPS) * gk_ref[:, j * LANES:(j + 1) * LANES]
        kf_ref[:, j * LANES:(j + 1) * LANES] = kn
        kb_ref[:, j * LANES:(j + 1) * LANES] = kn.astype(BF16)
    pv = proj(W_OFF_V, 512)
    vf_ref[...] = pv
    vb_ref[...] = pv.astype(BF16)
    qi_ref[...] = (proj(W_OFF_QI, 512) * 0.125).astype(qi_ref.dtype)
    lane = lax.broadcasted_iota(I32, (1, LANES), 1)
    kw_ref[...] = proj(W_OFF_KW, LANES) * jnp.where(lane < HEAD_DIM, 1.0, N_HEADS ** -0.5)
    gq_o[...] = proj(W_OFF_B, 512) * 0.125
    gam = gam_ref[...]
    gmax = jnp.max(gam, axis=0, keepdims=True)
    ge = jnp.exp(gam - gmax)
    lb = ge[0:1, :] / jnp.sum(ge, axis=0, keepdims=True)
    f = lb + (1.0 - lb) * _sigmoid(proj(W_OFF_B + 512, 512))
    gk_o[...] = 1.0 - f
    gl_o[...] = jnp.log(f)
    gi_o[...] = proj(W_OFF_B + 1024, 512)
    gg_o[...] = proj(W_OFF_B + 1536, 512)


def _inproj(x, sc, sh, ng, w_packed, gq_pad, gk_tiled, gamma, *, tm, qi_dtype):
    n, d = x.shape
    row = lambda i: (i, 0)
    mod_spec = (pl.BlockSpec((1, d), lambda i: (0, 0)) if sc.shape[0] == 1
                else pl.BlockSpec((tm, d), row))
    outs = [((n, 1024), BF16), ((n, 512), F32), ((n, 512), BF16), ((n, 512), F32), ((n, 512), BF16),
            ((n, 512), qi_dtype), ((n, LANES), F32)] + [((n, 512), F32)] * 5
    return pl.pallas_call(
        _inproj_kernel,
        grid=(n // tm,),
        in_specs=[pl.BlockSpec((tm, d), row), mod_spec, mod_spec,
                  _resident((1, d)), _resident(w_packed.shape), _resident((1, 1024)),
                  _resident((1, 512)), _resident(gamma.shape)],
        out_specs=[pl.BlockSpec((tm, s[1]), row) for s, _ in outs],
        out_shape=[jax.ShapeDtypeStruct(s, dt) for s, dt in outs],
        compiler_params=_cp(("arbitrary",)),
        name="in_proj",
    )(x, sc, sh, ng, w_packed, gq_pad, gk_tiled, gamma)


def _key_to_f32(key):
    neg = key < 0
    mag = jnp.where(neg, -key, key)
    return lax.bitcast_convert_type(jnp.where(neg, mag | jnp.int32(INT_MIN), mag), F32)


def _f32_to_key(x):
    bits = lax.bitcast_convert_type(x, I32)
    return jnp.where(bits < 0, -(bits & jnp.int32(0x7FFFFFFF)), bits)


def _kth_largest(count_ge, count_gt, shape, topk, bounds=None):
    if bounds is None:
        bounds = (jnp.full(shape, -jnp.inf, F32), jnp.full(shape, jnp.inf, F32))

    def cond(state):
        lo, hi, done, _ = state
        return jnp.max(jnp.where(jnp.logical_and(lo != hi, done == 0), 1, 0)) == 1

    def body(state):
        lo, hi, done, thr = state
        mid = (lo | hi) - ((lo ^ hi) >> 1)
        mid_f = _key_to_f32(mid)
        cnt = count_ge(mid_f)
        enough = cnt >= topk
        hit = jnp.logical_and(cnt == topk, done == 0)
        return (jnp.where(enough, mid, lo), jnp.where(enough, hi, mid - 1),
                jnp.where(hit, 1, done), jnp.where(hit, mid_f, thr))

    lo, hi = _f32_to_key(bounds[0]), _f32_to_key(bounds[1])
    zero = jnp.zeros(shape, F32)
    n_pos, n_nonneg = count_gt(zero), count_ge(zero)
    done = jnp.where(jnp.logical_and(n_pos < topk, n_nonneg >= topk), 1, 0)
    lo = jnp.where(n_pos >= topk, jnp.maximum(lo, 1), lo)
    hi = jnp.where(n_nonneg < topk, jnp.minimum(hi, -1), hi)
    lo, _, done, thr = lax.while_loop(cond, body, (lo, hi, done, jnp.zeros(shape, F32)))
    return jnp.where(done == 1, thr, _key_to_f32(lo))


def _pidx_kernel(qit_ref, wt_ref, ki_ref, bias_ref, sc_ref, cm_ref, *, tq, tk, topk, seq):
    assert topk <= tk
    q0 = pl.program_id(0) * tq
    n_kt = (q0 + tq + tk - 1) // tk
    key = lax.broadcasted_iota(I32, (tk, tq), 0)
    qry = lax.broadcasted_iota(I32, (tk, tq), 1) + q0

    def tile_off(kt):
        return pl.multiple_of(kt * tk, tk)

    def scores(kt, _):
        off = tile_off(kt)
        ks = ki_ref[pl.ds(off, tk), :]
        acc = jnp.zeros((tk, tq), F32)
        for h in range(N_HEADS):
            x = _dot(ks, qit_ref[h * HEAD_DIM:(h + 1) * HEAD_DIM, :])
            acc = acc + wt_ref[h:h + 1, :] * jnp.maximum(x, 0.0)
        acc = jnp.where(key + off <= qry, acc, -jnp.inf)
        sc_ref[pl.ds(off, tk), :] = acc
        cm_ref[...] = jnp.maximum(cm_ref[...], acc)
        return 0
    cm_ref[...] = jnp.full((tk, tq), -jnp.inf, F32)
    n_pairs = (n_kt + 1) // 2
    lax.fori_loop(0, 2 * n_pairs, scores, 0)
    bounds = (jnp.min(cm_ref[...], axis=0, keepdims=True), jnp.max(cm_ref[...], axis=0, keepdims=True))

    def counter(cmp):
        def count(thr):
            def body(kp, c):
                off = pl.multiple_of(kp * (2 * tk), 2 * tk)
                hit = jnp.where(cmp(sc_ref[pl.ds(off, 2 * tk), :], thr), 1.0, 0.0)
                return c + jnp.sum(hit.reshape(2 * tk // 64, 8, 8, tq), axis=0)
            c = lax.fori_loop(0, n_pairs, body, jnp.zeros((8, 8, tq), F32))
            return jnp.sum(jnp.sum(c, axis=0), axis=0, keepdims=True)
        return count

    count_ge, count_gt = counter(lambda s, t: s >= t), counter(lambda s, t: s > t)
    thr = _kth_largest(count_ge, count_gt, (1, tq), topk, bounds)
    need = topk - count_gt(thr)
    lower = jnp.where(lax.broadcasted_iota(I32, (tk, tk), 0) > lax.broadcasted_iota(I32, (tk, tk), 1),
                      1.0, 0.0).astype(BF16)

    def select(kt, ties_before):
        off = tile_off(kt)
        s = sc_ref[pl.ds(off, tk), :]
        eq = jnp.where(s == thr, 1.0, 0.0)
        rank = _dot(lower, eq.astype(BF16)) + ties_before
        keep = jnp.where(s > thr, 0.0, jnp.where(s == thr, jnp.where(rank < need, 0.0, NEG), NEG))
        bias_ref[pl.ds(off, tk), :] = jnp.where(key + off <= qry, keep, NEG).astype(BF16)
        return ties_before + jnp.sum(eq, axis=0, keepdims=True)
    lax.fori_loop(0, n_kt, select, jnp.zeros((1, tq), F32))

    def fill(kt, _):
        bias_ref[pl.ds(tile_off(kt), tk), :] = jnp.full((tk, tq), NEG, BF16)
        return 0
    lax.fori_loop(n_kt, seq // tk, fill, 0)


def _prompt_indexer(qi_t, w_t, ki, *, tq, tk, topk):
    seq = ki.shape[0]
    return pl.pallas_call(
        functools.partial(_pidx_kernel, tq=tq, tk=tk, topk=topk, seq=seq),
        grid=(seq // tq,),
        in_specs=[pl.BlockSpec((A_W, tq), lambda i: (0, i)),
                  pl.BlockSpec((N_HEADS, tq), lambda i: (0, i)),
                  _resident(ki.shape)],
        out_specs=pl.BlockSpec((None, seq, tq), lambda i: (i, 0, 0)),
        out_shape=jax.ShapeDtypeStruct((seq // tq, seq, tq), BF16),
        scratch_shapes=[pltpu.VMEM((seq, tq), F32), pltpu.VMEM((tk, tq), F32)],
        compiler_params=_cp(("arbitrary",)),
        name="prompt_indexer",
    )(qi_t, w_t, ki)


def _t5_bucket_table():
    n = np.arange(REL_MAX_DIST, dtype=np.int64)
    max_exact = REL_BUCKETS // 2
    nf = np.maximum(n, 1).astype(np.float32)
    large = max_exact + (np.log(nf / np.float32(max_exact)) / np.float32(math.log(REL_MAX_DIST / max_exact))
                         * np.float32(REL_BUCKETS - max_exact)).astype(np.int32)
    large = np.minimum(large, REL_BUCKETS - 1)
    return np.where(n < max_exact, n, large).astype(np.int32)


def _bias_by_distance(rel_bias, dist):
    table = _t5_bucket_table()
    bucket = np.where(dist >= REL_MAX_DIST, REL_BUCKETS - 1, table[np.clip(dist, 0, REL_MAX_DIST - 1)])
    b = jnp.moveaxis(rel_bias[bucket], -1, 0)
    far = rel_bias[REL_BUCKETS - 1].reshape((N_HEADS,) + (1,) * dist.ndim)
    return (b - far).astype(F32)


V_ROWS = HEAD_DIM + 16


def _pattn_kernel(qt_ref, k_ref, vt_ref, mask_ref, tab_ref, ot_ref, m_ref, acc_ref, *slots, tile, chunk):
    slot_a, slot_b = slots[:4], slots[4:]
    qb = pl.program_id(0)
    first = pl.program_id(1) * (chunk // tile)
    n_tiles = chunk // tile

    @pl.when(pl.program_id(1) == 0)
    def _():
        m_ref[...] = jnp.full(m_ref.shape, NEG, F32)
        acc_ref[...] = jnp.zeros(acc_ref.shape, F32)

    def scores(j, slot, near):
        s_ref, mx_ref, _, _ = slot
        off = pl.multiple_of(j * tile, tile)
        mb = mask_ref[pl.ds(off, tile), :].astype(F32)
        for h in range(N_HEADS):
            pr = h // 2
            s = _dot(k_ref[pl.ds(off, tile), pr * LANES:(pr + 1) * LANES], qt_ref[h * LANES:(h + 1) * LANES, :])
            s = s + mb
            if near is not None:
                s = s + tab_ref[near, h]
            s_ref[h] = s
            mx_ref[h] = jnp.max(s, axis=0, keepdims=True)

    def accumulate(j, slot):
        s_ref, mx_ref, p_ref, al_ref = slot
        off = pl.multiple_of(j * tile, tile)
        for h in range(N_HEADS):
            m_old = m_ref[h]
            m_new = jnp.maximum(m_old, mx_ref[h])
            alpha = jnp.exp(m_old - m_new)
            p_ref[h] = jnp.exp(s_ref[h] - m_new).astype(BF16)
            al_ref[h] = alpha
            m_ref[h] = m_new
        for h in range(N_HEADS):
            pv = _dot(vt_ref[h * V_ROWS:(h + 1) * V_ROWS, pl.ds(off, tile)], p_ref[h])
            acc_ref[h] = al_ref[h] * acc_ref[h] + pv

    all_far = first + n_tiles <= qb - 1

    @pl.when(all_far)
    def _():
        scores(0, slot_a, None)

        def pair(i, _):
            scores(2 * i + 1, slot_b, None)
            accumulate(2 * i, slot_a)
            scores(jnp.minimum(2 * i + 2, n_tiles - 1), slot_a, None)
            accumulate(2 * i + 1, slot_b)
            return 0
        lax.fori_loop(0, n_tiles // 2, pair, 0)

    @pl.when(jnp.logical_not(all_far))
    def _():
        def far_body(j, _):
            scores(j, slot_a, None)
            accumulate(j, slot_a)
            return 0
        lax.fori_loop(0, jnp.clip(qb - 1 - first, 0, n_tiles), far_body, 0)
        for near, j in ((1, qb - 1 - first), (0, qb - first)):
            @pl.when(jnp.logical_and(j >= 0, j < n_tiles))
            def _():
                scores(j, slot_a, near)
                accumulate(j, slot_a)

    @pl.when(pl.program_id(1) == pl.num_programs(1) - 1)
    def _():
        for h in range(N_HEADS):
            acc = acc_ref[h]
            ot_ref[h * HEAD_DIM:(h + 1) * HEAD_DIM, :] = (
                acc[:HEAD_DIM] / acc[HEAD_DIM:HEAD_DIM + 1]).astype(ot_ref.dtype)


def _near_bias_tables(rel_bias, tile):
    period = 2 * tile
    j = np.arange(period)
    offset = np.where(j < tile, j, j - period)
    w = _bias_by_distance(rel_bias, np.stack([offset, tile + offset]))
    skew = jnp.tile(w, (1, 1, tile))[..., :tile * (period - 1)].reshape(N_HEADS, 2, tile, period - 1)
    return jnp.moveaxis(skew[..., :tile], 0, 1)


def _prompt_attention(q_t, k_bf, v_t, mask, rel_bias, *, tile, chunk):
    seq = k_bf.shape[0]
    chunk = min(chunk, seq)
    tab = _near_bias_tables(rel_bias, tile)
    v_ext = jnp.concatenate([v_t.reshape(N_HEADS, HEAD_DIM, seq),
                             jnp.ones((N_HEADS, V_ROWS - HEAD_DIM, seq), v_t.dtype)], axis=1)
    v_ext = v_ext.reshape(N_HEADS * V_ROWS, seq)
    last_chunk = lambda i: (i * tile + tile - 1) // chunk
    return pl.pallas_call(
        functools.partial(_pattn_kernel, tile=tile, chunk=chunk),
        grid=(seq // tile, seq // chunk),
        in_specs=[pl.BlockSpec((N_HEADS * LANES, tile), lambda i, c: (0, i)),
                  pl.BlockSpec((chunk, A_W), lambda i, c: (jnp.minimum(c, last_chunk(i)), 0)),
                  pl.BlockSpec((N_HEADS * V_ROWS, chunk), lambda i, c: (0, jnp.minimum(c, last_chunk(i)))),
                  pl.BlockSpec((None, chunk, tile), lambda i, c: (i, jnp.minimum(c, last_chunk(i)), 0)),
                  pl.BlockSpec(tab.shape, lambda i, c: (0, 0, 0, 0), pipeline_mode=pl.Buffered(1))],
        out_specs=pl.BlockSpec((A_W, tile), lambda i, c: (0, i)),
        out_shape=jax.ShapeDtypeStruct((A_W, seq), BF16),
        scratch_shapes=[pltpu.VMEM((N_HEADS, 1, tile), F32), pltpu.VMEM((N_HEADS, V_ROWS, tile), F32)]
                       + [pltpu.VMEM((N_HEADS, tile, tile), F32), pltpu.VMEM((N_HEADS, 1, tile), F32),
                          pltpu.VMEM((N_HEADS, tile, tile), BF16), pltpu.VMEM((N_HEADS, 1, tile), F32)] * 2,
        compiler_params=_cp(("arbitrary", "arbitrary")),
        name="prompt_attention",
    )(q_t, k_bf, v_ext, mask, tab)


def _cumsum_rows(x):
    c = x.shape[0]
    tri = jnp.where(lax.broadcasted_iota(I32, (c, c), 0) >= lax.broadcasted_iota(I32, (c, c), 1),
                    1.0, 0.0).astype(BF16)
    hi = x.astype(BF16)
    r1 = x - hi.astype(F32)
    mid = r1.astype(BF16)
    lo = (r1 - mid.astype(F32)).astype(BF16)
    return _dot(tri, hi) + (_dot(tri, mid) + _dot(tri, lo))


def _gla_kernel(q_ref, k_ref, v_ref, g_ref, gate_ref, s0_ref, on_ref, seg_ref, diag_ref, o_ref, sfin_ref,
                st_ref, b_ref, oi_ref, *, chunk):
    @pl.when(pl.program_id(1) == 0)
    def _():
        st_ref[...] = jnp.zeros(st_ref.shape, F32)
        for h in range(N_HEADS):
            st_ref[h * HEAD_DIM:(h + 1) * HEAD_DIM, h * HEAD_DIM:(h + 1) * HEAD_DIM] = s0_ref[h]

    w = q_ref.shape[-1]
    b = _cumsum_rows(g_ref[...])
    b_ref[...] = b
    k = k_ref[...]
    v = v_ref[...]
    seg = seg_ref[...]
    st = st_ref[...]
    o_inter = _dot_nt((q_ref[...] * jnp.exp(b)).astype(BF16), st.astype(BF16))
    rows8 = lax.broadcasted_iota(I32, (8, w), 0)
    for grp in range(chunk // 8):
        m = 8 * grp
        out_rows = []
        for t in range(m, m + 8):
            bt, qt = b_ref[t:t + 1, :], q_ref[t:t + 1, :]
            own = jnp.where(rows8 <= t - m, jnp.exp(bt - b_ref[m:m + 8, :]), 0.0) * k_ref[m:m + 8, :]
            if grp:
                prod = jnp.concatenate([jnp.exp(bt - b_ref[0:m, :]) * k_ref[0:m, :], own], axis=0)
            else:
                prod = own
            att = _dot((qt * prod).astype(BF16), seg)
            out_rows.append(jnp.sum(att * v_ref[0:m + 8, :], axis=0, keepdims=True))
        oi_ref[m:m + 8, :] = jnp.concatenate(out_rows, axis=0)

    o = o_inter + oi_ref[...]
    ms = _dot((o * o).astype(BF16), seg) * (1.0 / HEAD_DIM)
    o_ref[...] = (o * lax.rsqrt(ms + EPS) * on_ref[...] * _silu(gate_ref[...])).astype(o_ref.dtype)

    b_last = b[chunk - 1:chunk, :]
    kd = (k * jnp.exp(b_last - b)).astype(BF16)
    upd = _dot(v.T.astype(BF16), kd)
    st_new = st * jnp.exp(b_last) + upd * diag_ref[...]
    st_ref[...] = st_new

    @pl.when(pl.program_id(1) == pl.num_programs(1) - 1)
    def _():
        for h in range(N_HEADS):
            sfin_ref[h] = st_new[h * HEAD_DIM:(h + 1) * HEAD_DIM, h * HEAD_DIM:(h + 1) * HEAD_DIM]


def _gla(gq, gk, gv, glog, gate, s0, onorm_tiled, *, batch, chunk):
    n, w = gq.shape
    nc = n // batch // chunk
    row = lambda b, c: (b * nc + c, 0)
    tile = pl.BlockSpec((chunk, w), row)
    state = pl.BlockSpec((None, N_HEADS, HEAD_DIM, HEAD_DIM), lambda b, c: (b, 0, 0, 0))
    same_head = np.equal.outer(np.arange(w) // HEAD_DIM, np.arange(w) // HEAD_DIM).astype(np.float32)
    const = lambda shape: pl.BlockSpec(shape, lambda b, c: (0, 0), pipeline_mode=pl.Buffered(1))
    o, s_fin = pl.pallas_call(
        functools.partial(_gla_kernel, chunk=chunk),
        grid=(batch, nc),
        in_specs=[tile] * 5 + [state, const((1, w)), const((w, w)), const((w, w))],
        out_specs=[tile, state],
        out_shape=[jax.ShapeDtypeStruct((n, w), BF16),
                   jax.ShapeDtypeStruct((batch, N_HEADS, HEAD_DIM, HEAD_DIM), F32)],
        scratch_shapes=[pltpu.VMEM((w, w), F32), pltpu.VMEM((chunk, w), F32), pltpu.VMEM((chunk, w), F32)],
        compiler_params=_cp(("arbitrary", "arbitrary")),
        name="hgrn2_gla",
    )(gq, gk, gv, glog, gate, jnp.swapaxes(s0, 2, 3), onorm_tiled,
      jnp.asarray(same_head, BF16), jnp.asarray(same_head, F32))
    return o, jnp.swapaxes(s_fin, 2, 3)


def _even_tail_kernel(x_ref, a_ref, b_ref, g1_ref, sc_ref, sh_ref, g2_ref, ng_ref,
                      wa_ref, wb_ref, w1_ref, w3_ref, w2_ref, o_ref, *, ff_split):
    mix = _dot(a_ref[...], wa_ref[...]) + _dot(b_ref[...], wb_ref[...])
    x1 = x_ref[...] + g1_ref[...] * mix
    h = _norm_mod(x1, ng_ref[...], sc_ref[...], sh_ref[...]).astype(BF16)
    ff = jnp.zeros(x1.shape, F32)
    step = w1_ref.shape[1] // ff_split
    for j in range(ff_split):
        a = _dot(h, w1_ref[:, j * step:(j + 1) * step])
        g = _dot(h, w3_ref[:, j * step:(j + 1) * step])
        ff = ff + _dot((_silu(a) * g).astype(BF16), w2_ref[j * step:(j + 1) * step, :])
    o_ref[...] = x1 + g2_ref[...] * ff


def _mod_spec(m, tm, d):
    return (pl.BlockSpec((1, d), lambda i: (0, 0)) if m.shape[0] == 1
            else pl.BlockSpec((tm, d), lambda i: (i, 0)))


def _even_tail(x, a_out, b_out, g1, sc2, sh2, g2, ng2, wa, wb, w1, w3, w2, *, tm):
    n, d = x.shape
    row = lambda i: (i, 0)
    return pl.pallas_call(
        functools.partial(_even_tail_kernel, ff_split=2),
        grid=(n // tm,),
        in_specs=[pl.BlockSpec((tm, d), row), pl.BlockSpec((tm, A_W), row), pl.BlockSpec((tm, A_W), row),
                  _mod_spec(g1, tm, d), _mod_spec(sc2, tm, d), _mod_spec(sh2, tm, d), _mod_spec(g2, tm, d),
                  _resident((1, d)), _resident(wa.shape), _resident(wb.shape),
                  _resident(w1.shape), _resident(w3.shape), _resident(w2.shape)],
        out_specs=pl.BlockSpec((tm, d), row),
        out_shape=jax.ShapeDtypeStruct((n, d), F32),
        compiler_params=_cp(("arbitrary",)),
        name="even_tail",
    )(x, a_out, b_out, g1, sc2, sh2, g2, ng2, wa, wb, w1, w3, w2)


S5_LG = S5_W // LANES
S5_LG_PER_BLK = S5_LG // S5_NBLK


def _s5_prep_kernel(lr_ref, li_ref, ldt_ref, br_ref, bi_ref, ar_o, ai_o, bbr_o, bbi_o):
    lr = jnp.minimum(lr_ref[...], -1e-4)
    li = li_ref[...]
    dt = jnp.exp(ldt_ref[...])
    mag = jnp.exp(lr * dt)
    a_re = mag * jnp.cos(li * dt)
    a_im = mag * jnp.sin(li * dt)
    den = lr * lr + li * li
    nr = a_re - 1.0
    coef_re = (nr * lr + a_im * li) / den
    coef_im = (a_im * lr - nr * li) / den
    ar_o[...] = a_re
    ai_o[...] = a_im
    br = br_ref[...]
    bi = bi_ref[...]
    bbr_o[...] = coef_re * br - coef_im * bi
    bbi_o[...] = coef_re * bi + coef_im * br


def _s5_prep(lam_re, lam_im, log_dt, b_re, b_im):
    g, p = lam_re.shape
    v3 = lambda a: a.reshape(g, 1, p)
    bt = lambda b: jnp.transpose(b, (0, 2, 1))
    sds = jax.ShapeDtypeStruct
    return pl.pallas_call(
        _s5_prep_kernel,
        out_shape=[sds((g, 1, p), F32), sds((g, 1, p), F32), sds((g, S5_CH, p), F32), sds((g, S5_CH, p), F32)],
        name="s5_prep",
    )(v3(lam_re), v3(lam_im), log_dt.reshape(g, 1, 1), bt(b_re), bt(b_im))


def _s5_blockdiag_in(bb_t):
    x = bb_t.reshape(S5_NBLK, S5_BLK_GROUPS, S5_CH, S5_STATE)
    bd = jnp.einsum('bgcp,gh->bgchp', x, jnp.eye(S5_BLK_GROUPS, dtype=x.dtype))
    return bd.reshape(S5_NBLK, S5_BLK_GROUPS * S5_CH, S5_BLK_GROUPS * S5_STATE)


def _s5_blockdiag_out(c):
    x = c.reshape(S5_NBLK, S5_BLK_GROUPS, S5_CH, S5_STATE)
    bd = jnp.einsum('bgcp,gh->bgphc', x, jnp.eye(S5_BLK_GROUPS, dtype=x.dtype))
    return bd.reshape(S5_NBLK, S5_BLK_GROUPS * S5_STATE, S5_BLK_GROUPS * S5_CH)


def _s5_project_in(u, bbr_ref, bbi_ref, xre_s, xim_s):
    for blk in range(S5_NBLK):
        ub = u[:, blk * LANES:(blk + 1) * LANES].astype(BF16)
        for w_ref, dst in ((bbr_ref, xre_s), (bbi_ref, xim_s)):
            r = _dot(ub, w_ref[blk])
            for j in range(S5_LG_PER_BLK):
                dst[blk * S5_LG_PER_BLK + j] = r[:, j * LANES:(j + 1) * LANES]


def _s5_scan(xre_s, xim_s, st_re, st_im, ar_ref, ai_ref, *, n_steps, r, store):
    def blk_body(blk, _):
        base = blk * S5_LG_PER_BLK
        ar = [jnp.broadcast_to(ar_ref[base + j], (r, LANES)) for j in range(S5_LG_PER_BLK)]
        ai = [jnp.broadcast_to(ai_ref[base + j], (r, LANES)) for j in range(S5_LG_PER_BLK)]

        def step(s, carry):
            r0 = pl.multiple_of(s * r, r)
            new = []
            for j in range(S5_LG_PER_BLK):
                sr, si = carry[2 * j], carry[2 * j + 1]
                nr = ar[j] * sr - ai[j] * si + xre_s[base + j, pl.ds(r0, r), :]
                ni = ar[j] * si + ai[j] * sr + xim_s[base + j, pl.ds(r0, r), :]
                if store:
                    xre_s[base + j, pl.ds(r0, r), :] = nr
                    xim_s[base + j, pl.ds(r0, r), :] = ni
                new += [nr, ni]
            return tuple(new)

        init = []
        for j in range(S5_LG_PER_BLK):
            init += [st_re[base + j], st_im[base + j]]
        fin = lax.fori_loop(0, n_steps, step, tuple(init))
        for j in range(S5_LG_PER_BLK):
            st_re[base + j] = fin[2 * j]
            st_im[base + j] = fin[2 * j + 1]
        return 0
    lax.fori_loop(0, S5_NBLK, blk_body, 0)


def _s5_local_kernel(x_ref, sc_ref, sh_ref, ng_ref, bbr_ref, bbi_ref, ar_ref, ai_ref,
                     lre_o, lim_o, xre_s, xim_s, st_re, st_im, *, n_steps, r):
    @pl.when(pl.program_id(0) == 0)
    def _():
        st_re[...] = jnp.zeros(st_re.shape, F32)
        st_im[...] = jnp.zeros(st_im.shape, F32)
    u = _norm_mod(x_ref[...], ng_ref[...], sc_ref[...], sh_ref[...])
    _s5_project_in(u, bbr_ref, bbi_ref, xre_s, xim_s)
    _s5_scan(xre_s, xim_s, st_re, st_im, ar_ref, ai_ref, n_steps=n_steps, r=r, store=False)
    lre_o[...] = st_re[...]
    lim_o[...] = st_im[...]


def _cpow(ar, ai, n):
    rr, ri = None, None
    br, bi = ar, ai
    while n:
        if n & 1:
            if rr is None:
                rr, ri = br, bi
            else:
                rr, ri = rr * br - ri * bi, rr * bi + ri * br
        n >>= 1
        if n:
            br, bi = br * br - bi * bi, 2.0 * br * bi
    return rr, ri


def _s5_full_kernel(x_ref, sc_ref, sh_ref, ng_ref, g1_ref, bbr_ref, bbi_ref, ar_ref, ai_ref,
                    cre_ref, cim_ref, d_ref, h0re_ref, h0im_ref, lre_ref, lim_ref, gluw_ref,
                    o_ref, fre_o, fim_o, xre_s, xim_s, st_re, st_im, *, n_steps, r, seg_len):
    @pl.when(pl.program_id(0) == 0)
    def _():
        if seg_len is None:
            st_re[...] = h0re_ref[...]
            st_im[...] = h0im_ref[...]
        else:
            pr, pi = _cpow(ar_ref[...], ai_ref[...], seg_len)
            sr = h0re_ref[:, 0:1, :]
            si = h0im_ref[:, 0:1, :]
            st_re[:, 0:1, :] = sr
            st_im[:, 0:1, :] = si
            for row in range(1, r):
                sr, si = (lre_ref[:, row - 1:row, :] + (pr * sr - pi * si),
                          lim_ref[:, row - 1:row, :] + (pr * si + pi * sr))
                st_re[:, row:row + 1, :] = sr
                st_im[:, row:row + 1, :] = si

    x = x_ref[...]
    u = _norm_mod(x, ng_ref[...], sc_ref[...], sh_ref[...])
    _s5_project_in(u, bbr_ref, bbi_ref, xre_s, xim_s)
    _s5_scan(xre_s, xim_s, st_re, st_im, ar_ref, ai_ref, n_steps=n_steps, r=r, store=True)
    fre_o[...] = st_re[...]
    fim_o[...] = st_im[...]
    ys = []
    for blk in range(S5_NBLK):
        acc = None
        for j in range(S5_LG_PER_BLK):
            lg = blk * S5_LG_PER_BLK + j
            t = (_dot(xre_s[lg].astype(BF16), cre_ref[blk, j * LANES:(j + 1) * LANES, :])
                 - _dot(xim_s[lg].astype(BF16), cim_ref[blk, j * LANES:(j + 1) * LANES, :]))
            acc = t if acc is None else acc + t
        ys.append(acc)
    y = jnp.concatenate(ys, axis=-1) + d_ref[...] * u
    z = _dot(_gelu_tanh(y).astype(BF16), gluw_ref[...])
    dm = x.shape[-1]
    mix = z[:, :dm] * _sigmoid(z[:, dm:])
    o_ref[...] = x + g1_ref[...] * mix


def _s5_weight_specs(ws):
    return [_resident(w.shape) for w in ws]


def _s5_local(x, sc, sh, ng, bb, a3, *, r, rows):
    n, d = x.shape
    n_steps = rows // r
    sds = jax.ShapeDtypeStruct((S5_LG, r, LANES), F32)
    return pl.pallas_call(
        functools.partial(_s5_local_kernel, n_steps=n_steps, r=r),
        grid=(n // rows,),
        in_specs=[pl.BlockSpec((rows, d), lambda i: (i, 0)), _mod_spec(sc, rows, d), _mod_spec(sh, rows, d),
                  _resident((1, d))] + _s5_weight_specs(bb + a3),
        out_specs=[_resident(sds.shape)] * 2,
        out_shape=[sds, sds],
        scratch_shapes=[pltpu.VMEM((S5_LG, rows, LANES), F32)] * 2 + [pltpu.VMEM((S5_LG, r, LANES), F32)] * 2,
        compiler_params=_cp(("arbitrary",)),
        name="s5_local_scan",
    )(x, sc, sh, ng, *bb, *a3)


def _s5_full(x, sc, sh, ng, g1, bb, a3, cre, cim, dskip, h0re, h0im, lre, lim, gluw, *, r, rows, seg_len):
    n, d = x.shape
    n_steps = rows // r
    st = jax.ShapeDtypeStruct((S5_LG, r, LANES), F32)
    return pl.pallas_call(
        functools.partial(_s5_full_kernel, n_steps=n_steps, r=r, seg_len=seg_len),
        grid=(n // rows,),
        in_specs=[pl.BlockSpec((rows, d), lambda i: (i, 0)), _mod_spec(sc, rows, d), _mod_spec(sh, rows, d),
                  _resident((1, d)), _mod_spec(g1, rows, d)]
                 + _s5_weight_specs(bb + a3 + [cre, cim, dskip, h0re, h0im, lre, lim, gluw]),
        out_specs=[pl.BlockSpec((rows, d), lambda i: (i, 0)), _resident(st.shape), _resident(st.shape)],
        out_shape=[jax.ShapeDtypeStruct((n, d), F32), st, st],
        scratch_shapes=[pltpu.VMEM((S5_LG, rows, LANES), F32)] * 2 + [pltpu.VMEM((S5_LG, r, LANES), F32)] * 2,
        compiler_params=_cp(("arbitrary",)),
        name="s5_scan_glu",
    )(x, sc, sh, ng, g1, *bb, *a3, cre, cim, dskip, h0re, h0im, lre, lim, gluw)


def _to_lane_groups(s):
    r = s.shape[0]
    return jnp.transpose(s.reshape(r, S5_LG, LANES), (1, 0, 2))


def _from_lane_groups(s):
    r = s.shape[1]
    return jnp.transpose(s, (1, 0, 2)).reshape(r, S5_W)


def _moe_kernel(x_ref, sc_ref, sh_ref, g2_ref, ng_ref, rw_ref, rb_ref, w1_ref, w3_ref, w2_ref,
                o_ref, h_s, gate_s, acc_s):
    e = pl.program_id(1)
    lane = lax.broadcasted_iota(I32, gate_s.shape, 1)

    @pl.when(e == 0)
    def _():
        h = _norm_mod(x_ref[...], ng_ref[...], sc_ref[...], sh_ref[...])
        h_s[...] = h.astype(BF16)
        logits = _dot(h_s[...], rw_ref[...]) + rb_ref[...]
        logits = jnp.where(lane < N_EXPERTS, logits, -jnp.inf)
        m1 = jnp.max(logits, axis=-1, keepdims=True)
        i1 = jnp.min(jnp.where(logits == m1, lane, LANES), axis=-1, keepdims=True)
        rest = jnp.where(lane == i1, -jnp.inf, logits)
        m2 = jnp.max(rest, axis=-1, keepdims=True)
        i2 = jnp.min(jnp.where(rest == m2, lane, LANES), axis=-1, keepdims=True)
        e2 = jnp.exp(m2 - m1)
        den = 1.0 + e2
        gate_s[...] = jnp.where(lane == i1, 1.0 / den, 0.0) + jnp.where(lane == i2, e2 / den, 0.0)
        acc_s[...] = jnp.zeros(acc_s.shape, F32)

    h = h_s[...]
    ge = jnp.sum(jnp.where(lane == e, gate_s[...], 0.0), axis=-1, keepdims=True)
    y = _dot((_silu(_dot(h, w1_ref[...])) * _dot(h, w3_ref[...])).astype(BF16), w2_ref[...])
    acc_s[...] += ge * y

    @pl.when(e == N_EXPERTS - 1)
    def _():
        o_ref[...] = x_ref[...] + g2_ref[...] * acc_s[...]


def _moe(x, sc, sh, g2, ng, rw, rb, w1, w3, w2, *, tm):
    n, d = x.shape
    ne, _, ff = w1.shape
    mod = lambda m: (pl.BlockSpec((1, d), lambda i, e: (0, 0)) if m.shape[0] == 1
                     else pl.BlockSpec((tm, d), lambda i, e: (i, 0)))
    const = lambda shape: pl.BlockSpec(shape, lambda i, e: (0,) * len(shape), pipeline_mode=pl.Buffered(1))
    return pl.pallas_call(
        _moe_kernel,
        grid=(n // tm, ne),
        in_specs=[pl.BlockSpec((tm, d), lambda i, e: (i, 0)), mod(sc), mod(sh), mod(g2), const((1, d)),
                  const(rw.shape), const(rb.shape),
                  pl.BlockSpec((None, d, ff), lambda i, e: (e, 0, 0)),
                  pl.BlockSpec((None, d, ff), lambda i, e: (e, 0, 0)),
                  pl.BlockSpec((None, ff, d), lambda i, e: (e, 0, 0))],
        out_specs=pl.BlockSpec((tm, d), lambda i, e: (i, 0)),
        out_shape=jax.ShapeDtypeStruct((n, d), F32),
        scratch_shapes=[pltpu.VMEM((tm, d), BF16), pltpu.VMEM((tm, LANES), F32), pltpu.VMEM((tm, d), F32)],
        compiler_params=_cp(("arbitrary", "arbitrary")),
        name="moe_dense",
    )(x, sc, sh, g2, ng, rw, rb, w1, w3, w2)


def _sidx_kernel(pt_ref, qi_ref, w_ref, *rest, n_pages, group, topk, t_new):
    del pt_ref
    page_refs, (ikn_ref, bias_ref, sc_ref, tot_ref) = rest[:group], rest[group:]
    step = pl.program_id(1)
    rows = t_new

    def page_scores(keys_t):
        r = jnp.maximum(_dot(qi_ref[...], keys_t.astype(BF16)), 0.0) * w_ref[...]
        acc = r[0:rows]
        for h in range(1, N_HEADS):
            acc = acc + r[h * rows:(h + 1) * rows]
        return acc

    for g in range(group):
        sc_ref[step * group + g] = page_scores(page_refs[g][...])

    @pl.when(step == n_pages // group - 1)
    def _():
        row = lax.broadcasted_iota(I32, (rows, PAGE), 0)
        col = lax.broadcasted_iota(I32, (rows, PAGE), 1)
        sc_ref[n_pages] = jnp.where(col <= row, page_scores(ikn_ref[...]), -jnp.inf)

        def counter(cmp):
            def count(thr):
                hit = jnp.where(cmp(sc_ref[...], jnp.broadcast_to(thr, (rows, PAGE))[None]), 1.0, 0.0)
                return jnp.sum(jnp.sum(hit, axis=0), axis=-1, keepdims=True)
            return count

        count_ge, count_gt = counter(lambda s, t: s >= t), counter(lambda s, t: s > t)
        thr = _kth_largest(count_ge, count_gt, (rows, 1), topk)
        need = topk - count_gt(thr)
        n_slots = n_pages + 1
        upper = jnp.where(lax.broadcasted_iota(I32, (PAGE, PAGE), 0) < lax.broadcasted_iota(I32, (PAGE, PAGE), 1),
                          1.0, 0.0).astype(BF16)
        thr_b = jnp.broadcast_to(thr, (rows, PAGE))[None]
        s_all = sc_ref[...]
        eq = jnp.where(s_all == thr_b, 1.0, 0.0).reshape(n_slots * rows, PAGE).astype(BF16)
        within = _dot(eq, upper).reshape(n_slots, rows, PAGE)
        tot_ref[...] = _dot(eq, jnp.ones((PAGE, PAGE), BF16)).reshape(n_slots, rows, PAGE)

        def running(j, before):
            total = tot_ref[j]
            tot_ref[j] = before
            return before + total
        lax.fori_loop(0, n_slots, running, jnp.zeros((rows, PAGE), F32))
        rank = within + tot_ref[...]
        keep = jnp.where(s_all > thr_b, 0.0,
                         jnp.where(s_all == thr_b,
                                   jnp.where(rank < jnp.broadcast_to(need, (rows, PAGE))[None], 0.0, NEG), NEG))
        bias_ref[...] = jnp.where(s_all == -jnp.inf, NEG, keep)


def _page_spec(width, group, g):
    return pl.BlockSpec((None, width, PAGE), lambda b, s, pt: (pt[b, s * group + g], 0, 0))


def _sample_indexer(page_table, qi_rows, w_rows, cache_ik, ki_new, *, topk, t_new, group):
    bd, n_pages = page_table.shape
    hq = qi_rows.shape[1]
    group = min(group, n_pages)
    grid_spec = pltpu.PrefetchScalarGridSpec(
        num_scalar_prefetch=1,
        grid=(bd, n_pages // group),
        in_specs=[pl.BlockSpec((None, hq, HEAD_DIM), lambda b, s, pt: (b, 0, 0)),
                  pl.BlockSpec((None, hq, 1), lambda b, s, pt: (b, 0, 0))]
                 + [_page_spec(HEAD_DIM, group, g) for g in range(group)]
                 + [pl.BlockSpec((None, HEAD_DIM, PAGE), lambda b, s, pt: (b, 0, 0))],
        out_specs=pl.BlockSpec((None, n_pages + 1, t_new, PAGE), lambda b, s, pt: (b, 0, 0, 0)),
        scratch_shapes=[pltpu.VMEM((n_pages + 1, t_new, PAGE), F32)] * 2)
    return pl.pallas_call(
        functools.partial(_sidx_kernel, n_pages=n_pages, group=group, topk=topk, t_new=t_new),
        grid_spec=grid_spec,
        out_shape=jax.ShapeDtypeStruct((bd, n_pages + 1, t_new, PAGE), F32),
        compiler_params=_cp(("arbitrary", "arbitrary")),
        name="sample_indexer",
    )(page_table, qi_rows, w_rows, *([cache_ik] * group), ki_new)


def _sattn_kernel(pt_ref, q_ref, *rest, n_pages, group, t_new):
    del pt_ref
    k_refs, v_refs = rest[:group], rest[group:2 * group]
    kn_ref, vn_ref, mb_ref, tabl_ref, tabn_ref, o_ref, m_s, l_s, acc_s = rest[2 * group:]
    step = pl.program_id(1)
    n_steps = n_pages // group

    @pl.when(step == 0)
    def _():
        m_s[...] = jnp.full(m_s.shape, NEG, F32)
        l_s[...] = jnp.zeros(l_s.shape, F32)
        acc_s[...] = jnp.zeros(acc_s.shape, F32)

    def logits(keys_t, mask_row, table):
        s = _dot(q_ref[...], keys_t) + jnp.concatenate([mb_ref[mask_row]] * N_HEADS, axis=0)
        return s if table is None else s + table

    def update(last):
        ks = [r[...].astype(BF16) for r in k_refs] + ([kn_ref[...].astype(BF16)] if last else [])
        vs = [r[...].astype(BF16) for r in v_refs] + ([vn_ref[...].astype(BF16)] if last else [])
        ss = [logits(ks[g], step * group + g, tabl_ref[...] if (last and g == group - 1) else None)
              for g in range(group)]
        if last:
            ss.append(logits(ks[group], n_pages, tabn_ref[...]))
        m_old = m_s[...]
        m_new = m_old
        for s in ss:
            m_new = jnp.maximum(m_new, jnp.max(s, axis=-1, keepdims=True))
        alpha = jnp.exp(m_old - m_new)
        l_new = alpha * l_s[...]
        acc = alpha * acc_s[...]
        for s, v in zip(ss, vs):
            pexp = jnp.exp(s - m_new)
            l_new = l_new + jnp.sum(pexp, axis=-1, keepdims=True)
            acc = acc + _dot_nt(pexp.astype(BF16), v)
        m_s[...] = m_new
        l_s[...] = l_new
        acc_s[...] = acc

    @pl.when(step < n_steps - 1)
    def _():
        update(False)

    @pl.when(step == n_steps - 1)
    def _():
        update(True)
        r = acc_s[...] / l_s[...]
        lane_head = lax.broadcasted_iota(I32, (t_new, A_W), 1) // HEAD_DIM
        out = jnp.zeros((t_new, A_W), F32)
        for h in range(N_HEADS):
            out = out + jnp.where(lane_head == h, r[h * t_new:(h + 1) * t_new], 0.0)
        o_ref[...] = out.astype(o_ref.dtype)


def _sample_attention(page_table, q_bd, cache_k, cache_v, k_new, v_new, mask, tab_last, tab_new, *, t_new, group):
    bd, n_pages = page_table.shape
    hq = q_bd.shape[1]
    group = min(group, n_pages)
    per_b = lambda b, s, pt: (b, 0, 0)
    const2 = lambda b, s, pt: (0, 0)
    pages = [pl.BlockSpec((None, A_W, PAGE), functools.partial(lambda b, s, pt, g: (pt[b, s * group + g], 0, 0), g=g))
             for g in range(group)]
    grid_spec = pltpu.PrefetchScalarGridSpec(
        num_scalar_prefetch=1,
        grid=(bd, n_pages // group),
        in_specs=[pl.BlockSpec((None, hq, A_W), per_b)] + pages + pages
                 + [pl.BlockSpec((None, A_W, PAGE), per_b), pl.BlockSpec((None, A_W, PAGE), per_b),
                    pl.BlockSpec((None, n_pages + 1, t_new, PAGE), lambda b, s, pt: (b, 0, 0, 0)),
                    pl.BlockSpec((hq, PAGE), const2), pl.BlockSpec((hq, PAGE), const2)],
        out_specs=pl.BlockSpec((None, t_new, A_W), per_b),
        scratch_shapes=[pltpu.VMEM((hq, 1), F32), pltpu.VMEM((hq, 1), F32), pltpu.VMEM((hq, A_W), F32)])
    return pl.pallas_call(
        functools.partial(_sattn_kernel, n_pages=n_pages, group=group, t_new=t_new),
        grid_spec=grid_spec,
        out_shape=jax.ShapeDtypeStruct((bd, t_new, A_W), BF16),
        compiler_params=_cp(("arbitrary", "arbitrary")),
        name="sample_attention",
    )(page_table, q_bd, *([cache_k] * group), *([cache_v] * group), k_new, v_new, mask, tab_last, tab_new)


ATTN_TILE = 256
ATTN_CHUNK = 2048
IDX_PAGE_GROUP = 16
ATTN_PAGE_GROUP = 8
S5_ROWS = 256
S5_SEGMENTS = 8


def _even_layer_front(x, sc1, sh1, p, *, tm, qi_dtype):
    return _inproj(x, sc1, sh1, p['ng0'], p['w_in'], p['gq_pad'], p['gk_tiled'], p['gamma'],
                   tm=tm, qi_dtype=qi_dtype)


def _odd_layer(x, m, p, h0re, h0im, *, r, seg_len):
    sh1, sc1, g1, sh2, sc2, g2 = m
    zeros = jnp.zeros((S5_LG, r, LANES), F32)
    if seg_len is None:
        lre, lim = zeros, zeros
    else:
        lre, lim = _s5_local(x, sc1, sh1, p['ng1'], p['bb'], p['a3'], r=r, rows=S5_ROWS)
    x, fre, fim = _s5_full(x, sc1, sh1, p['ng1'], g1, p['bb'], p['a3'], p['cre'], p['cim'], p['dskip'],
                           h0re, h0im, lre, lim, p['gluw'], r=r, rows=S5_ROWS, seg_len=seg_len)
    x = _moe(x, sc2, sh2, g2, p['ng1b'], p['rw'], p['rb'], p['mw1'], p['mw3'], p['mw2'],
             tm=min(512, x.shape[0]))
    return x, _from_lane_groups(fre), _from_lane_groups(fim)


def kernel(x_prompt, x_sample, c_prompt, c_sample, cache_k, cache_v, cache_idx_k, state_hgrn, state_s5_re, state_s5_im, page_table, rel_bias, ada_w, ada_b, norm_g, w_in, qk_norm_g, hgrn_gamma, hgrn_onorm_g, w_out, ffn_w1, ffn_w3, ffn_w2, s5_lambda_re, s5_lambda_im, s5_log_dt, s5_b_re, s5_b_im, s5_c_re, s5_c_im, s5_d, s5_glu_w, moe_router_w, moe_router_b, moe_w1, moe_w3, moe_w2):
    bp, seq, d = x_prompt.shape
    bd, t_new, _ = x_sample.shape
    n_dec = bd * t_new
    n_phys = cache_k.shape[1]
    past = page_table.shape[1] * PAGE
    assert bp == 1 and d == D_MODEL and seq % 512 == 0
    assert n_dec == S5_ROWS and t_new == 8

    g_q, g_k = qk_norm_g[0, 0], qk_norm_g[0, 1]
    ar, ai, bbr_t, bbi_t = _s5_prep(s5_lambda_re[0], s5_lambda_im[0], s5_log_dt[0], s5_b_re[0], s5_b_im[0])
    bb = [_s5_blockdiag_in(bbr_t).astype(BF16), _s5_blockdiag_in(bbi_t).astype(BF16)]
    p = dict(
        ng0=norm_g[0, 0][None], ng0b=norm_g[0, 1][None], ng1=norm_g[1, 0][None], ng1b=norm_g[1, 1][None],
        w_in=_pack_w_in(w_in[0]), gq_pad=jnp.tile(g_q, 2 * N_HEADS)[None], gk_tiled=jnp.tile(g_k, N_HEADS)[None],
        gamma=hgrn_gamma, onorm=jnp.tile(hgrn_onorm_g[0], N_HEADS)[None],
        wa=w_out[0, :A_W].astype(BF16), wb=w_out[0, A_W:].astype(BF16),
        w1=ffn_w1[0].astype(BF16), w3=ffn_w3[0].astype(BF16), w2=ffn_w2[0].astype(BF16),
        bb=bb, a3=[ar.reshape(S5_LG, 1, LANES), ai.reshape(S5_LG, 1, LANES)],
        cre=_s5_blockdiag_out(s5_c_re[0]).astype(BF16), cim=_s5_blockdiag_out(s5_c_im[0]).astype(BF16),
        dskip=s5_d[0][None], gluw=s5_glu_w[0].astype(BF16),
        rw=jnp.pad(moe_router_w[0], ((0, 0), (0, LANES - N_EXPERTS))).astype(BF16),
        rb=jnp.pad(moe_router_b[0], (0, LANES - N_EXPERTS))[None],
        mw1=moe_w1[0].astype(BF16), mw3=moe_w3[0].astype(BF16), mw2=moe_w2[0].astype(BF16),
    )

    c_rows = bp + bd
    c_all = jnp.concatenate([c_prompt, c_sample, jnp.zeros((-c_rows % 8, d), F32)], axis=0)
    mod = _ada(c_all, ada_w.astype(BF16), ada_b)

    def mods(layer, rows, expand):
        m = mod[layer, rows]
        return tuple(expand(m[:, i * d:(i + 1) * d]) for i in range(6))

    xp = x_prompt.reshape(seq, d)
    sh1, sc1, g1, sh2, sc2, g2 = mods(0, slice(0, 1), lambda a: a)
    (q_pad, k_f, k_b, v_f, v_b, qi, kw, gq, gk, gl, gi, gg) = _even_layer_front(
        xp, sc1, sh1, p, tm=512, qi_dtype=BF16)
    ki_p = kw[:, :HEAD_DIM]
    mask = _prompt_indexer(qi.T, kw[:, HEAD_DIM:HEAD_DIM + N_HEADS].T, ki_p.astype(BF16),
                           tq=ATTN_TILE, tk=ATTN_TILE, topk=min(TOPK_MAX, seq // 4))
    a_out = _prompt_attention(q_pad.T, k_b, v_b.T, mask, rel_bias, tile=ATTN_TILE, chunk=ATTN_CHUNK).T
    b_out, st_p = _gla(gq, gk, gi, gl, gg, jnp.zeros((1, N_HEADS, HEAD_DIM, HEAD_DIM), F32), p['onorm'],
                       batch=1, chunk=64)
    xp = _even_tail(xp, a_out, b_out, g1, sc2, sh2, g2, p['ng0b'], p['wa'], p['wb'], p['w1'], p['w3'], p['w2'],
                    tm=512)
    seg_len = seq // S5_SEGMENTS
    to_seg = lambda a: a.reshape(S5_SEGMENTS, seg_len, d).transpose(1, 0, 2).reshape(seq, d)
    zero_state = jnp.zeros((S5_LG, S5_SEGMENTS, LANES), F32)
    xp, fre_p, fim_p = _odd_layer(to_seg(xp), mods(1, slice(0, 1), lambda a: a), p, zero_state, zero_state,
                                  r=S5_SEGMENTS, seg_len=seg_len)
    y_prompt = xp.reshape(seg_len, S5_SEGMENTS, d).transpose(1, 0, 2).reshape(bp, seq, d)

    xs = x_sample.reshape(n_dec, d)
    per_token = lambda a: jnp.repeat(a, t_new, axis=0)
    sh1, sc1, g1, sh2, sc2, g2 = mods(0, slice(bp, bp + bd), per_token)
    (q_pad_s, k_fs, _, v_fs, _, qi_s, kw_s, gq, gk, gl, gi, gg) = _even_layer_front(
        xs, sc1, sh1, p, tm=n_dec, qi_dtype=BF16)
    heads_first = lambda a: jnp.transpose(a, (0, 2, 1, 3)).reshape(bd, N_HEADS * t_new, a.shape[-1])
    qi_rows = heads_first(qi_s.reshape(bd, t_new, N_HEADS, HEAD_DIM))
    w_rows = heads_first(kw_s[:, HEAD_DIM:HEAD_DIM + N_HEADS].reshape(bd, t_new, N_HEADS, 1))
    pad_new = lambda a: jnp.swapaxes(jnp.pad(a.reshape(bd, t_new, -1), ((0, 0), (0, PAGE - t_new), (0, 0))), 1, 2)
    ki_s = kw_s[:, :HEAD_DIM]
    mask_s = _sample_indexer(page_table, qi_rows, w_rows, jnp.swapaxes(cache_idx_k[0], 1, 2), pad_new(ki_s),
                             topk=min(TOPK_MAX, (past + t_new) // 4), t_new=t_new, group=IDX_PAGE_GROUP)
    q4 = q_pad_s.reshape(bd, t_new, N_HEADS, LANES)
    even = (jnp.arange(N_HEADS) % 2 == 0)[None, None, :, None]
    q_nat = jnp.where(even, q4[..., :HEAD_DIM], q4[..., HEAD_DIM:])
    q_bd = jnp.einsum('bthd,hg->bhtgd', q_nat, jnp.eye(N_HEADS, dtype=q_nat.dtype))
    q_bd = q_bd.reshape(bd, N_HEADS * t_new, A_W)
    qpos = np.arange(t_new)[:, None]
    col = np.arange(PAGE)[None, :]
    rows_hq = lambda t: t.reshape(N_HEADS * t_new, PAGE)
    tab_last = rows_hq(_bias_by_distance(rel_bias, PAGE + qpos - col))
    tab_new = rows_hq(_bias_by_distance(rel_bias, qpos - col))
    feature_major = lambda c: jnp.transpose(c[0], (0, 2, 3, 1)).reshape(n_phys, A_W, PAGE)
    a_out_s = _sample_attention(page_table, q_bd, feature_major(cache_k), feature_major(cache_v),
                                pad_new(k_fs), pad_new(v_fs), mask_s, tab_last, tab_new,
                                t_new=t_new, group=ATTN_PAGE_GROUP).reshape(n_dec, A_W)
    gla_chunk = 64
    pad_t = lambda a: jnp.pad(a.reshape(bd, t_new, A_W), ((0, 0), (0, gla_chunk - t_new), (0, 0))).reshape(-1, A_W)
    b_out_s, st_s = _gla(pad_t(gq), pad_t(gk), pad_t(gi), pad_t(gl), pad_t(gg),
                         state_hgrn[0], p['onorm'], batch=bd, chunk=gla_chunk)
    b_out_s = b_out_s.reshape(bd, gla_chunk, A_W)[:, :t_new].reshape(n_dec, A_W)
    xs = _even_tail(xs, a_out_s, b_out_s, g1, sc2, sh2, g2, p['ng0b'], p['wa'], p['wb'],
                    p['w1'], p['w3'], p['w2'], tm=n_dec)
    step_major = lambda a: a.reshape(bd, t_new, d).transpose(1, 0, 2).reshape(n_dec, d)
    xs, fre_s, fim_s = _odd_layer(step_major(xs), mods(1, slice(bp, bp + bd), lambda a: jnp.tile(a, (t_new, 1))), p,
                                  _to_lane_groups(state_s5_re[0].reshape(bd, S5_W)),
                                  _to_lane_groups(state_s5_im[0].reshape(bd, S5_W)), r=bd, seg_len=None)
    y_sample = xs.reshape(t_new, bd, d).transpose(1, 0, 2)

    heads = lambda a, b, t: a.reshape(1, b, t, N_HEADS, HEAD_DIM)
    s5_state = lambda f: f.reshape(1, -1, S5_GROUPS, S5_STATE)
    return (y_prompt, y_sample,
            heads(k_f, bp, seq), heads(v_f, bp, seq), ki_p.reshape(1, bp, seq, HEAD_DIM),
            st_p[None], s5_state(fre_p[S5_SEGMENTS - 1:]), s5_state(fim_p[S5_SEGMENTS - 1:]),
            heads(k_fs, bd, t_new), heads(v_fs, bd, t_new), ki_s.reshape(1, bd, t_new, HEAD_DIM),
            st_s[None], s5_state(fre_s), s5_state(fim_s))
```

```python
import functools
import math

import numpy as np
import jax
import jax.numpy as jnp
from jax import lax
from jax.experimental import pallas as pl
from jax.experimental.pallas import tpu as pltpu

F32 = jnp.float32
BF16 = jnp.bfloat16
I32 = jnp.int32

D_MODEL = 1024
N_HEADS = 8
HEAD_DIM = 64
A_W = N_HEADS * HEAD_DIM
LANES = 128
TOPK_MAX = 256
PAGE = 128
REL_BUCKETS = 32
REL_MAX_DIST = 128
S5_GROUPS = 64
S5_CH = 16
S5_STATE = 64
S5_W = S5_GROUPS * S5_STATE
S5_BLK_GROUPS = 8
S5_NBLK = S5_GROUPS // S5_BLK_GROUPS
D_FF = 2816
N_EXPERTS = 8
MOE_FF = 1408
EPS = 1e-6
NEG = -1e30
INT_MIN = -2 ** 31
VMEM_LIMIT = 56 * 2 ** 20


def _cp(sem, vmem=VMEM_LIMIT):
    return pltpu.CompilerParams(dimension_semantics=sem, vmem_limit_bytes=vmem)


def _resident(shape):
    n = len(shape)
    return pl.BlockSpec(shape, lambda *_: (0,) * n, pipeline_mode=pl.Buffered(1))


def _dot(a, b):
    return jnp.dot(a, b, preferred_element_type=F32)


def _dot_nt(a, b):
    return lax.dot_general(a, b, (((1,), (1,)), ((), ())), preferred_element_type=F32)


def _sigmoid(x):
    return 1.0 / (1.0 + jnp.exp(-x))


def _silu(x):
    return x * _sigmoid(x)


def _gelu_tanh(x):
    return 0.5 * x * (1.0 + jnp.tanh(math.sqrt(2.0 / math.pi) * (x + 0.044715 * (x * x * x))))


def _norm_mod(x, g, sc, sh):
    ms = jnp.mean(x * x, axis=-1, keepdims=True)
    return (x * lax.rsqrt(ms + EPS) * g) * (1.0 + sc) + sh


def _seg_matrix(n, seg, dtype, scale=1.0):
    r = lax.broadcasted_iota(I32, (n, n), 0) // seg
    c = lax.broadcasted_iota(I32, (n, n), 1) // seg
    return jnp.where(r == c, scale, 0.0).astype(dtype)


def _ada_kernel(c_ref, w_ref, b_ref, o_ref):
    c = c_ref[...]
    o_ref[...] = _dot(_silu(c).astype(BF16), w_ref[...]) + b_ref[...]


def _ada(c_all, ada_w, ada_b):
    depth, d, n6 = ada_w.shape
    rows = c_all.shape[0]
    tn = 1536
    return pl.pallas_call(
        _ada_kernel,
        grid=(depth, n6 // tn),
        in_specs=[pl.BlockSpec((rows, d), lambda l, j: (0, 0)),
                  pl.BlockSpec((None, d, tn), lambda l, j: (l, 0, j)),
                  pl.BlockSpec((None, 1, tn), lambda l, j: (l, 0, j))],
        out_specs=pl.BlockSpec((None, rows, tn), lambda l, j: (l, 0, j)),
        out_shape=jax.ShapeDtypeStruct((depth, rows, n6), F32),
        compiler_params=_cp(("arbitrary", "arbitrary")),
        name="ada_mod",
    )(c_all, ada_w, ada_b.reshape(depth, 1, n6))


W_OFF_Q, W_OFF_K, W_OFF_V, W_OFF_QI, W_OFF_KW, W_OFF_B = 0, 1024, 1536, 2048, 2560, 2688


def _pack_w_in(w):
    d = w.shape[0]
    z64 = jnp.zeros((d, N_HEADS, HEAD_DIM), w.dtype)
    q = w[:, 0:512].reshape(d, N_HEADS, HEAD_DIM)
    even = (jnp.arange(N_HEADS) % 2 == 0)[None, :, None]
    q_pad = jnp.concatenate([jnp.where(even, q, z64), jnp.where(even, z64, q)], axis=-1).reshape(d, 1024)
    kw = jnp.concatenate([w[:, 2048:2120], jnp.zeros((d, 56), w.dtype)], axis=-1)
    return jnp.concatenate([q_pad, w[:, 512:2048], kw, w[:, 2120:4168]], axis=-1).astype(BF16)


def _inproj_kernel(x_ref, sc_ref, sh_ref, ng_ref, w_ref, gq_ref, gk_ref, gam_ref,
                   q_ref, kf_ref, kb_ref, vf_ref, vb_ref, qi_ref, kw_ref,
                   gq_o, gk_o, gl_o, gi_o, gg_o):
    h = _norm_mod(x_ref[...], ng_ref[...], sc_ref[...], sh_ref[...]).astype(BF16)
    ones_seg = jnp.full((LANES, LANES), 1.0 / HEAD_DIM, BF16)
    pair_seg = _seg_matrix(LANES, HEAD_DIM, BF16, 1.0 / HEAD_DIM)

    def proj(off, n):
        return _dot(h, w_ref[:, off:off + n])

    pq = proj(W_OFF_Q, 1024)
    for j in range(8):
        blk = pq[:, j * LANES:(j + 1) * LANES]
        ms = _dot((blk * blk).astype(BF16), ones_seg)
        q_ref[:, j * LANES:(j + 1) * LANES] = (
            blk * lax.rsqrt(ms + EPS) * gq_ref[:, j * LANES:(j + 1) * LANES] * 0.125).astype(q_ref.dtype)
    pk = proj(W_OFF_K, 512)
    for j in range(4):
        blk = pk[:, j * LANES:(j + 1) * LANES]
        ms = _dot((blk * blk).astype(BF16), pair_seg)
        kn = blk * lax.rsqrt(ms + EPS) * gk_ref[:, j * LANES:(j + 1) * LANES]
        kf_ref[:, j * LANES:(j + 1) * LANES] = kn
        kb_ref[:, j * LANES:(j + 1) * LANES] = kn.astype(BF16)
    pv = proj(W_OFF_V, 512)
    vf_ref[...] = pv
    vb_ref[...] = pv.astype(BF16)
    qi_ref[...] = (proj(W_OFF_QI, 512) * 0.125).astype(qi_ref.dtype)
    lane = lax.broadcasted_iota(I32, (1, LANES), 1)
    kw_ref[...] = proj(W_OFF_KW, LANES) * jnp.where(lane < HEAD_DIM, 1.0, N_HEADS ** -0.5)
    gq_o[...] = proj(W_OFF_B, 512) * 0.125
    gam = gam_ref[...]
    gmax = jnp.max(gam, axis=0, keepdims=True)
    ge = jnp.exp(gam - gmax)
    lb = ge[0:1, :] / jnp.sum(ge, axis=0, keepdims=True)
    f = lb + (1.0 - lb) * _sigmoid(proj(W_OFF_B + 512, 512))
    gk_o[...] = 1.0 - f
    gl_o[...] = jnp.log(f)
    gi_o[...] = proj(W_OFF_B + 1024, 512)
    gg_o[...] = proj(W_OFF_B + 1536, 512)


def _inproj(x, sc, sh, ng, w_packed, gq_pad, gk_tiled, gamma, *, tm, qi_dtype):
    n, d = x.shape
    row = lambda i: (i, 0)
    mod_spec = (pl.BlockSpec((1, d), lambda i: (0, 0)) if sc.shape[0] == 1
                else pl.BlockSpec((tm, d), row))
    outs = [((n, 1024), BF16), ((n, 512), F32), ((n, 512), BF16), ((n, 512), F32), ((n, 512), BF16),
            ((n, 512), qi_dtype), ((n, LANES), F32)] + [((n, 512), F32)] * 5
    return pl.pallas_call(
        _inproj_kernel,
        grid=(n // tm,),
        in_specs=[pl.BlockSpec((tm, d), row), mod_spec, mod_spec,
                  _resident((1, d)), _resident(w_packed.shape), _resident((1, 1024)),
                  _resident((1, 512)), _resident(gamma.shape)],
        out_specs=[pl.BlockSpec((tm, s[1]), row) for s, _ in outs],
        out_shape=[jax.ShapeDtypeStruct(s, dt) for s, dt in outs],
        compiler_params=_cp(("arbitrary",)),
        name="in_proj",
    )(x, sc, sh, ng, w_packed, gq_pad, gk_tiled, gamma)


def _key_to_f32(key):
    neg = key < 0
    mag = jnp.where(neg, -key, key)
    return lax.bitcast_convert_type(jnp.where(neg, mag | jnp.int32(INT_MIN), mag), F32)


def _f32_to_key(x):
    bits = lax.bitcast_convert_type(x, I32)
    return jnp.where(bits < 0, -(bits & jnp.int32(0x7FFFFFFF)), bits)


def _kth_largest(count_ge, count_gt, shape, topk, bounds=None):
    if bounds is None:
        bounds = (jnp.full(shape, -jnp.inf, F32), jnp.full(shape, jnp.inf, F32))

    def cond(state):
        lo, hi, done, _ = state
        return jnp.max(jnp.where(jnp.logical_and(lo != hi, done == 0), 1, 0)) == 1

    def body(state):
        lo, hi, done, thr = state
        mid = (lo | hi) - ((lo ^ hi) >> 1)
        mid_f = _key_to_f32(mid)
        cnt = count_ge(mid_f)
        enough = cnt >= topk
        hit = jnp.logical_and(cnt == topk, done == 0)
        return (jnp.where(enough, mid, lo), jnp.where(enough, hi, mid - 1),
                jnp.where(hit, 1, done), jnp.where(hit, mid_f, thr))

    lo, hi = _f32_to_key(bounds[0]), _f32_to_key(bounds[1])
    zero = jnp.zeros(shape, F32)
    n_pos, n_nonneg = count_gt(zero), count_ge(zero)
    done = jnp.where(jnp.logical_and(n_pos < topk, n_nonneg >= topk), 1, 0)
    lo = jnp.where(n_pos >= topk, jnp.maximum(lo, 1), lo)
    hi = jnp.where(n_nonneg < topk, jnp.minimum(hi, -1), hi)
    lo, _, done, thr = lax.while_loop(cond, body, (lo, hi, done, jnp.zeros(shape, F32)))
    return jnp.where(done == 1, thr, _key_to_f32(lo))


def _pidx_kernel(qit_ref, wt_ref, ki_ref, bias_ref, sc_ref, cm_ref, *, tq, tk, topk, seq):
    assert topk <= tk
    q0 = pl.program_id(0) * tq
    n_kt = (q0 + tq + tk - 1) // tk
    key = lax.broadcasted_iota(I32, (tk, tq), 0)
    qry = lax.broadcasted_iota(I32, (tk, tq), 1) + q0

    def tile_off(kt):
        return pl.multiple_of(kt * tk, tk)

    def scores(kt, _):
        off = tile_off(kt)
        ks = ki_ref[pl.ds(off, tk), :]
        acc = jnp.zeros((tk, tq), F32)
        for h in range(N_HEADS):
            x = _dot(ks, qit_ref[h * HEAD_DIM:(h + 1) * HEAD_DIM, :])
            acc = acc + wt_ref[h:h + 1, :] * jnp.maximum(x, 0.0)
        acc = jnp.where(key + off <= qry, acc, -jnp.inf)
        sc_ref[pl.ds(off, tk), :] = acc
        cm_ref[...] = jnp.maximum(cm_ref[...], acc)
        return 0
    cm_ref[...] = jnp.full((tk, tq), -jnp.inf, F32)
    n_pairs = (n_kt + 1) // 2
    lax.fori_loop(0, 2 * n_pairs, scores, 0)
    bounds = (jnp.min(cm_ref[...], axis=0, keepdims=True), jnp.max(cm_ref[...], axis=0, keepdims=True))

    def counter(cmp):
        def count(thr):
            def body(kp, c):
                off = pl.multiple_of(kp * (2 * tk), 2 * tk)
                hit = jnp.where(cmp(sc_ref[pl.ds(off, 2 * tk), :], thr), 1.0, 0.0)
                return c + jnp.sum(hit.reshape(2 * tk // 64, 8, 8, tq), axis=0)
            c = lax.fori_loop(0, n_pairs, body, jnp.zeros((8, 8, tq), F32))
            return jnp.sum(jnp.sum(c, axis=0), axis=0, keepdims=True)
        return count

    count_ge, count_gt = counter(lambda s, t: s >= t), counter(lambda s, t: s > t)
    thr = _kth_largest(count_ge, count_gt, (1, tq), topk, bounds)
    need = topk - count_gt(thr)
    lower = jnp.where(lax.broadcasted_iota(I32, (tk, tk), 0) > lax.broadcasted_iota(I32, (tk, tk), 1),
                      1.0, 0.0).astype(BF16)

    def select(kt, ties_before):
        off = tile_off(kt)
        s = sc_ref[pl.ds(off, tk), :]
        eq = jnp.where(s == thr, 1.0, 0.0)
        rank = _dot(lower, eq.astype(BF16)) + ties_before
        keep = jnp.where(s > thr, 0.0, jnp.where(s == thr, jnp.where(rank < need, 0.0, NEG), NEG))
        bias_ref[pl.ds(off, tk), :] = jnp.where(key + off <= qry, keep, NEG).astype(BF16)
        return ties_before + jnp.sum(eq, axis=0, keepdims=True)
    lax.fori_loop(0, n_kt, select, jnp.zeros((1, tq), F32))

    def fill(kt, _):
        bias_ref[pl.ds(tile_off(kt), tk), :] = jnp.full((tk, tq), NEG, BF16)
        return 0
    lax.fori_loop(n_kt, seq // tk, fill, 0)


def _prompt_indexer(qi_t, w_t, ki, *, tq, tk, topk):
    seq = ki.shape[0]
    return pl.pallas_call(
        functools.partial(_pidx_kernel, tq=tq, tk=tk, topk=topk, seq=seq),
        grid=(seq // tq,),
        in_specs=[pl.BlockSpec((A_W, tq), lambda i: (0, i)),
                  pl.BlockSpec((N_HEADS, tq), lambda i: (0, i)),
                  _resident(ki.shape)],
        out_specs=pl.BlockSpec((None, seq, tq), lambda i: (i, 0, 0)),
        out_shape=jax.ShapeDtypeStruct((seq // tq, seq, tq), BF16),
        scratch_shapes=[pltpu.VMEM((seq, tq), F32), pltpu.VMEM((tk, tq), F32)],
        compiler_params=_cp(("arbitrary",)),
        name="prompt_indexer",
    )(qi_t, w_t, ki)


def _t5_bucket_table():
    n = np.arange(REL_MAX_DIST, dtype=np.int64)
    max_exact = REL_BUCKETS // 2
    nf = np.maximum(n, 1).astype(np.float32)
    large = max_exact + (np.log(nf / np.float32(max_exact)) / np.float32(math.log(REL_MAX_DIST / max_exact))
                         * np.float32(REL_BUCKETS - max_exact)).astype(np.int32)
    large = np.minimum(large, REL_BUCKETS - 1)
    return np.where(n < max_exact, n, large).astype(np.int32)


def _bias_by_distance(rel_bias, dist):
    table = _t5_bucket_table()
    bucket = np.where(dist >= REL_MAX_DIST, REL_BUCKETS - 1, table[np.clip(dist, 0, REL_MAX_DIST - 1)])
    b = jnp.moveaxis(rel_bias[bucket], -1, 0)
    far = rel_bias[REL_BUCKETS - 1].reshape((N_HEADS,) + (1,) * dist.ndim)
    return (b - far).astype(F32)


V_ROWS = HEAD_DIM + 16


def _pattn_kernel(qt_ref, k_ref, vt_ref, mask_ref, tab_ref, ot_ref, m_ref, acc_ref, *slots, tile, chunk):
    slot_a, slot_b = slots[:4], slots[4:]
    qb = pl.program_id(0)
    first = pl.program_id(1) * (chunk // tile)
    n_tiles = chunk // tile

    @pl.when(pl.program_id(1) == 0)
    def _():
        m_ref[...] = jnp.full(m_ref.shape, NEG, F32)
        acc_ref[...] = jnp.zeros(acc_ref.shape, F32)

    def scores(j, slot, near):
        s_ref, mx_ref, _, _ = slot
        off = pl.multiple_of(j * tile, tile)
        mb = mask_ref[pl.ds(off, tile), :].astype(F32)
        for h in range(N_HEADS):
            pr = h // 2
            s = _dot(k_ref[pl.ds(off, tile), pr * LANES:(pr + 1) * LANES], qt_ref[h * LANES:(h + 1) * LANES, :])
            s = s + mb
            if near is not None:
                s = s + tab_ref[near, h]
            s_ref[h] = s
            mx_ref[h] = jnp.max(s, axis=0, keepdims=True)

    def accumulate(j, slot):
        s_ref, mx_ref, p_ref, al_ref = slot
        off = pl.multiple_of(j * tile, tile)
        for h in range(N_HEADS):
            m_old = m_ref[h]
            m_new = jnp.maximum(m_old, mx_ref[h])
            alpha = jnp.exp(m_old - m_new)
            p_ref[h] = jnp.exp(s_ref[h] - m_new).astype(BF16)
            al_ref[h] = alpha
            m_ref[h] = m_new
        for h in range(N_HEADS):
            pv = _dot(vt_ref[h * V_ROWS:(h + 1) * V_ROWS, pl.ds(off, tile)], p_ref[h])
            acc_ref[h] = al_ref[h] * acc_ref[h] + pv

    all_far = first + n_tiles <= qb - 1

    @pl.when(all_far)
    def _():
        scores(0, slot_a, None)

        def pair(i, _):
            scores(2 * i + 1, slot_b, None)
            accumulate(2 * i, slot_a)
            scores(jnp.minimum(2 * i + 2, n_tiles - 1), slot_a, None)
            accumulate(2 * i + 1, slot_b)
            return 0
        lax.fori_loop(0, n_tiles // 2, pair, 0)

    @pl.when(jnp.logical_not(all_far))
    def _():
        def far_body(j, _):
            scores(j, slot_a, None)
            accumulate(j, slot_a)
            return 0
        lax.fori_loop(0, jnp.clip(qb - 1 - first, 0, n_tiles), far_body, 0)
        for near, j in ((1, qb - 1 - first), (0, qb - first)):
            @pl.when(jnp.logical_and(j >= 0, j < n_tiles))
            def _():
                scores(j, slot_a, near)
                accumulate(j, slot_a)

    @pl.when(pl.program_id(1) == pl.num_programs(1) - 1)
    def _():
        for h in range(N_HEADS):
            acc = acc_ref[h]
            ot_ref[h * HEAD_DIM:(h + 1) * HEAD_DIM, :] = (
                acc[:HEAD_DIM] / acc[HEAD_DIM:HEAD_DIM + 1]).astype(ot_ref.dtype)


def _near_bias_tables(rel_bias, tile):
    period = 2 * tile
    j = np.arange(period)
    offset = np.where(j < tile, j, j - period)
    w = _bias_by_distance(rel_bias, np.stack([offset, tile + offset]))
    skew = jnp.tile(w, (1, 1, tile))[..., :tile * (period - 1)].reshape(N_HEADS, 2, tile, period - 1)
    return jnp.moveaxis(skew[..., :tile], 0, 1)


def _prompt_attention(q_t, k_bf, v_t, mask, rel_bias, *, tile, chunk):
    seq = k_bf.shape[0]
    chunk = min(chunk, seq)
    tab = _near_bias_tables(rel_bias, tile)
    v_ext = jnp.concatenate([v_t.reshape(N_HEADS, HEAD_DIM, seq),
                             jnp.ones((N_HEADS, V_ROWS - HEAD_DIM, seq), v_t.dtype)], axis=1)
    v_ext = v_ext.reshape(N_HEADS * V_ROWS, seq)
    last_chunk = lambda i: (i * tile + tile - 1) // chunk
    return pl.pallas_call(
        functools.partial(_pattn_kernel, tile=tile, chunk=chunk),
        grid=(seq // tile, seq // chunk),
        in_specs=[pl.BlockSpec((N_HEADS * LANES, tile), lambda i, c: (0, i)),
                  pl.BlockSpec((chunk, A_W), lambda i, c: (jnp.minimum(c, last_chunk(i)), 0)),
                  pl.BlockSpec((N_HEADS * V_ROWS, chunk), lambda i, c: (0, jnp.minimum(c, last_chunk(i)))),
                  pl.BlockSpec((None, chunk, tile), lambda i, c: (i, jnp.minimum(c, last_chunk(i)), 0)),
                  pl.BlockSpec(tab.shape, lambda i, c: (0, 0, 0, 0), pipeline_mode=pl.Buffered(1))],
        out_specs=pl.BlockSpec((A_W, tile), lambda i, c: (0, i)),
        out_shape=jax.ShapeDtypeStruct((A_W, seq), BF16),
        scratch_shapes=[pltpu.VMEM((N_HEADS, 1, tile), F32), pltpu.VMEM((N_HEADS, V_ROWS, tile), F32)]
                       + [pltpu.VMEM((N_HEADS, tile, tile), F32), pltpu.VMEM((N_HEADS, 1, tile), F32),
                          pltpu.VMEM((N_HEADS, tile, tile), BF16), pltpu.VMEM((N_HEADS, 1, tile), F32)] * 2,
        compiler_params=_cp(("arbitrary", "arbitrary")),
        name="prompt_attention",
    )(q_t, k_bf, v_ext, mask, tab)


def _cumsum_rows(x):
    c = x.shape[0]
    tri = jnp.where(lax.broadcasted_iota(I32, (c, c), 0) >= lax.broadcasted_iota(I32, (c, c), 1),
                    1.0, 0.0).astype(BF16)
    hi = x.astype(BF16)
    r1 = x - hi.astype(F32)
    mid = r1.astype(BF16)
    lo = (r1 - mid.astype(F32)).astype(BF16)
    return _dot(tri, hi) + (_dot(tri, mid) + _dot(tri, lo))


def _gla_kernel(q_ref, k_ref, v_ref, g_ref, gate_ref, s0_ref, on_ref, diag_ref, o_ref, sfin_ref,
                st_ref, b_ref, oi_ref, *, chunk):
    @pl.when(pl.program_id(1) == 0)
    def _():
        st_ref[...] = jnp.zeros(st_ref.shape, F32)
        for h in range(N_HEADS):
            st_ref[h * HEAD_DIM:(h + 1) * HEAD_DIM, h * HEAD_DIM:(h + 1) * HEAD_DIM] = s0_ref[h]

    w = q_ref.shape[-1]
    b = _cumsum_rows(g_ref[...])
    b_ref[...] = b
    k = k_ref[...]
    v = v_ref[...]
    seg = _seg_matrix(w, HEAD_DIM, BF16)
    st = st_ref[...]
    o_inter = _dot_nt((q_ref[...] * jnp.exp(b)).astype(BF16), st.astype(BF16))
    for grp in range(chunk // 8):
        n = 8 * (grp + 1)
        rows = lax.broadcasted_iota(I32, (n, w), 0)
        out_rows = []
        for t in range(8 * grp, n):
            dec = jnp.where(rows <= t, jnp.exp(b_ref[t:t + 1, :] - b_ref[0:n, :]), 0.0)
            prod = (q_ref[t:t + 1, :] * dec * k_ref[0:n, :]).astype(BF16)
            out_rows.append(jnp.sum(_dot(prod, seg) * v_ref[0:n, :], axis=0, keepdims=True))
        oi_ref[8 * grp:n, :] = jnp.concatenate(out_rows, axis=0)

    o = o_inter + oi_ref[...]
    ms = _dot((o * o).astype(BF16), seg) * (1.0 / HEAD_DIM)
    o_ref[...] = (o * lax.rsqrt(ms + EPS) * on_ref[...] * _silu(gate_ref[...])).astype(o_ref.dtype)

    b_last = b[chunk - 1:chunk, :]
    kd = (k * jnp.exp(b_last - b)).astype(BF16)
    upd = _dot(v.T.astype(BF16), kd)
    st_new = st * jnp.exp(b_last) + upd * diag_ref[...]
    st_ref[...] = st_new

    @pl.when(pl.program_id(1) == pl.num_programs(1) - 1)
    def _():
        for h in range(N_HEADS):
            sfin_ref[h] = st_new[h * HEAD_DIM:(h + 1) * HEAD_DIM, h * HEAD_DIM:(h + 1) * HEAD_DIM]


def _gla(gq, gk, gv, glog, gate, s0, onorm_tiled, *, batch, chunk):
    n, w = gq.shape
    nc = n // batch // chunk
    row = lambda b, c: (b * nc + c, 0)
    tile = pl.BlockSpec((chunk, w), row)
    state = pl.BlockSpec((None, N_HEADS, HEAD_DIM, HEAD_DIM), lambda b, c: (b, 0, 0, 0))
    same_head = np.equal.outer(np.arange(w) // HEAD_DIM, np.arange(w) // HEAD_DIM).astype(np.float32)
    const = lambda shape: pl.BlockSpec(shape, lambda b, c: (0, 0), pipeline_mode=pl.Buffered(1))
    o, s_fin = pl.pallas_call(
        functools.partial(_gla_kernel, chunk=chunk),
        grid=(batch, nc),
        in_specs=[tile] * 5 + [state, const((1, w)), const((w, w))],
        out_specs=[tile, state],
        out_shape=[jax.ShapeDtypeStruct((n, w), BF16),
                   jax.ShapeDtypeStruct((batch, N_HEADS, HEAD_DIM, HEAD_DIM), F32)],
        scratch_shapes=[pltpu.VMEM((w, w), F32), pltpu.VMEM((chunk, w), F32), pltpu.VMEM((chunk, w), F32)],
        compiler_params=_cp(("arbitrary", "arbitrary")),
        name="hgrn2_gla",
    )(gq, gk, gv, glog, gate, jnp.swapaxes(s0, 2, 3), onorm_tiled, jnp.asarray(same_head, F32))
    return o, jnp.swapaxes(s_fin, 2, 3)


def _even_tail_kernel(x_ref, a_ref, b_ref, g1_ref, sc_ref, sh_ref, g2_ref, ng_ref,
                      wa_ref, wb_ref, w1_ref, w3_ref, w2_ref, o_ref, *, ff_split):
    mix = _dot(a_ref[...], wa_ref[...]) + _dot(b_ref[...], wb_ref[...])
    x1 = x_ref[...] + g1_ref[...] * mix
    h = _norm_mod(x1, ng_ref[...], sc_ref[...], sh_ref[...]).astype(BF16)
    ff = jnp.zeros(x1.shape, F32)
    step = w1_ref.shape[1] // ff_split
    for j in range(ff_split):
        a = _dot(h, w1_ref[:, j * step:(j + 1) * step])
        g = _dot(h, w3_ref[:, j * step:(j + 1) * step])
        ff = ff + _dot((_silu(a) * g).astype(BF16), w2_ref[j * step:(j + 1) * step, :])
    o_ref[...] = x1 + g2_ref[...] * ff


def _mod_spec(m, tm, d):
    return (pl.BlockSpec((1, d), lambda i: (0, 0)) if m.shape[0] == 1
            else pl.BlockSpec((tm, d), lambda i: (i, 0)))


def _even_tail(x, a_out, b_out, g1, sc2, sh2, g2, ng2, wa, wb, w1, w3, w2, *, tm):
    n, d = x.shape
    row = lambda i: (i, 0)
    return pl.pallas_call(
        functools.partial(_even_tail_kernel, ff_split=2),
        grid=(n // tm,),
        in_specs=[pl.BlockSpec((tm, d), row), pl.BlockSpec((tm, A_W), row), pl.BlockSpec((tm, A_W), row),
                  _mod_spec(g1, tm, d), _mod_spec(sc2, tm, d), _mod_spec(sh2, tm, d), _mod_spec(g2, tm, d),
                  _resident((1, d)), _resident(wa.shape), _resident(wb.shape),
                  _resident(w1.shape), _resident(w3.shape), _resident(w2.shape)],
        out_specs=pl.BlockSpec((tm, d), row),
        out_shape=jax.ShapeDtypeStruct((n, d), F32),
        compiler_params=_cp(("arbitrary",)),
        name="even_tail",
    )(x, a_out, b_out, g1, sc2, sh2, g2, ng2, wa, wb, w1, w3, w2)


S5_LG = S5_W // LANES
S5_LG_PER_BLK = S5_LG // S5_NBLK


def _s5_prep_kernel(lr_ref, li_ref, ldt_ref, br_ref, bi_ref, ar_o, ai_o, bbr_o, bbi_o):
    lr = jnp.minimum(lr_ref[...], -1e-4)
    li = li_ref[...]
    dt = jnp.exp(ldt_ref[...])
    mag = jnp.exp(lr * dt)
    a_re = mag * jnp.cos(li * dt)
    a_im = mag * jnp.sin(li * dt)
    den = lr * lr + li * li
    nr = a_re - 1.0
    coef_re = (nr * lr + a_im * li) / den
    coef_im = (a_im * lr - nr * li) / den
    ar_o[...] = a_re
    ai_o[...] = a_im
    br = br_ref[...]
    bi = bi_ref[...]
    bbr_o[...] = coef_re * br - coef_im * bi
    bbi_o[...] = coef_re * bi + coef_im * br


def _s5_prep(lam_re, lam_im, log_dt, b_re, b_im):
    g, p = lam_re.shape
    v3 = lambda a: a.reshape(g, 1, p)
    bt = lambda b: jnp.transpose(b, (0, 2, 1))
    sds = jax.ShapeDtypeStruct
    return pl.pallas_call(
        _s5_prep_kernel,
        out_shape=[sds((g, 1, p), F32), sds((g, 1, p), F32), sds((g, S5_CH, p), F32), sds((g, S5_CH, p), F32)],
        name="s5_prep",
    )(v3(lam_re), v3(lam_im), log_dt.reshape(g, 1, 1), bt(b_re), bt(b_im))


def _s5_blockdiag_in(bb_t):
    x = bb_t.reshape(S5_NBLK, S5_BLK_GROUPS, S5_CH, S5_STATE)
    bd = jnp.einsum('bgcp,gh->bgchp', x, jnp.eye(S5_BLK_GROUPS, dtype=x.dtype))
    return bd.reshape(S5_NBLK, S5_BLK_GROUPS * S5_CH, S5_BLK_GROUPS * S5_STATE)


def _s5_blockdiag_out(c):
    x = c.reshape(S5_NBLK, S5_BLK_GROUPS, S5_CH, S5_STATE)
    bd = jnp.einsum('bgcp,gh->bgphc', x, jnp.eye(S5_BLK_GROUPS, dtype=x.dtype))
    return bd.reshape(S5_NBLK, S5_BLK_GROUPS * S5_STATE, S5_BLK_GROUPS * S5_CH)


def _s5_project_in(u, bbr_ref, bbi_ref, xre_s, xim_s):
    for blk in range(S5_NBLK):
        ub = u[:, blk * LANES:(blk + 1) * LANES].astype(BF16)
        for w_ref, dst in ((bbr_ref, xre_s), (bbi_ref, xim_s)):
            r = _dot(ub, w_ref[blk])
            for j in range(S5_LG_PER_BLK):
                dst[blk * S5_LG_PER_BLK + j] = r[:, j * LANES:(j + 1) * LANES]


def _s5_scan(xre_s, xim_s, st_re, st_im, ar_ref, ai_ref, *, n_steps, r, store):
    def blk_body(blk, _):
        base = blk * S5_LG_PER_BLK
        ar = [jnp.broadcast_to(ar_ref[base + j], (r, LANES)) for j in range(S5_LG_PER_BLK)]
        ai = [jnp.broadcast_to(ai_ref[base + j], (r, LANES)) for j in range(S5_LG_PER_BLK)]

        def step(s, carry):
            r0 = pl.multiple_of(s * r, r)
            new = []
            for j in range(S5_LG_PER_BLK):
                sr, si = carry[2 * j], carry[2 * j + 1]
                nr = ar[j] * sr - ai[j] * si + xre_s[base + j, pl.ds(r0, r), :]
                ni = ar[j] * si + ai[j] * sr + xim_s[base + j, pl.ds(r0, r), :]
                if store:
                    xre_s[base + j, pl.ds(r0, r), :] = nr
                    xim_s[base + j, pl.ds(r0, r), :] = ni
                new += [nr, ni]
            return tuple(new)

        init = []
        for j in range(S5_LG_PER_BLK):
            init += [st_re[base + j], st_im[base + j]]
        fin = lax.fori_loop(0, n_steps, step, tuple(init))
        for j in range(S5_LG_PER_BLK):
            st_re[base + j] = fin[2 * j]
            st_im[base + j] = fin[2 * j + 1]
        return 0
    lax.fori_loop(0, S5_NBLK, blk_body, 0)


def _s5_local_kernel(x_ref, sc_ref, sh_ref, ng_ref, bbr_ref, bbi_ref, ar_ref, ai_ref,
                     lre_o, lim_o, xre_s, xim_s, st_re, st_im, *, n_steps, r):
    @pl.when(pl.program_id(0) == 0)
    def _():
        st_re[...] = jnp.zeros(st_re.shape, F32)
        st_im[...] = jnp.zeros(st_im.shape, F32)
    u = _norm_mod(x_ref[...], ng_ref[...], sc_ref[...], sh_ref[...])
    _s5_project_in(u, bbr_ref, bbi_ref, xre_s, xim_s)
    _s5_scan(xre_s, xim_s, st_re, st_im, ar_ref, ai_ref, n_steps=n_steps, r=r, store=False)
    lre_o[...] = st_re[...]
    lim_o[...] = st_im[...]


def _cpow(ar, ai, n):
    rr, ri = None, None
    br, bi = ar, ai
    while n:
        if n & 1:
            if rr is None:
                rr, ri = br, bi
            else:
                rr, ri = rr * br - ri * bi, rr * bi + ri * br
        n >>= 1
        if n:
            br, bi = br * br - bi * bi, 2.0 * br * bi
    return rr, ri


def _s5_full_kernel(x_ref, sc_ref, sh_ref, ng_ref, g1_ref, bbr_ref, bbi_ref, ar_ref, ai_ref,
                    cre_ref, cim_ref, d_ref, h0re_ref, h0im_ref, lre_ref, lim_ref, gluw_ref,
                    o_ref, fre_o, fim_o, xre_s, xim_s, st_re, st_im, *, n_steps, r, seg_len):
    @pl.when(pl.program_id(0) == 0)
    def _():
        if seg_len is None:
            st_re[...] = h0re_ref[...]
            st_im[...] = h0im_ref[...]
        else:
            pr, pi = _cpow(ar_ref[...], ai_ref[...], seg_len)
            sr = h0re_ref[:, 0:1, :]
            si = h0im_ref[:, 0:1, :]
            st_re[:, 0:1, :] = sr
            st_im[:, 0:1, :] = si
            for row in range(1, r):
                sr, si = (lre_ref[:, row - 1:row, :] + (pr * sr - pi * si),
                          lim_ref[:, row - 1:row, :] + (pr * si + pi * sr))
                st_re[:, row:row + 1, :] = sr
                st_im[:, row:row + 1, :] = si

    x = x_ref[...]
    u = _norm_mod(x, ng_ref[...], sc_ref[...], sh_ref[...])
    _s5_project_in(u, bbr_ref, bbi_ref, xre_s, xim_s)
    _s5_scan(xre_s, xim_s, st_re, st_im, ar_ref, ai_ref, n_steps=n_steps, r=r, store=True)
    fre_o[...] = st_re[...]
    fim_o[...] = st_im[...]
    ys = []
    for blk in range(S5_NBLK):
        acc = None
        for j in range(S5_LG_PER_BLK):
            lg = blk * S5_LG_PER_BLK + j
            t = (_dot(xre_s[lg].astype(BF16), cre_ref[blk, j * LANES:(j + 1) * LANES, :])
                 - _dot(xim_s[lg].astype(BF16), cim_ref[blk, j * LANES:(j + 1) * LANES, :]))
            acc = t if acc is None else acc + t
        ys.append(acc)
    y = jnp.concatenate(ys, axis=-1) + d_ref[...] * u
    z = _dot(_gelu_tanh(y).astype(BF16), gluw_ref[...])
    dm = x.shape[-1]
    mix = z[:, :dm] * _sigmoid(z[:, dm:])
    o_ref[...] = x + g1_ref[...] * mix


def _s5_weight_specs(ws):
    return [_resident(w.shape) for w in ws]


def _s5_local(x, sc, sh, ng, bb, a3, *, r, rows):
    n, d = x.shape
    n_steps = rows // r
    sds = jax.ShapeDtypeStruct((S5_LG, r, LANES), F32)
    return pl.pallas_call(
        functools.partial(_s5_local_kernel, n_steps=n_steps, r=r),
        grid=(n // rows,),
        in_specs=[pl.BlockSpec((rows, d), lambda i: (i, 0)), _mod_spec(sc, rows, d), _mod_spec(sh, rows, d),
                  _resident((1, d))] + _s5_weight_specs(bb + a3),
        out_specs=[_resident(sds.shape)] * 2,
        out_shape=[sds, sds],
        scratch_shapes=[pltpu.VMEM((S5_LG, rows, LANES), F32)] * 2 + [pltpu.VMEM((S5_LG, r, LANES), F32)] * 2,
        compiler_params=_cp(("arbitrary",)),
        name="s5_local_scan",
    )(x, sc, sh, ng, *bb, *a3)


def _s5_full(x, sc, sh, ng, g1, bb, a3, cre, cim, dskip, h0re, h0im, lre, lim, gluw, *, r, rows, seg_len):
    n, d = x.shape
    n_steps = rows // r
    st = jax.ShapeDtypeStruct((S5_LG, r, LANES), F32)
    return pl.pallas_call(
        functools.partial(_s5_full_kernel, n_steps=n_steps, r=r, seg_len=seg_len),
        grid=(n // rows,),
        in_specs=[pl.BlockSpec((rows, d), lambda i: (i, 0)), _mod_spec(sc, rows, d), _mod_spec(sh, rows, d),
                  _resident((1, d)), _mod_spec(g1, rows, d)]
                 + _s5_weight_specs(bb + a3 + [cre, cim, dskip, h0re, h0im, lre, lim, gluw]),
        out_specs=[pl.BlockSpec((rows, d), lambda i: (i, 0)), _resident(st.shape), _resident(st.shape)],
        out_shape=[jax.ShapeDtypeStruct((n, d), F32), st, st],
        scratch_shapes=[pltpu.VMEM((S5_LG, rows, LANES), F32)] * 2 + [pltpu.VMEM((S5_LG, r, LANES), F32)] * 2,
        compiler_params=_cp(("arbitrary",)),
        name="s5_scan_glu",
    )(x, sc, sh, ng, g1, *bb, *a3, cre, cim, dskip, h0re, h0im, lre, lim, gluw)


def _to_lane_groups(s):
    r = s.shape[0]
    return jnp.transpose(s.reshape(r, S5_LG, LANES), (1, 0, 2))


def _from_lane_groups(s):
    r = s.shape[1]
    return jnp.transpose(s, (1, 0, 2)).reshape(r, S5_W)


MOE_CAP = 128


def _moe_kernel(x_ref, sc_ref, sh_ref, g2_ref, ng_ref, rw_ref, rb_ref, w1_ref, w3_ref, w2_ref,
                o_ref, h_s, gate_s, rank_s, acc_s):
    e = pl.program_id(1)
    tm = gate_s.shape[0]
    lane = lax.broadcasted_iota(I32, gate_s.shape, 1)

    @pl.when(e == 0)
    def _():
        h = _norm_mod(x_ref[...], ng_ref[...], sc_ref[...], sh_ref[...])
        h_s[...] = h.astype(BF16)
        logits = _dot(h_s[...], rw_ref[...]) + rb_ref[...]
        logits = jnp.where(lane < N_EXPERTS, logits, -jnp.inf)
        m1 = jnp.max(logits, axis=-1, keepdims=True)
        i1 = jnp.min(jnp.where(logits == m1, lane, LANES), axis=-1, keepdims=True)
        rest = jnp.where(lane == i1, -jnp.inf, logits)
        m2 = jnp.max(rest, axis=-1, keepdims=True)
        i2 = jnp.min(jnp.where(rest == m2, lane, LANES), axis=-1, keepdims=True)
        e2 = jnp.exp(m2 - m1)
        den = 1.0 + e2
        gates = jnp.where(lane == i1, 1.0 / den, 0.0) + jnp.where(lane == i2, e2 / den, 0.0)
        gate_s[...] = gates
        earlier = jnp.where(lax.broadcasted_iota(I32, (tm, tm), 0) > lax.broadcasted_iota(I32, (tm, tm), 1),
                            1.0, 0.0).astype(BF16)
        rank_s[...] = _dot(earlier, jnp.where(gates > 0.0, 1.0, 0.0).astype(BF16))
        acc_s[...] = jnp.zeros(acc_s.shape, F32)

    ge = jnp.sum(jnp.where(lane == e, gate_s[...], 0.0), axis=-1, keepdims=True)
    rk = jnp.sum(jnp.where(lane == e, rank_s[...], 0.0), axis=-1, keepdims=True)
    routed = jnp.sum(jnp.where(ge > 0.0, 1.0, 0.0)).astype(I32)
    slot = lax.broadcasted_iota(I32, (tm, MOE_CAP), 1).astype(F32)
    ge_b = jnp.broadcast_to(ge, (tm, LANES))
    ge_hi = ge_b.astype(BF16)
    ge_lo = (ge_b - ge_hi.astype(F32)).astype(BF16)

    def chunk(j, _):
        scatter = jnp.where(jnp.logical_and(rk - (j * MOE_CAP).astype(F32) == slot, ge > 0.0), 1.0, 0.0)
        gather = scatter.T.astype(BF16)
        he = _dot(gather, h_s[...]).astype(BF16)
        y = _dot((_silu(_dot(he, w1_ref[...])) * _dot(he, w3_ref[...])).astype(BF16), w2_ref[...])
        gate = _dot(gather, ge_hi) + _dot(gather, ge_lo)
        y = y * jnp.tile(gate, (1, y.shape[1] // LANES))
        acc_s[...] += _dot(scatter.astype(BF16), y.astype(BF16))
        return 0
    lax.fori_loop(0, (routed + MOE_CAP - 1) // MOE_CAP, chunk, 0)

    @pl.when(e == N_EXPERTS - 1)
    def _():
        o_ref[...] = x_ref[...] + g2_ref[...] * acc_s[...]


def _moe(x, sc, sh, g2, ng, rw, rb, w1, w3, w2, *, tm):
    n, d = x.shape
    ne, _, ff = w1.shape
    mod = lambda m: (pl.BlockSpec((1, d), lambda i, e: (0, 0)) if m.shape[0] == 1
                     else pl.BlockSpec((tm, d), lambda i, e: (i, 0)))
    const = lambda shape: pl.BlockSpec(shape, lambda i, e: (0,) * len(shape), pipeline_mode=pl.Buffered(1))
    return pl.pallas_call(
        _moe_kernel,
        grid=(n // tm, ne),
        in_specs=[pl.BlockSpec((tm, d), lambda i, e: (i, 0)), mod(sc), mod(sh), mod(g2), const((1, d)),
                  const(rw.shape), const(rb.shape),
                  pl.BlockSpec((None, d, ff), lambda i, e: (e, 0, 0)),
                  pl.BlockSpec((None, d, ff), lambda i, e: (e, 0, 0)),
                  pl.BlockSpec((None, ff, d), lambda i, e: (e, 0, 0))],
        out_specs=pl.BlockSpec((tm, d), lambda i, e: (i, 0)),
        out_shape=jax.ShapeDtypeStruct((n, d), F32),
        scratch_shapes=[pltpu.VMEM((tm, d), BF16), pltpu.VMEM((tm, LANES), F32), pltpu.VMEM((tm, LANES), F32),
                        pltpu.VMEM((tm, d), F32)],
        compiler_params=_cp(("arbitrary", "arbitrary")),
        name="moe_top2",
    )(x, sc, sh, g2, ng, rw, rb, w1, w3, w2)


def _sidx_kernel(pt_ref, qi_ref, w_ref, *rest, n_pages, group, topk, t_new):
    del pt_ref
    page_refs, (ikn_ref, bias_ref, sc_ref, tot_ref) = rest[:group], rest[group:]
    step = pl.program_id(1)
    rows = t_new

    def page_scores(keys_t):
        r = jnp.maximum(_dot(qi_ref[...], keys_t.astype(BF16)), 0.0) * w_ref[...]
        acc = r[0:rows]
        for h in range(1, N_HEADS):
            acc = acc + r[h * rows:(h + 1) * rows]
        return acc

    for g in range(group):
        sc_ref[step * group + g] = page_scores(page_refs[g][...])

    @pl.when(step == n_pages // group - 1)
    def _():
        row = lax.broadcasted_iota(I32, (rows, PAGE), 0)
        col = lax.broadcasted_iota(I32, (rows, PAGE), 1)
        sc_ref[n_pages] = jnp.where(col <= row, page_scores(ikn_ref[...]), -jnp.inf)

        def counter(cmp):
            def count(thr):
                hit = jnp.where(cmp(sc_ref[...], jnp.broadcast_to(thr, (rows, PAGE))[None]), 1.0, 0.0)
                return jnp.sum(jnp.sum(hit, axis=0), axis=-1, keepdims=True)
            return count

        count_ge, count_gt = counter(lambda s, t: s >= t), counter(lambda s, t: s > t)
        thr = _kth_largest(count_ge, count_gt, (rows, 1), topk)
        need = topk - count_gt(thr)
        n_slots = n_pages + 1
        upper = jnp.where(lax.broadcasted_iota(I32, (PAGE, PAGE), 0) < lax.broadcasted_iota(I32, (PAGE, PAGE), 1),
                          1.0, 0.0).astype(BF16)
        thr_b = jnp.broadcast_to(thr, (rows, PAGE))[None]
        s_all = sc_ref[...]
        eq = jnp.where(s_all == thr_b, 1.0, 0.0).reshape(n_slots * rows, PAGE).astype(BF16)
        within = _dot(eq, upper).reshape(n_slots, rows, PAGE)
        tot_ref[...] = _dot(eq, jnp.ones((PAGE, PAGE), BF16)).reshape(n_slots, rows, PAGE)

        def running(j, before):
            total = tot_ref[j]
            tot_ref[j] = before
            return before + total
        lax.fori_loop(0, n_slots, running, jnp.zeros((rows, PAGE), F32))
        rank = within + tot_ref[...]
        keep = jnp.where(s_all > thr_b, 0.0,
                         jnp.where(s_all == thr_b,
                                   jnp.where(rank < jnp.broadcast_to(need, (rows, PAGE))[None], 0.0, NEG), NEG))
        bias_ref[...] = jnp.where(s_all == -jnp.inf, NEG, keep)


def _page_spec(width, group, g):
    return pl.BlockSpec((None, width, PAGE), lambda b, s, pt: (pt[b, s * group + g], 0, 0))


def _sample_indexer(page_table, qi_rows, w_rows, cache_ik, ki_new, *, topk, t_new, group):
    bd, n_pages = page_table.shape
    hq = qi_rows.shape[1]
    group = min(group, n_pages)
    grid_spec = pltpu.PrefetchScalarGridSpec(
        num_scalar_prefetch=1,
        grid=(bd, n_pages // group),
        in_specs=[pl.BlockSpec((None, hq, HEAD_DIM), lambda b, s, pt: (b, 0, 0)),
                  pl.BlockSpec((None, hq, 1), lambda b, s, pt: (b, 0, 0))]
                 + [_page_spec(HEAD_DIM, group, g) for g in range(group)]
                 + [pl.BlockSpec((None, HEAD_DIM, PAGE), lambda b, s, pt: (b, 0, 0))],
        out_specs=pl.BlockSpec((None, n_pages + 1, t_new, PAGE), lambda b, s, pt: (b, 0, 0, 0)),
        scratch_shapes=[pltpu.VMEM((n_pages + 1, t_new, PAGE), F32)] * 2)
    return pl.pallas_call(
        functools.partial(_sidx_kernel, n_pages=n_pages, group=group, topk=topk, t_new=t_new),
        grid_spec=grid_spec,
        out_shape=jax.ShapeDtypeStruct((bd, n_pages + 1, t_new, PAGE), F32),
        compiler_params=_cp(("arbitrary", "arbitrary")),
        name="sample_indexer",
    )(page_table, qi_rows, w_rows, *([cache_ik] * group), ki_new)


def _sattn_kernel(pt_ref, q_ref, *rest, n_pages, group, t_new):
    del pt_ref
    k_refs, v_refs = rest[:group], rest[group:2 * group]
    kn_ref, vn_ref, mb_ref, tabl_ref, tabn_ref, o_ref, m_s, l_s, acc_s = rest[2 * group:]
    step = pl.program_id(1)
    n_steps = n_pages // group

    @pl.when(step == 0)
    def _():
        m_s[...] = jnp.full(m_s.shape, NEG, F32)
        l_s[...] = jnp.zeros(l_s.shape, F32)
        acc_s[...] = jnp.zeros(acc_s.shape, F32)

    def logits(keys_t, mask_row, table):
        s = _dot(q_ref[...], keys_t) + jnp.concatenate([mb_ref[mask_row]] * N_HEADS, axis=0)
        return s if table is None else s + table

    def update(last):
        ks = [r[...].astype(BF16) for r in k_refs] + ([kn_ref[...].astype(BF16)] if last else [])
        vs = [r[...].astype(BF16) for r in v_refs] + ([vn_ref[...].astype(BF16)] if last else [])
        ss = [logits(ks[g], step * group + g, tabl_ref[...] if (last and g == group - 1) else None)
              for g in range(group)]
        if last:
            ss.append(logits(ks[group], n_pages, tabn_ref[...]))
        m_old = m_s[...]
        m_new = m_old
        for s in ss:
            m_new = jnp.maximum(m_new, jnp.max(s, axis=-1, keepdims=True))
        alpha = jnp.exp(m_old - m_new)
        l_new = alpha * l_s[...]
        acc = alpha * acc_s[...]
        for s, v in zip(ss, vs):
            pexp = jnp.exp(s - m_new)
            l_new = l_new + jnp.sum(pexp, axis=-1, keepdims=True)
            acc = acc + _dot_nt(pexp.astype(BF16), v)
        m_s[...] = m_new
        l_s[...] = l_new
        acc_s[...] = acc

    @pl.when(step < n_steps - 1)
    def _():
        update(False)

    @pl.when(step == n_steps - 1)
    def _():
        update(True)
        r = acc_s[...] / l_s[...]
        lane_head = lax.broadcasted_iota(I32, (t_new, A_W), 1) // HEAD_DIM
        out = jnp.zeros((t_new, A_W), F32)
        for h in range(N_HEADS):
            out = out + jnp.where(lane_head == h, r[h * t_new:(h + 1) * t_new], 0.0)
        o_ref[...] = out.astype(o_ref.dtype)


def _sample_attention(page_table, q_bd, cache_k, cache_v, k_new, v_new, mask, tab_last, tab_new, *, t_new, group):
    bd, n_pages = page_table.shape
    hq = q_bd.shape[1]
    group = min(group, n_pages)
    per_b = lambda b, s, pt: (b, 0, 0)
    const2 = lambda b, s, pt: (0, 0)
    pages = [pl.BlockSpec((None, A_W, PAGE), functools.partial(lambda b, s, pt, g: (pt[b, s * group + g], 0, 0), g=g))
             for g in range(group)]
    grid_spec = pltpu.PrefetchScalarGridSpec(
        num_scalar_prefetch=1,
        grid=(bd, n_pages // group),
        in_specs=[pl.BlockSpec((None, hq, A_W), per_b)] + pages + pages
                 + [pl.BlockSpec((None, A_W, PAGE), per_b), pl.BlockSpec((None, A_W, PAGE), per_b),
                    pl.BlockSpec((None, n_pages + 1, t_new, PAGE), lambda b, s, pt: (b, 0, 0, 0)),
                    pl.BlockSpec((hq, PAGE), const2), pl.BlockSpec((hq, PAGE), const2)],
        out_specs=pl.BlockSpec((None, t_new, A_W), per_b),
        scratch_shapes=[pltpu.VMEM((hq, 1), F32), pltpu.VMEM((hq, 1), F32), pltpu.VMEM((hq, A_W), F32)])
    return pl.pallas_call(
        functools.partial(_sattn_kernel, n_pages=n_pages, group=group, t_new=t_new),
        grid_spec=grid_spec,
        out_shape=jax.ShapeDtypeStruct((bd, t_new, A_W), BF16),
        compiler_params=_cp(("arbitrary", "arbitrary")),
        name="sample_attention",
    )(page_table, q_bd, *([cache_k] * group), *([cache_v] * group), k_new, v_new, mask, tab_last, tab_new)


ATTN_TILE = 256
ATTN_CHUNK = 2048
IDX_PAGE_GROUP = 16
ATTN_PAGE_GROUP = 8
MOE_ROWS = 1024
S5_ROWS = 256
S5_SEGMENTS = 8


def _even_layer_front(x, sc1, sh1, p, *, tm, qi_dtype):
    return _inproj(x, sc1, sh1, p['ng0'], p['w_in'], p['gq_pad'], p['gk_tiled'], p['gamma'],
                   tm=tm, qi_dtype=qi_dtype)


def _odd_layer(x, m, p, h0re, h0im, *, r, seg_len):
    sh1, sc1, g1, sh2, sc2, g2 = m
    zeros = jnp.zeros((S5_LG, r, LANES), F32)
    if seg_len is None:
        lre, lim = zeros, zeros
    else:
        lre, lim = _s5_local(x, sc1, sh1, p['ng1'], p['bb'], p['a3'], r=r, rows=S5_ROWS)
    x, fre, fim = _s5_full(x, sc1, sh1, p['ng1'], g1, p['bb'], p['a3'], p['cre'], p['cim'], p['dskip'],
                           h0re, h0im, lre, lim, p['gluw'], r=r, rows=S5_ROWS, seg_len=seg_len)
    x = _moe(x, sc2, sh2, g2, p['ng1b'], p['rw'], p['rb'], p['mw1'], p['mw3'], p['mw2'],
             tm=min(MOE_ROWS, x.shape[0]))
    return x, _from_lane_groups(fre), _from_lane_groups(fim)


def kernel(x_prompt, x_sample, c_prompt, c_sample, cache_k, cache_v, cache_idx_k, state_hgrn, state_s5_re, state_s5_im, page_table, rel_bias, ada_w, ada_b, norm_g, w_in, qk_norm_g, hgrn_gamma, hgrn_onorm_g, w_out, ffn_w1, ffn_w3, ffn_w2, s5_lambda_re, s5_lambda_im, s5_log_dt, s5_b_re, s5_b_im, s5_c_re, s5_c_im, s5_d, s5_glu_w, moe_router_w, moe_router_b, moe_w1, moe_w3, moe_w2):
    bp, seq, d = x_prompt.shape
    bd, t_new, _ = x_sample.shape
    n_dec = bd * t_new
    n_phys = cache_k.shape[1]
    past = page_table.shape[1] * PAGE
    assert bp == 1 and d == D_MODEL and seq % 512 == 0
    assert n_dec == S5_ROWS and t_new == 8

    g_q, g_k = qk_norm_g[0, 0], qk_norm_g[0, 1]
    ar, ai, bbr_t, bbi_t = _s5_prep(s5_lambda_re[0], s5_lambda_im[0], s5_log_dt[0], s5_b_re[0], s5_b_im[0])
    bb = [_s5_blockdiag_in(bbr_t).astype(BF16), _s5_blockdiag_in(bbi_t).astype(BF16)]
    p = dict(
        ng0=norm_g[0, 0][None], ng0b=norm_g[0, 1][None], ng1=norm_g[1, 0][None], ng1b=norm_g[1, 1][None],
        w_in=_pack_w_in(w_in[0]), gq_pad=jnp.tile(g_q, 2 * N_HEADS)[None], gk_tiled=jnp.tile(g_k, N_HEADS)[None],
        gamma=hgrn_gamma, onorm=jnp.tile(hgrn_onorm_g[0], N_HEADS)[None],
        wa=w_out[0, :A_W].astype(BF16), wb=w_out[0, A_W:].astype(BF16),
        w1=ffn_w1[0].astype(BF16), w3=ffn_w3[0].astype(BF16), w2=ffn_w2[0].astype(BF16),
        bb=bb, a3=[ar.reshape(S5_LG, 1, LANES), ai.reshape(S5_LG, 1, LANES)],
        cre=_s5_blockdiag_out(s5_c_re[0]).astype(BF16), cim=_s5_blockdiag_out(s5_c_im[0]).astype(BF16),
        dskip=s5_d[0][None], gluw=s5_glu_w[0].astype(BF16),
        rw=jnp.pad(moe_router_w[0], ((0, 0), (0, LANES - N_EXPERTS))).astype(BF16),
        rb=jnp.pad(moe_router_b[0], (0, LANES - N_EXPERTS))[None],
        mw1=moe_w1[0].astype(BF16), mw3=moe_w3[0].astype(BF16), mw2=moe_w2[0].astype(BF16),
    )

    c_rows = bp + bd
    c_all = jnp.concatenate([c_prompt, c_sample, jnp.zeros((-c_rows % 8, d), F32)], axis=0)
    mod = _ada(c_all, ada_w.astype(BF16), ada_b)

    def mods(layer, rows, expand):
        m = mod[layer, rows]
        return tuple(expand(m[:, i * d:(i + 1) * d]) for i in range(6))

    xp = x_prompt.reshape(seq, d)
    sh1, sc1, g1, sh2, sc2, g2 = mods(0, slice(0, 1), lambda a: a)
    (q_pad, k_f, k_b, v_f, v_b, qi, kw, gq, gk, gl, gi, gg) = _even_layer_front(
        xp, sc1, sh1, p, tm=512, qi_dtype=BF16)
    ki_p = kw[:, :HEAD_DIM]
    mask = _prompt_indexer(qi.T, kw[:, HEAD_DIM:HEAD_DIM + N_HEADS].T, ki_p.astype(BF16),
                           tq=ATTN_TILE, tk=ATTN_TILE, topk=min(TOPK_MAX, seq // 4))
    a_out = _prompt_attention(q_pad.T, k_b, v_b.T, mask, rel_bias, tile=ATTN_TILE, chunk=ATTN_CHUNK).T
    b_out, st_p = _gla(gq, gk, gi, gl, gg, jnp.zeros((1, N_HEADS, HEAD_DIM, HEAD_DIM), F32), p['onorm'],
                       batch=1, chunk=64)
    xp = _even_tail(xp, a_out, b_out, g1, sc2, sh2, g2, p['ng0b'], p['wa'], p['wb'], p['w1'], p['w3'], p['w2'],
                    tm=512)
    seg_len = seq // S5_SEGMENTS
    to_seg = lambda a: a.reshape(S5_SEGMENTS, seg_len, d).transpose(1, 0, 2).reshape(seq, d)
    zero_state = jnp.zeros((S5_LG, S5_SEGMENTS, LANES), F32)
    xp, fre_p, fim_p = _odd_layer(to_seg(xp), mods(1, slice(0, 1), lambda a: a), p, zero_state, zero_state,
                                  r=S5_SEGMENTS, seg_len=seg_len)
    y_prompt = xp.reshape(seg_len, S5_SEGMENTS, d).transpose(1, 0, 2).reshape(bp, seq, d)

    xs = x_sample.reshape(n_dec, d)
    per_token = lambda a: jnp.repeat(a, t_new, axis=0)
    sh1, sc1, g1, sh2, sc2, g2 = mods(0, slice(bp, bp + bd), per_token)
    (q_pad_s, k_fs, _, v_fs, _, qi_s, kw_s, gq, gk, gl, gi, gg) = _even_layer_front(
        xs, sc1, sh1, p, tm=n_dec, qi_dtype=BF16)
    heads_first = lambda a: jnp.transpose(a, (0, 2, 1, 3)).reshape(bd, N_HEADS * t_new, a.shape[-1])
    qi_rows = heads_first(qi_s.reshape(bd, t_new, N_HEADS, HEAD_DIM))
    w_rows = heads_first(kw_s[:, HEAD_DIM:HEAD_DIM + N_HEADS].reshape(bd, t_new, N_HEADS, 1))
    pad_new = lambda a: jnp.swapaxes(jnp.pad(a.reshape(bd, t_new, -1), ((0, 0), (0, PAGE - t_new), (0, 0))), 1, 2)
    ki_s = kw_s[:, :HEAD_DIM]
    mask_s = _sample_indexer(page_table, qi_rows, w_rows, jnp.swapaxes(cache_idx_k[0], 1, 2), pad_new(ki_s),
                             topk=min(TOPK_MAX, (past + t_new) // 4), t_new=t_new, group=IDX_PAGE_GROUP)
    q4 = q_pad_s.reshape(bd, t_new, N_HEADS, LANES)
    even = (jnp.arange(N_HEADS) % 2 == 0)[None, None, :, None]
    q_nat = jnp.where(even, q4[..., :HEAD_DIM], q4[..., HEAD_DIM:])
    q_bd = jnp.einsum('bthd,hg->bhtgd', q_nat, jnp.eye(N_HEADS, dtype=q_nat.dtype))
    q_bd = q_bd.reshape(bd, N_HEADS * t_new, A_W)
    qpos = np.arange(t_new)[:, None]
    col = np.arange(PAGE)[None, :]
    rows_hq = lambda t: t.reshape(N_HEADS * t_new, PAGE)
    tab_last = rows_hq(_bias_by_distance(rel_bias, PAGE + qpos - col))
    tab_new = rows_hq(_bias_by_distance(rel_bias, qpos - col))
    feature_major = lambda c: jnp.transpose(c[0], (0, 2, 3, 1)).reshape(n_phys, A_W, PAGE)
    a_out_s = _sample_attention(page_table, q_bd, feature_major(cache_k), feature_major(cache_v),
                                pad_new(k_fs), pad_new(v_fs), mask_s, tab_last, tab_new,
                                t_new=t_new, group=ATTN_PAGE_GROUP).reshape(n_dec, A_W)
    gla_chunk = 64
    pad_t = lambda a: jnp.pad(a.reshape(bd, t_new, A_W), ((0, 0), (0, gla_chunk - t_new), (0, 0))).reshape(-1, A_W)
    b_out_s, st_s = _gla(pad_t(gq), pad_t(gk), pad_t(gi), pad_t(gl), pad_t(gg),
                         state_hgrn[0], p['onorm'], batch=bd, chunk=gla_chunk)
    b_out_s = b_out_s.reshape(bd, gla_chunk, A_W)[:, :t_new].reshape(n_dec, A_W)
    xs = _even_tail(xs, a_out_s, b_out_s, g1, sc2, sh2, g2, p['ng0b'], p['wa'], p['wb'],
                    p['w1'], p['w3'], p['w2'], tm=n_dec)
    step_major = lambda a: a.reshape(bd, t_new, d).transpose(1, 0, 2).reshape(n_dec, d)
    xs, fre_s, fim_s = _odd_layer(step_major(xs), mods(1, slice(bp, bp + bd), lambda a: jnp.tile(a, (t_new, 1))), p,
                                  _to_lane_groups(state_s5_re[0].reshape(bd, S5_W)),
                                  _to_lane_groups(state_s5_im[0].reshape(bd, S5_W)), r=bd, seg_len=None)
    y_sample = xs.reshape(t_new, bd, d).transpose(1, 0, 2)

    heads = lambda a, b, t: a.reshape(1, b, t, N_HEADS, HEAD_DIM)
    s5_state = lambda f: f.reshape(1, -1, S5_GROUPS, S5_STATE)
    return (y_prompt, y_sample,
            heads(k_f, bp, seq), heads(v_f, bp, seq), ki_p.reshape(1, bp, seq, HEAD_DIM),
            st_p[None], s5_state(fre_p[S5_SEGMENTS - 1:]), s5_state(fim_p[S5_SEGMENTS - 1:]),
            heads(k_fs, bd, t_new), heads(v_fs, bd, t_new), ki_s.reshape(1, bd, t_new, HEAD_DIM),
            st_s[None], s5_state(fre_s), s5_state(fim_s))
```

```python
import functools
import math

import numpy as np
import jax
import jax.numpy as jnp
from jax import lax
from jax.experimental import pallas as pl
from jax.experimental.pallas import tpu as pltpu

F32 = jnp.float32
BF16 = jnp.bfloat16
I32 = jnp.int32

D_MODEL = 1024
N_HEADS = 8
HEAD_DIM = 64
A_W = N_HEADS * HEAD_DIM
LANES = 128
TOPK_MAX = 256
PAGE = 128
REL_BUCKETS = 32
REL_MAX_DIST = 128
S5_GROUPS = 64
S5_CH = 16
S5_STATE = 64
S5_W = S5_GROUPS * S5_STATE
S5_BLK_GROUPS = 8
S5_NBLK = S5_GROUPS // S5_BLK_GROUPS
N_EXPERTS = 8
EPS = 1e-6
NEG = -1e30
INT_MIN = -2 ** 31
VMEM_LIMIT = 56 * 2 ** 20


def _cp(sem, vmem=VMEM_LIMIT):
    return pltpu.CompilerParams(dimension_semantics=sem, vmem_limit_bytes=vmem)


def _resident(shape):
    n = len(shape)
    return pl.BlockSpec(shape, lambda *_: (0,) * n, pipeline_mode=pl.Buffered(1))


def _dot(a, b):
    return jnp.dot(a, b, preferred_element_type=F32)


def _dot_nt(a, b):
    return lax.dot_general(a, b, (((1,), (1,)), ((), ())), preferred_element_type=F32)


def _sigmoid(x):
    return 1.0 / (1.0 + jnp.exp(-x))


def _silu(x):
    return x * _sigmoid(x)


def _gelu_tanh(x):
    return 0.5 * x * (1.0 + jnp.tanh(math.sqrt(2.0 / math.pi) * (x + 0.044715 * (x * x * x))))


def _norm_mod(x, g, sc, sh):
    ms = jnp.mean(x * x, axis=-1, keepdims=True)
    return (x * lax.rsqrt(ms + EPS) * g) * (1.0 + sc) + sh


def _seg_matrix(n, seg, dtype, scale=1.0):
    r = lax.broadcasted_iota(I32, (n, n), 0) // seg
    c = lax.broadcasted_iota(I32, (n, n), 1) // seg
    return jnp.where(r == c, scale, 0.0).astype(dtype)


def _ada_kernel(c_ref, w_ref, b_ref, o_ref):
    c = c_ref[...]
    o_ref[...] = _dot(_silu(c).astype(BF16), w_ref[...]) + b_ref[...]


def _ada(c_all, ada_w, ada_b):
    depth, d, n6 = ada_w.shape
    rows = c_all.shape[0]
    tn = 1536
    return pl.pallas_call(
        _ada_kernel,
        grid=(depth, n6 // tn),
        in_specs=[pl.BlockSpec((rows, d), lambda l, j: (0, 0)),
                  pl.BlockSpec((None, d, tn), lambda l, j: (l, 0, j)),
                  pl.BlockSpec((None, 1, tn), lambda l, j: (l, 0, j))],
        out_specs=pl.BlockSpec((None, rows, tn), lambda l, j: (l, 0, j)),
        out_shape=jax.ShapeDtypeStruct((depth, rows, n6), F32),
        compiler_params=_cp(("arbitrary", "arbitrary")),
        name="ada_mod",
    )(c_all, ada_w, ada_b.reshape(depth, 1, n6))


W_OFF_Q, W_OFF_K, W_OFF_V, W_OFF_QI, W_OFF_KW, W_OFF_B = 0, 1024, 1536, 2048, 2560, 2688


def _pack_w_in(w):
    d = w.shape[0]
    z64 = jnp.zeros((d, N_HEADS, HEAD_DIM), w.dtype)
    q = w[:, 0:512].reshape(d, N_HEADS, HEAD_DIM)
    even = (jnp.arange(N_HEADS) % 2 == 0)[None, :, None]
    q_pad = jnp.concatenate([jnp.where(even, q, z64), jnp.where(even, z64, q)], axis=-1).reshape(d, 1024)
    kw = jnp.concatenate([w[:, 2048:2120], jnp.zeros((d, 56), w.dtype)], axis=-1)
    return jnp.concatenate([q_pad, w[:, 512:2048], kw, w[:, 2120:4168]], axis=-1).astype(BF16)


def _inproj_kernel(x_ref, sc_ref, sh_ref, ng_ref, w_ref, gq_ref, gk_ref, gam_ref,
                   q_ref, kf_ref, kb_ref, vf_ref, vb_ref, qi_ref, kw_ref,
                   gq_o, gk_o, gl_o, gi_o, gg_o):
    h = _norm_mod(x_ref[...], ng_ref[...], sc_ref[...], sh_ref[...]).astype(BF16)
    ones_seg = jnp.full((LANES, LANES), 1.0 / HEAD_DIM, BF16)
    pair_seg = _seg_matrix(LANES, HEAD_DIM, BF16, 1.0 / HEAD_DIM)

    def proj(off, n):
        return _dot(h, w_ref[:, off:off + n])

    pq = proj(W_OFF_Q, 1024)
    for j in range(8):
        blk = pq[:, j * LANES:(j + 1) * LANES]
        ms = _dot((blk * blk).astype(BF16), ones_seg)
        q_ref[:, j * LANES:(j + 1) * LANES] = (
            blk * lax.rsqrt(ms + EPS) * gq_ref[:, j * LANES:(j + 1) * LANES] * 0.125).astype(q_ref.dtype)
    pk = proj(W_OFF_K, 512)
    for j in range(4):
        blk = pk[:, j * LANES:(j + 1) * LANES]
        ms = _dot((blk * blk).astype(BF16), pair_seg)
        kn = blk * lax.rsqrt(ms + EPS) * gk_ref[:, j * LANES:(j + 1) * LANES]
        kf_ref[:, j * LANES:(j + 1) * LANES] = kn
        kb_ref[:, j * LANES:(j + 1) * LANES] = kn.astype(BF16)
    pv = proj(W_OFF_V, 512)
    vf_ref[...] = pv
    vb_ref[...] = pv.astype(BF16)
    qi_ref[...] = (proj(W_OFF_QI, 512) * 0.125).astype(qi_ref.dtype)
    lane = lax.broadcasted_iota(I32, (1, LANES), 1)
    kw_ref[...] = proj(W_OFF_KW, LANES) * jnp.where(lane < HEAD_DIM, 1.0, N_HEADS ** -0.5)
    gq_o[...] = proj(W_OFF_B, 512) * 0.125
    gam = gam_ref[...]
    gmax = jnp.max(gam, axis=0, keepdims=True)
    ge = jnp.exp(gam - gmax)
    lb = ge[0:1, :] / jnp.sum(ge, axis=0, keepdims=True)
    f = lb + (1.0 - lb) * _sigmoid(proj(W_OFF_B + 512, 512))
    gk_o[...] = 1.0 - f
    gl_o[...] = jnp.log(f)
    gi_o[...] = proj(W_OFF_B + 1024, 512)
    gg_o[...] = proj(W_OFF_B + 1536, 512)


def _inproj(x, sc, sh, ng, w_packed, gq_pad, gk_tiled, gamma, *, tm):
    n, d = x.shape
    row = lambda i: (i, 0)
    mod_spec = (pl.BlockSpec((1, d), lambda i: (0, 0)) if sc.shape[0] == 1
                else pl.BlockSpec((tm, d), row))
    outs = [((n, 1024), BF16), ((n, 512), F32), ((n, 512), BF16), ((n, 512), F32), ((n, 512), BF16),
            ((n, 512), BF16), ((n, LANES), F32)] + [((n, 512), F32)] * 5
    return pl.pallas_call(
        _inproj_kernel,
        grid=(n // tm,),
        in_specs=[pl.BlockSpec((tm, d), row), mod_spec, mod_spec,
                  _resident((1, d)), _resident(w_packed.shape), _resident((1, 1024)),
                  _resident((1, 512)), _resident(gamma.shape)],
        out_specs=[pl.BlockSpec((tm, s[1]), row) for s, _ in outs],
        out_shape=[jax.ShapeDtypeStruct(s, dt) for s, dt in outs],
        compiler_params=_cp(("arbitrary",)),
        name="in_proj",
    )(x, sc, sh, ng, w_packed, gq_pad, gk_tiled, gamma)


def _key_to_f32(key):
    neg = key < 0
    mag = jnp.where(neg, -key, key)
    return lax.bitcast_convert_type(jnp.where(neg, mag | jnp.int32(INT_MIN), mag), F32)


def _f32_to_key(x):
    bits = lax.bitcast_convert_type(x, I32)
    return jnp.where(bits < 0, -(bits & jnp.int32(0x7FFFFFFF)), bits)


def _kth_largest(count_ge, count_gt, shape, topk, bounds=None):
    if bounds is None:
        bounds = (jnp.full(shape, -jnp.inf, F32), jnp.full(shape, jnp.inf, F32))

    def cond(state):
        lo, hi, done, _ = state
        return jnp.max(jnp.where(jnp.logical_and(lo != hi, done == 0), 1, 0)) == 1

    def body(state):
        lo, hi, done, thr = state
        mid = (lo | hi) - ((lo ^ hi) >> 1)
        mid_f = _key_to_f32(mid)
        cnt = count_ge(mid_f)
        enough = cnt >= topk
        hit = jnp.logical_and(cnt == topk, done == 0)
        return (jnp.where(enough, mid, lo), jnp.where(enough, hi, mid - 1),
                jnp.where(hit, 1, done), jnp.where(hit, mid_f, thr))

    lo, hi = _f32_to_key(bounds[0]), _f32_to_key(bounds[1])
    zero = jnp.zeros(shape, F32)
    n_pos, n_nonneg = count_gt(zero), count_ge(zero)
    done = jnp.where(jnp.logical_and(n_pos < topk, n_nonneg >= topk), 1, 0)
    lo = jnp.where(n_pos >= topk, jnp.maximum(lo, 1), lo)
    hi = jnp.where(n_nonneg < topk, jnp.minimum(hi, -1), hi)
    lo, _, done, thr = lax.while_loop(cond, body, (lo, hi, done, jnp.zeros(shape, F32)))
    return jnp.where(done == 1, thr, _key_to_f32(lo))


def _pidx_kernel(qit_ref, wt_ref, ki_ref, bias_ref, sc_ref, cm_ref, *, tq, tk, topk, seq):
    assert topk <= tk
    q0 = pl.program_id(0) * tq
    n_kt = (q0 + tq + tk - 1) // tk
    key = lax.broadcasted_iota(I32, (tk, tq), 0)
    qry = lax.broadcasted_iota(I32, (tk, tq), 1) + q0

    def tile_off(kt):
        return pl.multiple_of(kt * tk, tk)

    def scores(kt, _):
        off = tile_off(kt)
        ks = ki_ref[pl.ds(off, tk), :]
        acc = jnp.zeros((tk, tq), F32)
        for h in range(N_HEADS):
            x = _dot(ks, qit_ref[h * HEAD_DIM:(h + 1) * HEAD_DIM, :])
            acc = acc + wt_ref[h:h + 1, :] * jnp.maximum(x, 0.0)
        acc = jnp.where(key + off <= qry, acc, -jnp.inf)
        sc_ref[pl.ds(off, tk), :] = acc
        cm_ref[...] = jnp.maximum(cm_ref[...], acc)
        return 0
    cm_ref[...] = jnp.full((tk, tq), -jnp.inf, F32)
    n_pairs = (n_kt + 1) // 2
    lax.fori_loop(0, 2 * n_pairs, scores, 0)
    bounds = (jnp.min(cm_ref[...], axis=0, keepdims=True), jnp.max(cm_ref[...], axis=0, keepdims=True))

    def counter(cmp):
        def count(thr):
            def body(kp, c):
                off = pl.multiple_of(kp * (2 * tk), 2 * tk)
                hit = jnp.where(cmp(sc_ref[pl.ds(off, 2 * tk), :], thr), 1.0, 0.0)
                return c + jnp.sum(hit.reshape(2 * tk // 64, 8, 8, tq), axis=0)
            c = lax.fori_loop(0, n_pairs, body, jnp.zeros((8, 8, tq), F32))
            return jnp.sum(jnp.sum(c, axis=0), axis=0, keepdims=True)
        return count

    count_ge, count_gt = counter(lambda s, t: s >= t), counter(lambda s, t: s > t)
    thr = _kth_largest(count_ge, count_gt, (1, tq), topk, bounds)
    need = topk - count_gt(thr)
    lower = jnp.where(lax.broadcasted_iota(I32, (tk, tk), 0) > lax.broadcasted_iota(I32, (tk, tk), 1),
                      1.0, 0.0).astype(BF16)

    def select(kt, ties_before):
        off = tile_off(kt)
        s = sc_ref[pl.ds(off, tk), :]
        eq = jnp.where(s == thr, 1.0, 0.0)
        rank = _dot(lower, eq.astype(BF16)) + ties_before
        keep = jnp.where(s > thr, 0.0, jnp.where(s == thr, jnp.where(rank < need, 0.0, NEG), NEG))
        bias_ref[pl.ds(off, tk), :] = jnp.where(key + off <= qry, keep, NEG).astype(BF16)
        return ties_before + jnp.sum(eq, axis=0, keepdims=True)

    def select_all_ties(kt, _):
        off = tile_off(kt)
        keep = jnp.where(sc_ref[pl.ds(off, tk), :] >= thr, 0.0, NEG)
        bias_ref[pl.ds(off, tk), :] = jnp.where(key + off <= qry, keep, NEG).astype(BF16)
        return 0

    surplus = jnp.max(counter(lambda s, t: s == t)(thr) - need) > 0.0

    @pl.when(surplus)
    def _():
        lax.fori_loop(0, n_kt, select, jnp.zeros((1, tq), F32))

    @pl.when(jnp.logical_not(surplus))
    def _():
        lax.fori_loop(0, n_kt, select_all_ties, 0)

    def fill(kt, _):
        bias_ref[pl.ds(tile_off(kt), tk), :] = jnp.full((tk, tq), NEG, BF16)
        return 0
    lax.fori_loop(n_kt, seq // tk, fill, 0)


def _prompt_indexer(qi_t, w_t, ki, *, tq, tk, topk):
    seq = ki.shape[0]
    return pl.pallas_call(
        functools.partial(_pidx_kernel, tq=tq, tk=tk, topk=topk, seq=seq),
        grid=(seq // tq,),
        in_specs=[pl.BlockSpec((A_W, tq), lambda i: (0, i)),
                  pl.BlockSpec((N_HEADS, tq), lambda i: (0, i)),
                  _resident(ki.shape)],
        out_specs=pl.BlockSpec((None, seq, tq), lambda i: (i, 0, 0)),
        out_shape=jax.ShapeDtypeStruct((seq // tq, seq, tq), BF16),
        scratch_shapes=[pltpu.VMEM((seq, tq), F32), pltpu.VMEM((tk, tq), F32)],
        compiler_params=_cp(("arbitrary",)),
        name="prompt_indexer",
    )(qi_t, w_t, ki)


def _t5_bucket_table():
    n = np.arange(REL_MAX_DIST, dtype=np.int64)
    max_exact = REL_BUCKETS // 2
    nf = np.maximum(n, 1).astype(np.float32)
    large = max_exact + (np.log(nf / np.float32(max_exact)) / np.float32(math.log(REL_MAX_DIST / max_exact))
                         * np.float32(REL_BUCKETS - max_exact)).astype(np.int32)
    large = np.minimum(large, REL_BUCKETS - 1)
    return np.where(n < max_exact, n, large).astype(np.int32)


def _bias_by_distance(rel_bias, dist):
    table = _t5_bucket_table()
    bucket = np.where(dist >= REL_MAX_DIST, REL_BUCKETS - 1, table[np.clip(dist, 0, REL_MAX_DIST - 1)])
    b = jnp.moveaxis(rel_bias[bucket], -1, 0)
    far = rel_bias[REL_BUCKETS - 1].reshape((N_HEADS,) + (1,) * dist.ndim)
    return (b - far).astype(F32)


V_ROWS = HEAD_DIM + 16


def _pattn_kernel(qt_ref, k_ref, vt_ref, mask_ref, tab_ref, ot_ref, m_ref, acc_ref, *slots, tile, chunk):
    slot_a, slot_b = slots[:4], slots[4:]
    qb = pl.program_id(0)
    first = pl.program_id(1) * (chunk // tile)
    n_tiles = chunk // tile

    @pl.when(pl.program_id(1) == 0)
    def _():
        m_ref[...] = jnp.full(m_ref.shape, NEG, F32)
        acc_ref[...] = jnp.zeros(acc_ref.shape, F32)

    def scores(j, slot, near):
        s_ref, mx_ref, _, _ = slot
        off = pl.multiple_of(j * tile, tile)
        mb = mask_ref[pl.ds(off, tile), :].astype(F32)
        for h in range(N_HEADS):
            pr = h // 2
            s = _dot(k_ref[pl.ds(off, tile), pr * LANES:(pr + 1) * LANES], qt_ref[h * LANES:(h + 1) * LANES, :])
            s = s + mb
            if near is not None:
                s = s + tab_ref[near, h]
            s_ref[h] = s
            mx_ref[h] = jnp.max(s, axis=0, keepdims=True)

    def accumulate(j, slot):
        s_ref, mx_ref, p_ref, al_ref = slot
        off = pl.multiple_of(j * tile, tile)
        for h in range(N_HEADS):
            m_old = m_ref[h]
            m_new = jnp.maximum(m_old, mx_ref[h])
            alpha = jnp.exp(m_old - m_new)
            p_ref[h] = jnp.exp(s_ref[h] - m_new).astype(BF16)
            al_ref[h] = alpha
            m_ref[h] = m_new
        for h in range(N_HEADS):
            pv = _dot(vt_ref[h * V_ROWS:(h + 1) * V_ROWS, pl.ds(off, tile)], p_ref[h])
            acc_ref[h] = al_ref[h] * acc_ref[h] + pv

    n_far = jnp.clip(qb - 1 - first, 0, n_tiles)
    n_pairs = n_far // 2

    @pl.when(n_pairs > 0)
    def _():
        scores(0, slot_a, None)

    def pair(i, _):
        scores(2 * i + 1, slot_b, None)
        accumulate(2 * i, slot_a)
        scores(jnp.minimum(2 * i + 2, 2 * n_pairs - 1), slot_a, None)
        accumulate(2 * i + 1, slot_b)
        return 0
    lax.fori_loop(0, n_pairs, pair, 0)

    for near, j, run in ((None, n_far - 1, n_far % 2 == 1),
                         (1, qb - 1 - first, jnp.logical_and(qb - 1 - first >= 0, qb - 1 - first < n_tiles)),
                         (0, qb - first, jnp.logical_and(qb - first >= 0, qb - first < n_tiles))):
        @pl.when(run)
        def _():
            scores(j, slot_a, near)
            accumulate(j, slot_a)

    @pl.when(pl.program_id(1) == pl.num_programs(1) - 1)
    def _():
        for h in range(N_HEADS):
            acc = acc_ref[h]
            ot_ref[h * HEAD_DIM:(h + 1) * HEAD_DIM, :] = (
                acc[:HEAD_DIM] / acc[HEAD_DIM:HEAD_DIM + 1]).astype(ot_ref.dtype)


def _near_bias_tables(rel_bias, tile):
    period = 2 * tile
    j = np.arange(period)
    offset = np.where(j < tile, j, j - period)
    w = _bias_by_distance(rel_bias, np.stack([offset, tile + offset]))
    skew = jnp.tile(w, (1, 1, tile))[..., :tile * (period - 1)].reshape(N_HEADS, 2, tile, period - 1)
    return jnp.moveaxis(skew[..., :tile], 0, 1)


def _prompt_attention(q_t, k_bf, v_t, mask, rel_bias, *, tile, chunk):
    seq = k_bf.shape[0]
    chunk = min(chunk, seq)
    tab = _near_bias_tables(rel_bias, tile)
    v_ext = jnp.concatenate([v_t.reshape(N_HEADS, HEAD_DIM, seq),
                             jnp.ones((N_HEADS, V_ROWS - HEAD_DIM, seq), v_t.dtype)], axis=1)
    v_ext = v_ext.reshape(N_HEADS * V_ROWS, seq)
    last_chunk = lambda i: (i * tile + tile - 1) // chunk
    return pl.pallas_call(
        functools.partial(_pattn_kernel, tile=tile, chunk=chunk),
        grid=(seq // tile, seq // chunk),
        in_specs=[pl.BlockSpec((N_HEADS * LANES, tile), lambda i, c: (0, i)),
                  pl.BlockSpec((chunk, A_W), lambda i, c: (jnp.minimum(c, last_chunk(i)), 0)),
                  pl.BlockSpec((N_HEADS * V_ROWS, chunk), lambda i, c: (0, jnp.minimum(c, last_chunk(i)))),
                  pl.BlockSpec((None, chunk, tile), lambda i, c: (i, jnp.minimum(c, last_chunk(i)), 0)),
                  pl.BlockSpec(tab.shape, lambda i, c: (0, 0, 0, 0), pipeline_mode=pl.Buffered(1))],
        out_specs=pl.BlockSpec((A_W, tile), lambda i, c: (0, i)),
        out_shape=jax.ShapeDtypeStruct((A_W, seq), BF16),
        scratch_shapes=[pltpu.VMEM((N_HEADS, 1, tile), F32), pltpu.VMEM((N_HEADS, V_ROWS, tile), F32)]
                       + [pltpu.VMEM((N_HEADS, tile, tile), F32), pltpu.VMEM((N_HEADS, 1, tile), F32),
                          pltpu.VMEM((N_HEADS, tile, tile), BF16), pltpu.VMEM((N_HEADS, 1, tile), F32)] * 2,
        compiler_params=_cp(("arbitrary", "arbitrary")),
        name="prompt_attention",
    )(q_t, k_bf, v_ext, mask, tab)


def _cumsum_rows(x):
    c = x.shape[0]
    tri = jnp.where(lax.broadcasted_iota(I32, (c, c), 0) >= lax.broadcasted_iota(I32, (c, c), 1),
                    1.0, 0.0).astype(BF16)
    hi = x.astype(BF16)
    r1 = x - hi.astype(F32)
    mid = r1.astype(BF16)
    lo = (r1 - mid.astype(F32)).astype(BF16)
    return _dot(tri, hi) + (_dot(tri, mid) + _dot(tri, lo))


def _gla_kernel(q_ref, k_ref, v_ref, g_ref, gate_ref, s0_ref, on_ref, diag_ref, o_ref, sfin_ref,
                st_ref, b_ref, oi_ref, *, chunk):
    @pl.when(pl.program_id(1) == 0)
    def _():
        st_ref[...] = jnp.zeros(st_ref.shape, F32)
        for h in range(N_HEADS):
            st_ref[h * HEAD_DIM:(h + 1) * HEAD_DIM, h * HEAD_DIM:(h + 1) * HEAD_DIM] = s0_ref[h]

    w = q_ref.shape[-1]
    b = _cumsum_rows(g_ref[...])
    b_ref[...] = b
    k = k_ref[...]
    v = v_ref[...]
    seg = _seg_matrix(w, HEAD_DIM, BF16)
    st = st_ref[...]
    o_inter = _dot_nt((q_ref[...] * jnp.exp(b)).astype(BF16), st.astype(BF16))
    for grp in range(chunk // 8):
        n = 8 * (grp + 1)
        rows = lax.broadcasted_iota(I32, (n, w), 0)
        out_rows = []
        for t in range(8 * grp, n):
            dec = jnp.where(rows <= t, jnp.exp(b_ref[t:t + 1, :] - b_ref[0:n, :]), 0.0)
            prod = (q_ref[t:t + 1, :] * dec * k_ref[0:n, :]).astype(BF16)
            out_rows.append(jnp.sum(_dot(prod, seg) * v_ref[0:n, :], axis=0, keepdims=True))
        oi_ref[8 * grp:n, :] = jnp.concatenate(out_rows, axis=0)

    o = o_inter + oi_ref[...]
    ms = _dot((o * o).astype(BF16), seg) * (1.0 / HEAD_DIM)
    o_ref[...] = (o * lax.rsqrt(ms + EPS) * on_ref[...] * _silu(gate_ref[...])).astype(o_ref.dtype)

    b_last = b[chunk - 1:chunk, :]
    kd = (k * jnp.exp(b_last - b)).astype(BF16)
    upd = _dot(v.T.astype(BF16), kd)
    st_new = st * jnp.exp(b_last) + upd * diag_ref[...]
    st_ref[...] = st_new

    @pl.when(pl.program_id(1) == pl.num_programs(1) - 1)
    def _():
        for h in range(N_HEADS):
            sfin_ref[h] = st_new[h * HEAD_DIM:(h + 1) * HEAD_DIM, h * HEAD_DIM:(h + 1) * HEAD_DIM]


def _gla(gq, gk, gv, glog, gate, s0, onorm_tiled, *, batch, chunk):
    n, w = gq.shape
    nc = n // batch // chunk
    row = lambda b, c: (b * nc + c, 0)
    tile = pl.BlockSpec((chunk, w), row)
    state = pl.BlockSpec((None, N_HEADS, HEAD_DIM, HEAD_DIM), lambda b, c: (b, 0, 0, 0))
    same_head = np.equal.outer(np.arange(w) // HEAD_DIM, np.arange(w) // HEAD_DIM).astype(np.float32)
    const = lambda shape: pl.BlockSpec(shape, lambda b, c: (0, 0), pipeline_mode=pl.Buffered(1))
    o, s_fin = pl.pallas_call(
        functools.partial(_gla_kernel, chunk=chunk),
        grid=(batch, nc),
        in_specs=[tile] * 5 + [state, const((1, w)), const((w, w))],
        out_specs=[tile, state],
        out_shape=[jax.ShapeDtypeStruct((n, w), BF16),
                   jax.ShapeDtypeStruct((batch, N_HEADS, HEAD_DIM, HEAD_DIM), F32)],
        scratch_shapes=[pltpu.VMEM((w, w), F32), pltpu.VMEM((chunk, w), F32), pltpu.VMEM((chunk, w), F32)],
        compiler_params=_cp(("arbitrary", "arbitrary")),
        name="hgrn2_gla",
    )(gq, gk, gv, glog, gate, jnp.swapaxes(s0, 2, 3), onorm_tiled, jnp.asarray(same_head, F32))
    return o, jnp.swapaxes(s_fin, 2, 3)


def _even_tail_kernel(x_ref, a_ref, b_ref, g1_ref, sc_ref, sh_ref, g2_ref, ng_ref,
                      wa_ref, wb_ref, w1_ref, w3_ref, w2_ref, o_ref, *, ff_split):
    mix = _dot(a_ref[...], wa_ref[...]) + _dot(b_ref[...], wb_ref[...])
    x1 = x_ref[...] + g1_ref[...] * mix
    h = _norm_mod(x1, ng_ref[...], sc_ref[...], sh_ref[...]).astype(BF16)
    ff = jnp.zeros(x1.shape, F32)
    step = w1_ref.shape[1] // ff_split
    for j in range(ff_split):
        a = _dot(h, w1_ref[:, j * step:(j + 1) * step])
        g = _dot(h, w3_ref[:, j * step:(j + 1) * step])
        ff = ff + _dot((_silu(a) * g).astype(BF16), w2_ref[j * step:(j + 1) * step, :])
    o_ref[...] = x1 + g2_ref[...] * ff


def _mod_spec(m, tm, d):
    return (pl.BlockSpec((1, d), lambda i: (0, 0)) if m.shape[0] == 1
            else pl.BlockSpec((tm, d), lambda i: (i, 0)))


def _even_tail(x, a_out, b_out, g1, sc2, sh2, g2, ng2, wa, wb, w1, w3, w2, *, tm):
    n, d = x.shape
    row = lambda i: (i, 0)
    return pl.pallas_call(
        functools.partial(_even_tail_kernel, ff_split=2),
        grid=(n // tm,),
        in_specs=[pl.BlockSpec((tm, d), row), pl.BlockSpec((tm, A_W), row), pl.BlockSpec((tm, A_W), row),
                  _mod_spec(g1, tm, d), _mod_spec(sc2, tm, d), _mod_spec(sh2, tm, d), _mod_spec(g2, tm, d),
                  _resident((1, d)), _resident(wa.shape), _resident(wb.shape),
                  _resident(w1.shape), _resident(w3.shape), _resident(w2.shape)],
        out_specs=pl.BlockSpec((tm, d), row),
        out_shape=jax.ShapeDtypeStruct((n, d), F32),
        compiler_params=_cp(("arbitrary",)),
        name="even_tail",
    )(x, a_out, b_out, g1, sc2, sh2, g2, ng2, wa, wb, w1, w3, w2)


S5_LG = S5_W // LANES
S5_LG_PER_BLK = S5_LG // S5_NBLK


def _s5_prep_kernel(lr_ref, li_ref, ldt_ref, br_ref, bi_ref, ar_o, ai_o, bbr_o, bbi_o):
    lr = jnp.minimum(lr_ref[...], -1e-4)
    li = li_ref[...]
    dt = jnp.exp(ldt_ref[...])
    mag = jnp.exp(lr * dt)
    a_re = mag * jnp.cos(li * dt)
    a_im = mag * jnp.sin(li * dt)
    den = lr * lr + li * li
    nr = a_re - 1.0
    coef_re = (nr * lr + a_im * li) / den
    coef_im = (a_im * lr - nr * li) / den
    ar_o[...] = a_re
    ai_o[...] = a_im
    br = br_ref[...]
    bi = bi_ref[...]
    bbr_o[...] = coef_re * br - coef_im * bi
    bbi_o[...] = coef_re * bi + coef_im * br


def _s5_prep(lam_re, lam_im, log_dt, b_re, b_im):
    g, p = lam_re.shape
    v3 = lambda a: a.reshape(g, 1, p)
    bt = lambda b: jnp.transpose(b, (0, 2, 1))
    sds = jax.ShapeDtypeStruct
    return pl.pallas_call(
        _s5_prep_kernel,
        out_shape=[sds((g, 1, p), F32), sds((g, 1, p), F32), sds((g, S5_CH, p), F32), sds((g, S5_CH, p), F32)],
        name="s5_prep",
    )(v3(lam_re), v3(lam_im), log_dt.reshape(g, 1, 1), bt(b_re), bt(b_im))


def _s5_blockdiag_in(bb_t):
    x = bb_t.reshape(S5_NBLK, S5_BLK_GROUPS, S5_CH, S5_STATE)
    bd = jnp.einsum('bgcp,gh->bgchp', x, jnp.eye(S5_BLK_GROUPS, dtype=x.dtype))
    return bd.reshape(S5_NBLK, S5_BLK_GROUPS * S5_CH, S5_BLK_GROUPS * S5_STATE)


def _s5_blockdiag_out(c):
    x = c.reshape(S5_NBLK, S5_BLK_GROUPS, S5_CH, S5_STATE)
    bd = jnp.einsum('bgcp,gh->bgphc', x, jnp.eye(S5_BLK_GROUPS, dtype=x.dtype))
    return bd.reshape(S5_NBLK, S5_BLK_GROUPS * S5_STATE, S5_BLK_GROUPS * S5_CH)


def _s5_project_in(u, bbr_ref, bbi_ref, xre_s, xim_s):
    for blk in range(S5_NBLK):
        ub = u[:, blk * LANES:(blk + 1) * LANES].astype(BF16)
        for w_ref, dst in ((bbr_ref, xre_s), (bbi_ref, xim_s)):
            r = _dot(ub, w_ref[blk])
            for j in range(S5_LG_PER_BLK):
                dst[blk * S5_LG_PER_BLK + j] = r[:, j * LANES:(j + 1) * LANES]


def _s5_scan(xre_s, xim_s, st_re, st_im, ar_ref, ai_ref, *, n_steps, r, store):
    def blk_body(blk, _):
        base = blk * S5_LG_PER_BLK
        ar = [jnp.broadcast_to(ar_ref[base + j], (r, LANES)) for j in range(S5_LG_PER_BLK)]
        ai = [jnp.broadcast_to(ai_ref[base + j], (r, LANES)) for j in range(S5_LG_PER_BLK)]

        def step(s, carry):
            r0 = pl.multiple_of(s * r, r)
            new = []
            for j in range(S5_LG_PER_BLK):
                sr, si = carry[2 * j], carry[2 * j + 1]
                nr = ar[j] * sr - ai[j] * si + xre_s[base + j, pl.ds(r0, r), :]
                ni = ar[j] * si + ai[j] * sr + xim_s[base + j, pl.ds(r0, r), :]
                if store:
                    xre_s[base + j, pl.ds(r0, r), :] = nr
                    xim_s[base + j, pl.ds(r0, r), :] = ni
                new += [nr, ni]
            return tuple(new)

        init = []
        for j in range(S5_LG_PER_BLK):
            init += [st_re[base + j], st_im[base + j]]
        fin = lax.fori_loop(0, n_steps, step, tuple(init))
        for j in range(S5_LG_PER_BLK):
            st_re[base + j] = fin[2 * j]
            st_im[base + j] = fin[2 * j + 1]
        return 0
    lax.fori_loop(0, S5_NBLK, blk_body, 0)


def _s5_local_kernel(x_ref, sc_ref, sh_ref, ng_ref, bbr_ref, bbi_ref, ar_ref, ai_ref,
                     lre_o, lim_o, xre_s, xim_s, st_re, st_im, *, n_steps, r):
    @pl.when(pl.program_id(0) == 0)
    def _():
        st_re[...] = jnp.zeros(st_re.shape, F32)
        st_im[...] = jnp.zeros(st_im.shape, F32)
    u = _norm_mod(x_ref[...], ng_ref[...], sc_ref[...], sh_ref[...])
    _s5_project_in(u, bbr_ref, bbi_ref, xre_s, xim_s)
    _s5_scan(xre_s, xim_s, st_re, st_im, ar_ref, ai_ref, n_steps=n_steps, r=r, store=False)
    lre_o[...] = st_re[...]
    lim_o[...] = st_im[...]


def _cpow(ar, ai, n):
    rr, ri = None, None
    br, bi = ar, ai
    while n:
        if n & 1:
            if rr is None:
                rr, ri = br, bi
            else:
                rr, ri = rr * br - ri * bi, rr * bi + ri * br
        n >>= 1
        if n:
            br, bi = br * br - bi * bi, 2.0 * br * bi
    return rr, ri


def _s5_full_kernel(x_ref, sc_ref, sh_ref, ng_ref, g1_ref, bbr_ref, bbi_ref, ar_ref, ai_ref,
                    cre_ref, cim_ref, d_ref, h0re_ref, h0im_ref, lre_ref, lim_ref, gluw_ref,
                    o_ref, fre_o, fim_o, xre_s, xim_s, st_re, st_im, *, n_steps, r, seg_len):
    @pl.when(pl.program_id(0) == 0)
    def _():
        if seg_len is None:
            st_re[...] = h0re_ref[...]
            st_im[...] = h0im_ref[...]
        else:
            pr, pi = _cpow(ar_ref[...], ai_ref[...], seg_len)
            sr = h0re_ref[:, 0:1, :]
            si = h0im_ref[:, 0:1, :]
            st_re[:, 0:1, :] = sr
            st_im[:, 0:1, :] = si
            for row in range(1, r):
                sr, si = (lre_ref[:, row - 1:row, :] + (pr * sr - pi * si),
                          lim_ref[:, row - 1:row, :] + (pr * si + pi * sr))
                st_re[:, row:row + 1, :] = sr
                st_im[:, row:row + 1, :] = si

    x = x_ref[...]
    u = _norm_mod(x, ng_ref[...], sc_ref[...], sh_ref[...])
    _s5_project_in(u, bbr_ref, bbi_ref, xre_s, xim_s)
    _s5_scan(xre_s, xim_s, st_re, st_im, ar_ref, ai_ref, n_steps=n_steps, r=r, store=True)
    fre_o[...] = st_re[...]
    fim_o[...] = st_im[...]
    ys = []
    for blk in range(S5_NBLK):
        acc = None
        for j in range(S5_LG_PER_BLK):
            lg = blk * S5_LG_PER_BLK + j
            t = (_dot(xre_s[lg].astype(BF16), cre_ref[blk, j * LANES:(j + 1) * LANES, :])
                 - _dot(xim_s[lg].astype(BF16), cim_ref[blk, j * LANES:(j + 1) * LANES, :]))
            acc = t if acc is None else acc + t
        ys.append(acc)
    y = jnp.concatenate(ys, axis=-1) + d_ref[...] * u
    z = _dot(_gelu_tanh(y).astype(BF16), gluw_ref[...])
    dm = x.shape[-1]
    mix = z[:, :dm] * _sigmoid(z[:, dm:])
    o_ref[...] = x + g1_ref[...] * mix


def _s5_weight_specs(ws):
    return [_resident(w.shape) for w in ws]


def _s5_local(x, sc, sh, ng, bb, a3, *, r, rows):
    n, d = x.shape
    n_steps = rows // r
    sds = jax.ShapeDtypeStruct((S5_LG, r, LANES), F32)
    return pl.pallas_call(
        functools.partial(_s5_local_kernel, n_steps=n_steps, r=r),
        grid=(n // rows,),
        in_specs=[pl.BlockSpec((rows, d), lambda i: (i, 0)), _mod_spec(sc, rows, d), _mod_spec(sh, rows, d),
                  _resident((1, d))] + _s5_weight_specs(bb + a3),
        out_specs=[_resident(sds.shape)] * 2,
        out_shape=[sds, sds],
        scratch_shapes=[pltpu.VMEM((S5_LG, rows, LANES), F32)] * 2 + [pltpu.VMEM((S5_LG, r, LANES), F32)] * 2,
        compiler_params=_cp(("arbitrary",)),
        name="s5_local_scan",
    )(x, sc, sh, ng, *bb, *a3)


def _s5_full(x, sc, sh, ng, g1, bb, a3, cre, cim, dskip, h0re, h0im, lre, lim, gluw, *, r, rows, seg_len):
    n, d = x.shape
    n_steps = rows // r
    st = jax.ShapeDtypeStruct((S5_LG, r, LANES), F32)
    return pl.pallas_call(
        functools.partial(_s5_full_kernel, n_steps=n_steps, r=r, seg_len=seg_len),
        grid=(n // rows,),
        in_specs=[pl.BlockSpec((rows, d), lambda i: (i, 0)), _mod_spec(sc, rows, d), _mod_spec(sh, rows, d),
                  _resident((1, d)), _mod_spec(g1, rows, d)]
                 + _s5_weight_specs(bb + a3 + [cre, cim, dskip, h0re, h0im, lre, lim, gluw]),
        out_specs=[pl.BlockSpec((rows, d), lambda i: (i, 0)), _resident(st.shape), _resident(st.shape)],
        out_shape=[jax.ShapeDtypeStruct((n, d), F32), st, st],
        scratch_shapes=[pltpu.VMEM((S5_LG, rows, LANES), F32)] * 2 + [pltpu.VMEM((S5_LG, r, LANES), F32)] * 2,
        compiler_params=_cp(("arbitrary",)),
        name="s5_scan_glu",
    )(x, sc, sh, ng, g1, *bb, *a3, cre, cim, dskip, h0re, h0im, lre, lim, gluw)


def _to_lane_groups(s):
    r = s.shape[0]
    return jnp.transpose(s.reshape(r, S5_LG, LANES), (1, 0, 2))


def _from_lane_groups(s):
    r = s.shape[1]
    return jnp.transpose(s, (1, 0, 2)).reshape(r, S5_W)


MOE_CAP = 128


def _moe_kernel(x_ref, sc_ref, sh_ref, g2_ref, ng_ref, rw_ref, rb_ref, w1_ref, w3_ref, w2_ref,
                o_ref, h_s, gate_s, rank_s, acc_s):
    e = pl.program_id(1)
    tm = gate_s.shape[0]
    lane = lax.broadcasted_iota(I32, gate_s.shape, 1)

    @pl.when(e == 0)
    def _():
        h = _norm_mod(x_ref[...], ng_ref[...], sc_ref[...], sh_ref[...])
        h_s[...] = h.astype(BF16)
        logits = _dot(h_s[...], rw_ref[...]) + rb_ref[...]
        logits = jnp.where(lane < N_EXPERTS, logits, -jnp.inf)
        m1 = jnp.max(logits, axis=-1, keepdims=True)
        i1 = jnp.min(jnp.where(logits == m1, lane, LANES), axis=-1, keepdims=True)
        rest = jnp.where(lane == i1, -jnp.inf, logits)
        m2 = jnp.max(rest, axis=-1, keepdims=True)
        i2 = jnp.min(jnp.where(rest == m2, lane, LANES), axis=-1, keepdims=True)
        e2 = jnp.exp(m2 - m1)
        den = 1.0 + e2
        gates = jnp.where(lane == i1, 1.0 / den, 0.0) + jnp.where(lane == i2, e2 / den, 0.0)
        gate_s[...] = gates
        earlier = jnp.where(lax.broadcasted_iota(I32, (tm, tm), 0) > lax.broadcasted_iota(I32, (tm, tm), 1),
                            1.0, 0.0).astype(BF16)
        rank_s[...] = _dot(earlier, jnp.where(gates > 0.0, 1.0, 0.0).astype(BF16))
        acc_s[...] = jnp.zeros(acc_s.shape, F32)

    ge = jnp.sum(jnp.where(lane == e, gate_s[...], 0.0), axis=-1, keepdims=True)
    rk = jnp.sum(jnp.where(lane == e, rank_s[...], 0.0), axis=-1, keepdims=True)
    routed = jnp.sum(jnp.where(ge > 0.0, 1.0, 0.0)).astype(I32)
    slot = lax.broadcasted_iota(I32, (tm, MOE_CAP), 1).astype(F32)
    ge_b = jnp.broadcast_to(ge, (tm, LANES))
    ge_hi = ge_b.astype(BF16)
    ge_lo = (ge_b - ge_hi.astype(F32)).astype(BF16)

    def chunk(j, _):
        scatter = jnp.where(jnp.logical_and(rk - (j * MOE_CAP).astype(F32) == slot, ge > 0.0), 1.0, 0.0)
        gather = scatter.T.astype(BF16)
        he = _dot(gather, h_s[...]).astype(BF16)
        y = _dot((_silu(_dot(he, w1_ref[...])) * _dot(he, w3_ref[...])).astype(BF16), w2_ref[...])
        gate = _dot(gather, ge_hi) + _dot(gather, ge_lo)
        y = y * jnp.tile(gate, (1, y.shape[1] // LANES))
        acc_s[...] += _dot(scatter.astype(BF16), y.astype(BF16))
        return 0
    lax.fori_loop(0, (routed + MOE_CAP - 1) // MOE_CAP, chunk, 0)

    @pl.when(e == N_EXPERTS - 1)
    def _():
        o_ref[...] = x_ref[...] + g2_ref[...] * acc_s[...]


def _moe(x, sc, sh, g2, ng, rw, rb, w1, w3, w2, *, tm):
    n, d = x.shape
    ne, _, ff = w1.shape
    mod = lambda m: (pl.BlockSpec((1, d), lambda i, e: (0, 0)) if m.shape[0] == 1
                     else pl.BlockSpec((tm, d), lambda i, e: (i, 0)))
    const = lambda shape: pl.BlockSpec(shape, lambda i, e: (0,) * len(shape), pipeline_mode=pl.Buffered(1))
    return pl.pallas_call(
        _moe_kernel,
        grid=(n // tm, ne),
        in_specs=[pl.BlockSpec((tm, d), lambda i, e: (i, 0)), mod(sc), mod(sh), mod(g2), const((1, d)),
                  const(rw.shape), const(rb.shape),
                  pl.BlockSpec((None, d, ff), lambda i, e: (e, 0, 0)),
                  pl.BlockSpec((None, d, ff), lambda i, e: (e, 0, 0)),
                  pl.BlockSpec((None, ff, d), lambda i, e: (e, 0, 0))],
        out_specs=pl.BlockSpec((tm, d), lambda i, e: (i, 0)),
        out_shape=jax.ShapeDtypeStruct((n, d), F32),
        scratch_shapes=[pltpu.VMEM((tm, d), BF16), pltpu.VMEM((tm, LANES), F32), pltpu.VMEM((tm, LANES), F32),
                        pltpu.VMEM((tm, d), F32)],
        compiler_params=_cp(("arbitrary", "arbitrary")),
        name="moe_top2",
    )(x, sc, sh, g2, ng, rw, rb, w1, w3, w2)


def _sidx_kernel(pt_ref, qi_ref, w_ref, *rest, n_pages, group, topk, t_new):
    del pt_ref
    page_refs, (ikn_ref, bias_ref, sc_ref, tot_ref) = rest[:group], rest[group:]
    step = pl.program_id(1)
    rows = t_new

    def page_scores(keys_t):
        r = jnp.maximum(_dot(qi_ref[...], keys_t.astype(BF16)), 0.0) * w_ref[...]
        acc = r[0:rows]
        for h in range(1, N_HEADS):
            acc = acc + r[h * rows:(h + 1) * rows]
        return acc

    for g in range(group):
        sc_ref[step * group + g] = page_scores(page_refs[g][...])

    @pl.when(step == n_pages // group - 1)
    def _():
        row = lax.broadcasted_iota(I32, (rows, PAGE), 0)
        col = lax.broadcasted_iota(I32, (rows, PAGE), 1)
        sc_ref[n_pages] = jnp.where(col <= row, page_scores(ikn_ref[...]), -jnp.inf)

        def counter(cmp):
            def count(thr):
                hit = jnp.where(cmp(sc_ref[...], jnp.broadcast_to(thr, (rows, PAGE))[None]), 1.0, 0.0)
                return jnp.sum(jnp.sum(hit, axis=0), axis=-1, keepdims=True)
            return count

        count_ge, count_gt = counter(lambda s, t: s >= t), counter(lambda s, t: s > t)
        thr = _kth_largest(count_ge, count_gt, (rows, 1), topk)
        need = topk - count_gt(thr)
        n_slots = n_pages + 1
        upper = jnp.where(lax.broadcasted_iota(I32, (PAGE, PAGE), 0) < lax.broadcasted_iota(I32, (PAGE, PAGE), 1),
                          1.0, 0.0).astype(BF16)
        thr_b = jnp.broadcast_to(thr, (rows, PAGE))[None]
        s_all = sc_ref[...]
        eq = jnp.where(s_all == thr_b, 1.0, 0.0).reshape(n_slots * rows, PAGE).astype(BF16)
        within = _dot(eq, upper).reshape(n_slots, rows, PAGE)
        tot_ref[...] = _dot(eq, jnp.ones((PAGE, PAGE), BF16)).reshape(n_slots, rows, PAGE)

        def running(j, before):
            total = tot_ref[j]
            tot_ref[j] = before
            return before + total
        lax.fori_loop(0, n_slots, running, jnp.zeros((rows, PAGE), F32))
        rank = within + tot_ref[...]
        keep = jnp.where(s_all > thr_b, 0.0,
                         jnp.where(s_all == thr_b,
                                   jnp.where(rank < jnp.broadcast_to(need, (rows, PAGE))[None], 0.0, NEG), NEG))
        bias_ref[...] = jnp.where(s_all == -jnp.inf, NEG, keep)


def _page_spec(width, group, g):
    return pl.BlockSpec((None, width, PAGE), lambda b, s, pt: (pt[b, s * group + g], 0, 0))


def _sample_indexer(page_table, qi_rows, w_rows, cache_ik, ki_new, *, topk, t_new, group):
    bd, n_pages = page_table.shape
    hq = qi_rows.shape[1]
    group = min(group, n_pages)
    grid_spec = pltpu.PrefetchScalarGridSpec(
        num_scalar_prefetch=1,
        grid=(bd, n_pages // group),
        in_specs=[pl.BlockSpec((None, hq, HEAD_DIM), lambda b, s, pt: (b, 0, 0)),
                  pl.BlockSpec((None, hq, 1), lambda b, s, pt: (b, 0, 0))]
                 + [_page_spec(HEAD_DIM, group, g) for g in range(group)]
                 + [pl.BlockSpec((None, HEAD_DIM, PAGE), lambda b, s, pt: (b, 0, 0))],
        out_specs=pl.BlockSpec((None, n_pages + 1, t_new, PAGE), lambda b, s, pt: (b, 0, 0, 0)),
        scratch_shapes=[pltpu.VMEM((n_pages + 1, t_new, PAGE), F32)] * 2)
    return pl.pallas_call(
        functools.partial(_sidx_kernel, n_pages=n_pages, group=group, topk=topk, t_new=t_new),
        grid_spec=grid_spec,
        out_shape=jax.ShapeDtypeStruct((bd, n_pages + 1, t_new, PAGE), F32),
        compiler_params=_cp(("arbitrary", "arbitrary")),
        name="sample_indexer",
    )(page_table, qi_rows, w_rows, *([cache_ik] * group), ki_new)


def _sattn_kernel(pt_ref, q_ref, *rest, n_pages, group, t_new):
    del pt_ref
    k_refs, v_refs = rest[:group], rest[group:2 * group]
    kn_ref, vn_ref, mb_ref, tabl_ref, tabn_ref, o_ref, m_s, l_s, acc_s = rest[2 * group:]
    step = pl.program_id(1)
    n_steps = n_pages // group

    @pl.when(step == 0)
    def _():
        m_s[...] = jnp.full(m_s.shape, NEG, F32)
        l_s[...] = jnp.zeros(l_s.shape, F32)
        acc_s[...] = jnp.zeros(acc_s.shape, F32)

    def logits(keys_t, mask_row, table):
        s = _dot(q_ref[...], keys_t) + jnp.concatenate([mb_ref[mask_row]] * N_HEADS, axis=0)
        return s if table is None else s + table

    def update(last):
        ks = [r[...].astype(BF16) for r in k_refs] + ([kn_ref[...].astype(BF16)] if last else [])
        vs = [r[...].astype(BF16) for r in v_refs] + ([vn_ref[...].astype(BF16)] if last else [])
        ss = [logits(ks[g], step * group + g, tabl_ref[...] if (last and g == group - 1) else None)
              for g in range(group)]
        if last:
            ss.append(logits(ks[group], n_pages, tabn_ref[...]))
        m_old = m_s[...]
        m_new = m_old
        for s in ss:
            m_new = jnp.maximum(m_new, jnp.max(s, axis=-1, keepdims=True))
        alpha = jnp.exp(m_old - m_new)
        l_new = alpha * l_s[...]
        acc = alpha * acc_s[...]
        for s, v in zip(ss, vs):
            pexp = jnp.exp(s - m_new)
            l_new = l_new + jnp.sum(pexp, axis=-1, keepdims=True)
            acc = acc + _dot_nt(pexp.astype(BF16), v)
        m_s[...] = m_new
        l_s[...] = l_new
        acc_s[...] = acc

    @pl.when(step < n_steps - 1)
    def _():
        update(False)

    @pl.when(step == n_steps - 1)
    def _():
        update(True)
        r = acc_s[...] / l_s[...]
        lane_head = lax.broadcasted_iota(I32, (t_new, A_W), 1) // HEAD_DIM
        out = jnp.zeros((t_new, A_W), F32)
        for h in range(N_HEADS):
            out = out + jnp.where(lane_head == h, r[h * t_new:(h + 1) * t_new], 0.0)
        o_ref[...] = out.astype(o_ref.dtype)


def _sample_attention(page_table, q_bd, cache_k, cache_v, k_new, v_new, mask, tab_last, tab_new, *, t_new, group):
    bd, n_pages = page_table.shape
    hq = q_bd.shape[1]
    group = min(group, n_pages)
    per_b = lambda b, s, pt: (b, 0, 0)
    const2 = lambda b, s, pt: (0, 0)
    pages = [pl.BlockSpec((None, A_W, PAGE), functools.partial(lambda b, s, pt, g: (pt[b, s * group + g], 0, 0), g=g))
             for g in range(group)]
    grid_spec = pltpu.PrefetchScalarGridSpec(
        num_scalar_prefetch=1,
        grid=(bd, n_pages // group),
        in_specs=[pl.BlockSpec((None, hq, A_W), per_b)] + pages + pages
                 + [pl.BlockSpec((None, A_W, PAGE), per_b), pl.BlockSpec((None, A_W, PAGE), per_b),
                    pl.BlockSpec((None, n_pages + 1, t_new, PAGE), lambda b, s, pt: (b, 0, 0, 0)),
                    pl.BlockSpec((hq, PAGE), const2), pl.BlockSpec((hq, PAGE), const2)],
        out_specs=pl.BlockSpec((None, t_new, A_W), per_b),
        scratch_shapes=[pltpu.VMEM((hq, 1), F32), pltpu.VMEM((hq, 1), F32), pltpu.VMEM((hq, A_W), F32)])
    return pl.pallas_call(
        functools.partial(_sattn_kernel, n_pages=n_pages, group=group, t_new=t_new),
        grid_spec=grid_spec,
        out_shape=jax.ShapeDtypeStruct((bd, t_new, A_W), BF16),
        compiler_params=_cp(("arbitrary", "arbitrary")),
        name="sample_attention",
    )(page_table, q_bd, *([cache_k] * group), *([cache_v] * group), k_new, v_new, mask, tab_last, tab_new)


ATTN_TILE = 256
ATTN_CHUNK = 2048
IDX_PAGE_GROUP = 16
ATTN_PAGE_GROUP = 16
MOE_ROWS = 1024
S5_ROWS = 256
S5_SEGMENTS = 8


def _even_layer_front(x, sc1, sh1, p, *, tm):
    return _inproj(x, sc1, sh1, p['ng0'], p['w_in'], p['gq_pad'], p['gk_tiled'], p['gamma'],
                   tm=tm)


def _odd_layer(x, m, p, h0re, h0im, *, r, seg_len):
    sh1, sc1, g1, sh2, sc2, g2 = m
    zeros = jnp.zeros((S5_LG, r, LANES), F32)
    if seg_len is None:
        lre, lim = zeros, zeros
    else:
        lre, lim = _s5_local(x, sc1, sh1, p['ng1'], p['bb'], p['a3'], r=r, rows=S5_ROWS)
    x, fre, fim = _s5_full(x, sc1, sh1, p['ng1'], g1, p['bb'], p['a3'], p['cre'], p['cim'], p['dskip'],
                           h0re, h0im, lre, lim, p['gluw'], r=r, rows=S5_ROWS, seg_len=seg_len)
    x = _moe(x, sc2, sh2, g2, p['ng1b'], p['rw'], p['rb'], p['mw1'], p['mw3'], p['mw2'],
             tm=min(MOE_ROWS, x.shape[0]))
    return x, _from_lane_groups(fre), _from_lane_groups(fim)


def kernel(x_prompt, x_sample, c_prompt, c_sample, cache_k, cache_v, cache_idx_k, state_hgrn, state_s5_re, state_s5_im, page_table, rel_bias, ada_w, ada_b, norm_g, w_in, qk_norm_g, hgrn_gamma, hgrn_onorm_g, w_out, ffn_w1, ffn_w3, ffn_w2, s5_lambda_re, s5_lambda_im, s5_log_dt, s5_b_re, s5_b_im, s5_c_re, s5_c_im, s5_d, s5_glu_w, moe_router_w, moe_router_b, moe_w1, moe_w3, moe_w2):
    bp, seq, d = x_prompt.shape
    bd, t_new, _ = x_sample.shape
    n_dec = bd * t_new
    n_phys = cache_k.shape[1]
    past = page_table.shape[1] * PAGE
    assert bp == 1 and d == D_MODEL and seq % 512 == 0
    assert n_dec == S5_ROWS and t_new == 8

    g_q, g_k = qk_norm_g[0, 0], qk_norm_g[0, 1]
    ar, ai, bbr_t, bbi_t = _s5_prep(s5_lambda_re[0], s5_lambda_im[0], s5_log_dt[0], s5_b_re[0], s5_b_im[0])
    bb = [_s5_blockdiag_in(bbr_t).astype(BF16), _s5_blockdiag_in(bbi_t).astype(BF16)]
    p = dict(
        ng0=norm_g[0, 0][None], ng0b=norm_g[0, 1][None], ng1=norm_g[1, 0][None], ng1b=norm_g[1, 1][None],
        w_in=_pack_w_in(w_in[0]), gq_pad=jnp.tile(g_q, 2 * N_HEADS)[None], gk_tiled=jnp.tile(g_k, N_HEADS)[None],
        gamma=hgrn_gamma, onorm=jnp.tile(hgrn_onorm_g[0], N_HEADS)[None],
        wa=w_out[0, :A_W].astype(BF16), wb=w_out[0, A_W:].astype(BF16),
        w1=ffn_w1[0].astype(BF16), w3=ffn_w3[0].astype(BF16), w2=ffn_w2[0].astype(BF16),
        bb=bb, a3=[ar.reshape(S5_LG, 1, LANES), ai.reshape(S5_LG, 1, LANES)],
        cre=_s5_blockdiag_out(s5_c_re[0]).astype(BF16), cim=_s5_blockdiag_out(s5_c_im[0]).astype(BF16),
        dskip=s5_d[0][None], gluw=s5_glu_w[0].astype(BF16),
        rw=jnp.pad(moe_router_w[0], ((0, 0), (0, LANES - N_EXPERTS))).astype(BF16),
        rb=jnp.pad(moe_router_b[0], (0, LANES - N_EXPERTS))[None],
        mw1=moe_w1[0].astype(BF16), mw3=moe_w3[0].astype(BF16), mw2=moe_w2[0].astype(BF16),
    )

    c_rows = bp + bd
    c_all = jnp.concatenate([c_prompt, c_sample, jnp.zeros((-c_rows % 8, d), F32)], axis=0)
    mod = _ada(c_all, ada_w.astype(BF16), ada_b)

    def mods(layer, rows, expand):
        m = mod[layer, rows]
        return tuple(expand(m[:, i * d:(i + 1) * d]) for i in range(6))

    xp = x_prompt.reshape(seq, d)
    sh1, sc1, g1, sh2, sc2, g2 = mods(0, slice(0, 1), lambda a: a)
    (q_pad, k_f, k_b, v_f, v_b, qi, kw, gq, gk, gl, gi, gg) = _even_layer_front(
        xp, sc1, sh1, p, tm=512)
    ki_p = kw[:, :HEAD_DIM]
    mask = _prompt_indexer(qi.T, kw[:, HEAD_DIM:HEAD_DIM + N_HEADS].T, ki_p.astype(BF16),
                           tq=ATTN_TILE, tk=ATTN_TILE, topk=min(TOPK_MAX, seq // 4))
    a_out = _prompt_attention(q_pad.T, k_b, v_b.T, mask, rel_bias, tile=ATTN_TILE, chunk=ATTN_CHUNK).T
    b_out, st_p = _gla(gq, gk, gi, gl, gg, jnp.zeros((1, N_HEADS, HEAD_DIM, HEAD_DIM), F32), p['onorm'],
                       batch=1, chunk=64)
    xp = _even_tail(xp, a_out, b_out, g1, sc2, sh2, g2, p['ng0b'], p['wa'], p['wb'], p['w1'], p['w3'], p['w2'],
                    tm=512)
    seg_len = seq // S5_SEGMENTS
    to_seg = lambda a: a.reshape(S5_SEGMENTS, seg_len, d).transpose(1, 0, 2).reshape(seq, d)
    zero_state = jnp.zeros((S5_LG, S5_SEGMENTS, LANES), F32)
    xp, fre_p, fim_p = _odd_layer(to_seg(xp), mods(1, slice(0, 1), lambda a: a), p, zero_state, zero_state,
                                  r=S5_SEGMENTS, seg_len=seg_len)
    y_prompt = xp.reshape(seg_len, S5_SEGMENTS, d).transpose(1, 0, 2).reshape(bp, seq, d)

    xs = x_sample.reshape(n_dec, d)
    per_token = lambda a: jnp.repeat(a, t_new, axis=0)
    sh1, sc1, g1, sh2, sc2, g2 = mods(0, slice(bp, bp + bd), per_token)
    (q_pad_s, k_fs, _, v_fs, _, qi_s, kw_s, gq, gk, gl, gi, gg) = _even_layer_front(
        xs, sc1, sh1, p, tm=n_dec)
    heads_first = lambda a: jnp.transpose(a, (0, 2, 1, 3)).reshape(bd, N_HEADS * t_new, a.shape[-1])
    qi_rows = heads_first(qi_s.reshape(bd, t_new, N_HEADS, HEAD_DIM))
    w_rows = heads_first(kw_s[:, HEAD_DIM:HEAD_DIM + N_HEADS].reshape(bd, t_new, N_HEADS, 1))
    pad_new = lambda a: jnp.swapaxes(jnp.pad(a.reshape(bd, t_new, -1), ((0, 0), (0, PAGE - t_new), (0, 0))), 1, 2)
    ki_s = kw_s[:, :HEAD_DIM]
    mask_s = _sample_indexer(page_table, qi_rows, w_rows, jnp.swapaxes(cache_idx_k[0], 1, 2), pad_new(ki_s),
                             topk=min(TOPK_MAX, (past + t_new) // 4), t_new=t_new, group=IDX_PAGE_GROUP)
    q4 = q_pad_s.reshape(bd, t_new, N_HEADS, LANES)
    even = (jnp.arange(N_HEADS) % 2 == 0)[None, None, :, None]
    q_nat = jnp.where(even, q4[..., :HEAD_DIM], q4[..., HEAD_DIM:])
    q_bd = jnp.einsum('bthd,hg->bhtgd', q_nat, jnp.eye(N_HEADS, dtype=q_nat.dtype))
    q_bd = q_bd.reshape(bd, N_HEADS * t_new, A_W)
    qpos = np.arange(t_new)[:, None]
    col = np.arange(PAGE)[None, :]
    rows_hq = lambda t: t.reshape(N_HEADS * t_new, PAGE)
    tab_last = rows_hq(_bias_by_distance(rel_bias, PAGE + qpos - col))
    tab_new = rows_hq(_bias_by_distance(rel_bias, qpos - col))
    feature_major = lambda c: jnp.transpose(c[0], (0, 2, 3, 1)).reshape(n_phys, A_W, PAGE)
    a_out_s = _sample_attention(page_table, q_bd, feature_major(cache_k), feature_major(cache_v),
                                pad_new(k_fs), pad_new(v_fs), mask_s, tab_last, tab_new,
                                t_new=t_new, group=ATTN_PAGE_GROUP).reshape(n_dec, A_W)
    gla_chunk = 64
    pad_t = lambda a: jnp.pad(a.reshape(bd, t_new, A_W), ((0, 0), (0, gla_chunk - t_new), (0, 0))).reshape(-1, A_W)
    b_out_s, st_s = _gla(pad_t(gq), pad_t(gk), pad_t(gi), pad_t(gl), pad_t(gg),
                         state_hgrn[0], p['onorm'], batch=bd, chunk=gla_chunk)
    b_out_s = b_out_s.reshape(bd, gla_chunk, A_W)[:, :t_new].reshape(n_dec, A_W)
    xs = _even_tail(xs, a_out_s, b_out_s, g1, sc2, sh2, g2, p['ng0b'], p['wa'], p['wb'],
                    p['w1'], p['w3'], p['w2'], tm=n_dec)
    step_major = lambda a: a.reshape(bd, t_new, d).transpose(1, 0, 2).reshape(n_dec, d)
    xs, fre_s, fim_s = _odd_layer(step_major(xs), mods(1, slice(bp, bp + bd), lambda a: jnp.tile(a, (t_new, 1))), p,
                                  _to_lane_groups(state_s5_re[0].reshape(bd, S5_W)),
                                  _to_lane_groups(state_s5_im[0].reshape(bd, S5_W)), r=bd, seg_len=None)
    y_sample = xs.reshape(t_new, bd, d).transpose(1, 0, 2)

    heads = lambda a, b, t: a.reshape(1, b, t, N_HEADS, HEAD_DIM)
    s5_state = lambda f: f.reshape(1, -1, S5_GROUPS, S5_STATE)
    return (y_prompt, y_sample,
            heads(k_f, bp, seq), heads(v_f, bp, seq), ki_p.reshape(1, bp, seq, HEAD_DIM),
            st_p[None], s5_state(fre_p[S5_SEGMENTS - 1:]), s5_state(fim_p[S5_SEGMENTS - 1:]),
            heads(k_fs, bd, t_new), heads(v_fs, bd, t_new), ki_s.reshape(1, bd, t_new, HEAD_DIM),
            st_s[None], s5_state(fre_s), s5_state(fim_s))
```

```python
import functools
import math

import numpy as np
import jax
import jax.numpy as jnp
from jax import lax
from jax.experimental import pallas as pl
from jax.experimental.pallas import tpu as pltpu

F32 = jnp.float32
BF16 = jnp.bfloat16
I32 = jnp.int32

D_MODEL = 1024
N_HEADS = 8
HEAD_DIM = 64
A_W = N_HEADS * HEAD_DIM
LANES = 128
TOPK_MAX = 256
PAGE = 128
REL_BUCKETS = 32
REL_MAX_DIST = 128
S5_GROUPS = 64
S5_CH = 16
S5_STATE = 64
S5_W = S5_GROUPS * S5_STATE
S5_BLK_GROUPS = 8
S5_NBLK = S5_GROUPS // S5_BLK_GROUPS
N_EXPERTS = 8
EPS = 1e-6
NEG = -1e30
INT_MIN = -2 ** 31
VMEM_LIMIT = 56 * 2 ** 20


def _cp(sem, vmem=VMEM_LIMIT):
    return pltpu.CompilerParams(dimension_semantics=sem, vmem_limit_bytes=vmem)


def _resident(shape):
    n = len(shape)
    return pl.BlockSpec(shape, lambda *_: (0,) * n, pipeline_mode=pl.Buffered(1))


def _dot(a, b):
    return jnp.dot(a, b, preferred_element_type=F32)


def _dot_nt(a, b):
    return lax.dot_general(a, b, (((1,), (1,)), ((), ())), preferred_element_type=F32)


def _sigmoid(x):
    return 1.0 / (1.0 + jnp.exp(-x))


def _silu(x):
    return x * _sigmoid(x)


def _gelu_tanh(x):
    return 0.5 * x * (1.0 + jnp.tanh(math.sqrt(2.0 / math.pi) * (x + 0.044715 * (x * x * x))))


def _norm_mod(x, g, sc, sh):
    ms = jnp.mean(x * x, axis=-1, keepdims=True)
    return (x * lax.rsqrt(ms + EPS) * g) * (1.0 + sc) + sh


def _seg_matrix(n, seg, dtype, scale=1.0):
    r = lax.broadcasted_iota(I32, (n, n), 0) // seg
    c = lax.broadcasted_iota(I32, (n, n), 1) // seg
    return jnp.where(r == c, scale, 0.0).astype(dtype)


def _ada_kernel(c_ref, w_ref, b_ref, o_ref):
    c = c_ref[...]
    o_ref[...] = _dot(_silu(c).astype(BF16), w_ref[...]) + b_ref[...]


def _ada(c_all, ada_w, ada_b):
    depth, d, n6 = ada_w.shape
    rows = c_all.shape[0]
    tn = 1536
    return pl.pallas_call(
        _ada_kernel,
        grid=(depth, n6 // tn),
        in_specs=[pl.BlockSpec((rows, d), lambda l, j: (0, 0)),
                  pl.BlockSpec((None, d, tn), lambda l, j: (l, 0, j)),
                  pl.BlockSpec((None, 1, tn), lambda l, j: (l, 0, j))],
        out_specs=pl.BlockSpec((None, rows, tn), lambda l, j: (l, 0, j)),
        out_shape=jax.ShapeDtypeStruct((depth, rows, n6), F32),
        compiler_params=_cp(("arbitrary", "arbitrary")),
        name="ada_mod",
    )(c_all, ada_w, ada_b.reshape(depth, 1, n6))


W_OFF_Q, W_OFF_K, W_OFF_V, W_OFF_QI, W_OFF_KW, W_OFF_B = 0, 1024, 1536, 2048, 2560, 2688


def _pack_w_in(w):
    d = w.shape[0]
    z64 = jnp.zeros((d, N_HEADS, HEAD_DIM), w.dtype)
    q = w[:, 0:512].reshape(d, N_HEADS, HEAD_DIM)
    even = (jnp.arange(N_HEADS) % 2 == 0)[None, :, None]
    q_pad = jnp.concatenate([jnp.where(even, q, z64), jnp.where(even, z64, q)], axis=-1).reshape(d, 1024)
    kw = jnp.concatenate([w[:, 2048:2120], jnp.zeros((d, 56), w.dtype)], axis=-1)
    return jnp.concatenate([q_pad, w[:, 512:2048], kw, w[:, 2120:4168]], axis=-1).astype(BF16)


def _inproj_kernel(x_ref, sc_ref, sh_ref, ng_ref, w_ref, gq_ref, gk_ref, gam_ref,
                   q_ref, kf_ref, kb_ref, vf_ref, vb_ref, qi_ref, kw_ref,
                   gq_o, gk_o, gl_o, gi_o, gg_o):
    h = _norm_mod(x_ref[...], ng_ref[...], sc_ref[...], sh_ref[...]).astype(BF16)
    ones_seg = jnp.full((LANES, LANES), 1.0 / HEAD_DIM, BF16)
    pair_seg = _seg_matrix(LANES, HEAD_DIM, BF16, 1.0 / HEAD_DIM)

    def proj(off, n):
        return _dot(h, w_ref[:, off:off + n])

    pq = proj(W_OFF_Q, 1024)
    for j in range(8):
        blk = pq[:, j * LANES:(j + 1) * LANES]
        ms = _dot((blk * blk).astype(BF16), ones_seg)
        q_ref[:, j * LANES:(j + 1) * LANES] = (
            blk * lax.rsqrt(ms + EPS) * gq_ref[:, j * LANES:(j + 1) * LANES] * 0.125).astype(q_ref.dtype)
    pk = proj(W_OFF_K, 512)
    for j in range(4):
        blk = pk[:, j * LANES:(j + 1) * LANES]
        ms = _dot((blk * blk).astype(BF16), pair_seg)
        kn = blk * lax.rsqrt(ms + EPS) * gk_ref[:, j * LANES:(j + 1) * LANES]
        kf_ref[:, j * LANES:(j + 1) * LANES] = kn
        kb_ref[:, j * LANES:(j + 1) * LANES] = kn.astype(BF16)
    pv = proj(W_OFF_V, 512)
    vf_ref[...] = pv
    vb_ref[...] = pv.astype(BF16)
    qi_ref[...] = (proj(W_OFF_QI, 512) * 0.125).astype(qi_ref.dtype)
    lane = lax.broadcasted_iota(I32, (1, LANES), 1)
    kw_ref[...] = proj(W_OFF_KW, LANES) * jnp.where(lane < HEAD_DIM, 1.0, N_HEADS ** -0.5)
    gq_o[...] = proj(W_OFF_B, 512) * 0.125
    gam = gam_ref[...]
    gmax = jnp.max(gam, axis=0, keepdims=True)
    ge = jnp.exp(gam - gmax)
    lb = ge[0:1, :] / jnp.sum(ge, axis=0, keepdims=True)
    f = lb + (1.0 - lb) * _sigmoid(proj(W_OFF_B + 512, 512))
    gk_o[...] = 1.0 - f
    gl_o[...] = jnp.log(f)
    gi_o[...] = proj(W_OFF_B + 1024, 512)
    gg_o[...] = proj(W_OFF_B + 1536, 512)


def _inproj(x, sc, sh, ng, w_packed, gq_pad, gk_tiled, gamma, *, tm):
    n, d = x.shape
    row = lambda i: (i, 0)
    mod_spec = (pl.BlockSpec((1, d), lambda i: (0, 0)) if sc.shape[0] == 1
                else pl.BlockSpec((tm, d), row))
    outs = [((n, 1024), BF16), ((n, 512), F32), ((n, 512), BF16), ((n, 512), F32), ((n, 512), BF16),
            ((n, 512), BF16), ((n, LANES), F32)] + [((n, 512), F32)] * 5
    return pl.pallas_call(
        _inproj_kernel,
        grid=(n // tm,),
        in_specs=[pl.BlockSpec((tm, d), row), mod_spec, mod_spec,
                  _resident((1, d)), _resident(w_packed.shape), _resident((1, 1024)),
                  _resident((1, 512)), _resident(gamma.shape)],
        out_specs=[pl.BlockSpec((tm, s[1]), row) for s, _ in outs],
        out_shape=[jax.ShapeDtypeStruct(s, dt) for s, dt in outs],
        compiler_params=_cp(("arbitrary",)),
        name="in_proj",
    )(x, sc, sh, ng, w_packed, gq_pad, gk_tiled, gamma)


def _key_to_f32(key):
    neg = key < 0
    mag = jnp.where(neg, -key, key)
    return lax.bitcast_convert_type(jnp.where(neg, mag | jnp.int32(INT_MIN), mag), F32)


def _f32_to_key(x):
    bits = lax.bitcast_convert_type(x, I32)
    return jnp.where(bits < 0, -(bits & jnp.int32(0x7FFFFFFF)), bits)


def _kth_largest(count_ge, count_gt, shape, topk, bounds=None):
    if bounds is None:
        bounds = (jnp.full(shape, -jnp.inf, F32), jnp.full(shape, jnp.inf, F32))

    def cond(state):
        lo, hi, done, _ = state
        return jnp.max(jnp.where(jnp.logical_and(lo != hi, done == 0), 1, 0)) == 1

    def body(state):
        lo, hi, done, thr = state
        mid = (lo | hi) - ((lo ^ hi) >> 1)
        mid_f = _key_to_f32(mid)
        cnt = count_ge(mid_f)
        enough = cnt >= topk
        hit = jnp.logical_and(cnt == topk, done == 0)
        return (jnp.where(enough, mid, lo), jnp.where(enough, hi, mid - 1),
                jnp.where(hit, 1, done), jnp.where(hit, mid_f, thr))

    lo, hi = _f32_to_key(bounds[0]), _f32_to_key(bounds[1])
    zero = jnp.zeros(shape, F32)
    n_pos, n_nonneg = count_gt(zero), count_ge(zero)
    done = jnp.where(jnp.logical_and(n_pos < topk, n_nonneg >= topk), 1, 0)
    lo = jnp.where(n_pos >= topk, jnp.maximum(lo, 1), lo)
    hi = jnp.where(n_nonneg < topk, jnp.minimum(hi, -1), hi)
    lo, _, done, thr = lax.while_loop(cond, body, (lo, hi, done, jnp.zeros(shape, F32)))
    return jnp.where(done == 1, thr, _key_to_f32(lo))


def _pidx_kernel(qit_ref, wt_ref, ki_ref, bias_ref, sc_ref, cm_ref, *, tq, tk, topk, seq):
    assert topk <= tk
    q0 = pl.program_id(0) * tq
    n_kt = (q0 + tq + tk - 1) // tk
    key = lax.broadcasted_iota(I32, (tk, tq), 0)
    qry = lax.broadcasted_iota(I32, (tk, tq), 1) + q0

    def tile_off(kt):
        return pl.multiple_of(kt * tk, tk)

    def scores(kt, _):
        off = tile_off(kt)
        ks = ki_ref[pl.ds(off, tk), :]
        acc = jnp.zeros((tk, tq), F32)
        for h in range(N_HEADS):
            x = _dot(ks, qit_ref[h * HEAD_DIM:(h + 1) * HEAD_DIM, :])
            acc = acc + wt_ref[h:h + 1, :] * jnp.maximum(x, 0.0)
        acc = jnp.where(key + off <= qry, acc, -jnp.inf)
        sc_ref[pl.ds(off, tk), :] = acc
        cm_ref[...] = jnp.maximum(cm_ref[...], acc)
        return 0
    cm_ref[...] = jnp.full((tk, tq), -jnp.inf, F32)
    n_pairs = (n_kt + 1) // 2
    lax.fori_loop(0, 2 * n_pairs, scores, 0)
    bounds = (jnp.min(cm_ref[...], axis=0, keepdims=True), jnp.max(cm_ref[...], axis=0, keepdims=True))

    def counter(cmp):
        def count(thr):
            def body(kp, c):
                off = pl.multiple_of(kp * (2 * tk), 2 * tk)
                hit = jnp.where(cmp(sc_ref[pl.ds(off, 2 * tk), :], thr), 1.0, 0.0)
                return c + jnp.sum(hit.reshape(2 * tk // 64, 8, 8, tq), axis=0)
            c = lax.fori_loop(0, n_pairs, body, jnp.zeros((8, 8, tq), F32))
            return jnp.sum(jnp.sum(c, axis=0), axis=0, keepdims=True)
        return count

    count_ge, count_gt = counter(lambda s, t: s >= t), counter(lambda s, t: s > t)
    thr = _kth_largest(count_ge, count_gt, (1, tq), topk, bounds)
    need = topk - count_gt(thr)
    lower = jnp.where(lax.broadcasted_iota(I32, (tk, tk), 0) > lax.broadcasted_iota(I32, (tk, tk), 1),
                      1.0, 0.0).astype(BF16)

    def select(kt, ties_before):
        off = tile_off(kt)
        s = sc_ref[pl.ds(off, tk), :]
        eq = jnp.where(s == thr, 1.0, 0.0)
        rank = _dot(lower, eq.astype(BF16)) + ties_before
        keep = jnp.where(s > thr, 0.0, jnp.where(s == thr, jnp.where(rank < need, 0.0, NEG), NEG))
        bias_ref[pl.ds(off, tk), :] = jnp.where(key + off <= qry, keep, NEG).astype(BF16)
        return ties_before + jnp.sum(eq, axis=0, keepdims=True)

    def select_all_ties(kt, _):
        off = tile_off(kt)
        keep = jnp.where(sc_ref[pl.ds(off, tk), :] >= thr, 0.0, NEG)
        bias_ref[pl.ds(off, tk), :] = jnp.where(key + off <= qry, keep, NEG).astype(BF16)
        return 0

    surplus = jnp.max(counter(lambda s, t: s == t)(thr) - need) > 0.0

    @pl.when(surplus)
    def _():
        lax.fori_loop(0, n_kt, select, jnp.zeros((1, tq), F32))

    @pl.when(jnp.logical_not(surplus))
    def _():
        lax.fori_loop(0, n_kt, select_all_ties, 0)

    def fill(kt, _):
        bias_ref[pl.ds(tile_off(kt), tk), :] = jnp.full((tk, tq), NEG, BF16)
        return 0
    lax.fori_loop(n_kt, seq // tk, fill, 0)


def _prompt_indexer(qi_t, w_t, ki, *, tq, tk, topk):
    seq = ki.shape[0]
    return pl.pallas_call(
        functools.partial(_pidx_kernel, tq=tq, tk=tk, topk=topk, seq=seq),
        grid=(seq // tq,),
        in_specs=[pl.BlockSpec((A_W, tq), lambda i: (0, i)),
                  pl.BlockSpec((N_HEADS, tq), lambda i: (0, i)),
                  _resident(ki.shape)],
        out_specs=pl.BlockSpec((None, seq, tq), lambda i: (i, 0, 0)),
        out_shape=jax.ShapeDtypeStruct((seq // tq, seq, tq), BF16),
        scratch_shapes=[pltpu.VMEM((seq, tq), F32), pltpu.VMEM((tk, tq), F32)],
        compiler_params=_cp(("arbitrary",)),
        name="prompt_indexer",
    )(qi_t, w_t, ki)


def _t5_bucket_table():
    n = np.arange(REL_MAX_DIST, dtype=np.int64)
    max_exact = REL_BUCKETS // 2
    nf = np.maximum(n, 1).astype(np.float32)
    large = max_exact + (np.log(nf / np.float32(max_exact)) / np.float32(math.log(REL_MAX_DIST / max_exact))
                         * np.float32(REL_BUCKETS - max_exact)).astype(np.int32)
    large = np.minimum(large, REL_BUCKETS - 1)
    return np.where(n < max_exact, n, large).astype(np.int32)


def _bias_by_distance(rel_bias, dist):
    table = _t5_bucket_table()
    bucket = np.where(dist >= REL_MAX_DIST, REL_BUCKETS - 1, table[np.clip(dist, 0, REL_MAX_DIST - 1)])
    b = jnp.moveaxis(rel_bias[bucket], -1, 0)
    far = rel_bias[REL_BUCKETS - 1].reshape((N_HEADS,) + (1,) * dist.ndim)
    return (b - far).astype(F32)


V_ROWS = HEAD_DIM + 16


def _pattn_kernel(qt_ref, k_ref, vt_ref, mask_ref, tab_ref, ot_ref, m_ref, acc_ref, *slots, tile, chunk):
    slot_a, slot_b = slots[:4], slots[4:]
    qb = pl.program_id(0)
    first = pl.program_id(1) * (chunk // tile)
    n_tiles = chunk // tile

    @pl.when(pl.program_id(1) == 0)
    def _():
        m_ref[...] = jnp.full(m_ref.shape, NEG, F32)
        acc_ref[...] = jnp.zeros(acc_ref.shape, F32)

    def scores(j, slot, near):
        s_ref, mx_ref, _, _ = slot
        off = pl.multiple_of(j * tile, tile)
        mb = mask_ref[pl.ds(off, tile), :].astype(F32)
        for h in range(N_HEADS):
            pr = h // 2
            s = _dot(k_ref[pl.ds(off, tile), pr * LANES:(pr + 1) * LANES], qt_ref[h * LANES:(h + 1) * LANES, :])
            s = s + mb
            if near is not None:
                s = s + tab_ref[near, h]
            s_ref[h] = s
            mx_ref[h] = jnp.max(s, axis=0, keepdims=True)

    def accumulate(j, slot):
        s_ref, mx_ref, p_ref, al_ref = slot
        off = pl.multiple_of(j * tile, tile)
        for h in range(N_HEADS):
            m_old = m_ref[h]
            m_new = jnp.maximum(m_old, mx_ref[h])
            alpha = jnp.exp(m_old - m_new)
            p_ref[h] = jnp.exp(s_ref[h] - m_new).astype(BF16)
            al_ref[h] = alpha
            m_ref[h] = m_new
        for h in range(N_HEADS):
            pv = _dot(vt_ref[h * V_ROWS:(h + 1) * V_ROWS, pl.ds(off, tile)], p_ref[h])
            acc_ref[h] = al_ref[h] * acc_ref[h] + pv

    n_far = jnp.clip(qb - 1 - first, 0, n_tiles)
    n_pairs = n_far // 2

    @pl.when(n_pairs > 0)
    def _():
        scores(0, slot_a, None)

    def pair(i, _):
        scores(2 * i + 1, slot_b, None)
        accumulate(2 * i, slot_a)
        scores(jnp.minimum(2 * i + 2, 2 * n_pairs - 1), slot_a, None)
        accumulate(2 * i + 1, slot_b)
        return 0
    lax.fori_loop(0, n_pairs, pair, 0)

    for near, j, run in ((None, n_far - 1, n_far % 2 == 1),
                         (1, qb - 1 - first, jnp.logical_and(qb - 1 - first >= 0, qb - 1 - first < n_tiles)),
                         (0, qb - first, jnp.logical_and(qb - first >= 0, qb - first < n_tiles))):
        @pl.when(run)
        def _():
            scores(j, slot_a, near)
            accumulate(j, slot_a)

    @pl.when(pl.program_id(1) == pl.num_programs(1) - 1)
    def _():
        for h in range(N_HEADS):
            acc = acc_ref[h]
            ot_ref[h * HEAD_DIM:(h + 1) * HEAD_DIM, :] = (
                acc[:HEAD_DIM] / acc[HEAD_DIM:HEAD_DIM + 1]).astype(ot_ref.dtype)


def _near_bias_tables(rel_bias, tile):
    period = 2 * tile
    j = np.arange(period)
    offset = np.where(j < tile, j, j - period)
    w = _bias_by_distance(rel_bias, np.stack([offset, tile + offset]))
    skew = jnp.tile(w, (1, 1, tile))[..., :tile * (period - 1)].reshape(N_HEADS, 2, tile, period - 1)
    return jnp.moveaxis(skew[..., :tile], 0, 1)


def _prompt_attention(q_t, k_bf, v_t, mask, rel_bias, *, tile, chunk):
    seq = k_bf.shape[0]
    chunk = min(chunk, seq)
    tab = _near_bias_tables(rel_bias, tile)
    v_ext = jnp.concatenate([v_t.reshape(N_HEADS, HEAD_DIM, seq),
                             jnp.ones((N_HEADS, V_ROWS - HEAD_DIM, seq), v_t.dtype)], axis=1)
    v_ext = v_ext.reshape(N_HEADS * V_ROWS, seq)
    last_chunk = lambda i: (i * tile + tile - 1) // chunk
    return pl.pallas_call(
        functools.partial(_pattn_kernel, tile=tile, chunk=chunk),
        grid=(seq // tile, seq // chunk),
        in_specs=[pl.BlockSpec((N_HEADS * LANES, tile), lambda i, c: (0, i)),
                  pl.BlockSpec((chunk, A_W), lambda i, c: (jnp.minimum(c, last_chunk(i)), 0)),
                  pl.BlockSpec((N_HEADS * V_ROWS, chunk), lambda i, c: (0, jnp.minimum(c, last_chunk(i)))),
                  pl.BlockSpec((None, chunk, tile), lambda i, c: (i, jnp.minimum(c, last_chunk(i)), 0)),
                  pl.BlockSpec(tab.shape, lambda i, c: (0, 0, 0, 0), pipeline_mode=pl.Buffered(1))],
        out_specs=pl.BlockSpec((A_W, tile), lambda i, c: (0, i)),
        out_shape=jax.ShapeDtypeStruct((A_W, seq), BF16),
        scratch_shapes=[pltpu.VMEM((N_HEADS, 1, tile), F32), pltpu.VMEM((N_HEADS, V_ROWS, tile), F32)]
                       + [pltpu.VMEM((N_HEADS, tile, tile), F32), pltpu.VMEM((N_HEADS, 1, tile), F32),
                          pltpu.VMEM((N_HEADS, tile, tile), BF16), pltpu.VMEM((N_HEADS, 1, tile), F32)] * 2,
        compiler_params=_cp(("arbitrary", "arbitrary")),
        name="prompt_attention",
    )(q_t, k_bf, v_ext, mask, tab)


def _cumsum_rows(x):
    c = x.shape[0]
    tri = jnp.where(lax.broadcasted_iota(I32, (c, c), 0) >= lax.broadcasted_iota(I32, (c, c), 1),
                    1.0, 0.0).astype(BF16)
    hi = x.astype(BF16)
    r1 = x - hi.astype(F32)
    mid = r1.astype(BF16)
    lo = (r1 - mid.astype(F32)).astype(BF16)
    return _dot(tri, hi) + (_dot(tri, mid) + _dot(tri, lo))


def _gla_kernel(q_ref, k_ref, v_ref, g_ref, gate_ref, s0_ref, on_ref, diag_ref, o_ref, sfin_ref,
                st_ref, b_ref, oi_ref, *, chunk):
    @pl.when(pl.program_id(1) == 0)
    def _():
        st_ref[...] = jnp.zeros(st_ref.shape, F32)
        for h in range(N_HEADS):
            st_ref[h * HEAD_DIM:(h + 1) * HEAD_DIM, h * HEAD_DIM:(h + 1) * HEAD_DIM] = s0_ref[h]

    w = q_ref.shape[-1]
    b = _cumsum_rows(g_ref[...])
    b_ref[...] = b
    k = k_ref[...]
    v = v_ref[...]
    seg = _seg_matrix(w, HEAD_DIM, BF16)
    st = st_ref[...]
    o_inter = _dot_nt((q_ref[...] * jnp.exp(b)).astype(BF16), st.astype(BF16))
    for grp in range(chunk // 8):
        n = 8 * (grp + 1)
        rows = lax.broadcasted_iota(I32, (n, w), 0)
        out_rows = []
        for t in range(8 * grp, n):
            dec = jnp.where(rows <= t, jnp.exp(b_ref[t:t + 1, :] - b_ref[0:n, :]), 0.0)
            prod = (q_ref[t:t + 1, :] * dec * k_ref[0:n, :]).astype(BF16)
            out_rows.append(jnp.sum(_dot(prod, seg) * v_ref[0:n, :], axis=0, keepdims=True))
        oi_ref[8 * grp:n, :] = jnp.concatenate(out_rows, axis=0)

    o = o_inter + oi_ref[...]
    ms = _dot((o * o).astype(BF16), seg) * (1.0 / HEAD_DIM)
    o_ref[...] = (o * lax.rsqrt(ms + EPS) * on_ref[...] * _silu(gate_ref[...])).astype(o_ref.dtype)

    b_last = b[chunk - 1:chunk, :]
    kd = (k * jnp.exp(b_last - b)).astype(BF16)
    upd = _dot(v.T.astype(BF16), kd)
    st_new = st * jnp.exp(b_last) + upd * diag_ref[...]
    st_ref[...] = st_new

    @pl.when(pl.program_id(1) == pl.num_programs(1) - 1)
    def _():
        for h in range(N_HEADS):
            sfin_ref[h] = st_new[h * HEAD_DIM:(h + 1) * HEAD_DIM, h * HEAD_DIM:(h + 1) * HEAD_DIM]


def _gla(gq, gk, gv, glog, gate, s0, onorm_tiled, *, batch, chunk):
    n, w = gq.shape
    nc = n // batch // chunk
    row = lambda b, c: (b * nc + c, 0)
    tile = pl.BlockSpec((chunk, w), row)
    state = pl.BlockSpec((None, N_HEADS, HEAD_DIM, HEAD_DIM), lambda b, c: (b, 0, 0, 0))
    same_head = np.equal.outer(np.arange(w) // HEAD_DIM, np.arange(w) // HEAD_DIM).astype(np.float32)
    const = lambda shape: pl.BlockSpec(shape, lambda b, c: (0, 0), pipeline_mode=pl.Buffered(1))
    o, s_fin = pl.pallas_call(
        functools.partial(_gla_kernel, chunk=chunk),
        grid=(batch, nc),
        in_specs=[tile] * 5 + [state, const((1, w)), const((w, w))],
        out_specs=[tile, state],
        out_shape=[jax.ShapeDtypeStruct((n, w), BF16),
                   jax.ShapeDtypeStruct((batch, N_HEADS, HEAD_DIM, HEAD_DIM), F32)],
        scratch_shapes=[pltpu.VMEM((w, w), F32), pltpu.VMEM((chunk, w), F32), pltpu.VMEM((chunk, w), F32)],
        compiler_params=_cp(("arbitrary", "arbitrary")),
        name="hgrn2_gla",
    )(gq, gk, gv, glog, gate, jnp.swapaxes(s0, 2, 3), onorm_tiled, jnp.asarray(same_head, F32))
    return o, jnp.swapaxes(s_fin, 2, 3)


def _even_tail_kernel(x_ref, a_ref, b_ref, g1_ref, sc_ref, sh_ref, g2_ref, ng_ref,
                      wa_ref, wb_ref, w1_ref, w3_ref, w2_ref, o_ref, *, ff_split):
    mix = _dot(a_ref[...], wa_ref[...]) + _dot(b_ref[...], wb_ref[...])
    x1 = x_ref[...] + g1_ref[...] * mix
    h = _norm_mod(x1, ng_ref[...], sc_ref[...], sh_ref[...]).astype(BF16)
    ff = jnp.zeros(x1.shape, F32)
    step = w1_ref.shape[1] // ff_split
    for j in range(ff_split):
        a = _dot(h, w1_ref[:, j * step:(j + 1) * step])
        g = _dot(h, w3_ref[:, j * step:(j + 1) * step])
        ff = ff + _dot((_silu(a) * g).astype(BF16), w2_ref[j * step:(j + 1) * step, :])
    o_ref[...] = x1 + g2_ref[...] * ff


def _mod_spec(m, tm, d):
    return (pl.BlockSpec((1, d), lambda i: (0, 0)) if m.shape[0] == 1
            else pl.BlockSpec((tm, d), lambda i: (i, 0)))


def _even_tail(x, a_out, b_out, g1, sc2, sh2, g2, ng2, wa, wb, w1, w3, w2, *, tm):
    n, d = x.shape
    row = lambda i: (i, 0)
    return pl.pallas_call(
        functools.partial(_even_tail_kernel, ff_split=2),
        grid=(n // tm,),
        in_specs=[pl.BlockSpec((tm, d), row), pl.BlockSpec((tm, A_W), row), pl.BlockSpec((tm, A_W), row),
                  _mod_spec(g1, tm, d), _mod_spec(sc2, tm, d), _mod_spec(sh2, tm, d), _mod_spec(g2, tm, d),
                  _resident((1, d)), _resident(wa.shape), _resident(wb.shape),
                  _resident(w1.shape), _resident(w3.shape), _resident(w2.shape)],
        out_specs=pl.BlockSpec((tm, d), row),
        out_shape=jax.ShapeDtypeStruct((n, d), F32),
        compiler_params=_cp(("arbitrary",)),
        name="even_tail",
    )(x, a_out, b_out, g1, sc2, sh2, g2, ng2, wa, wb, w1, w3, w2)


S5_LG = S5_W // LANES
S5_LG_PER_BLK = S5_LG // S5_NBLK


def _s5_prep_kernel(lr_ref, li_ref, ldt_ref, br_ref, bi_ref, ar_o, ai_o, bbr_o, bbi_o):
    lr = jnp.minimum(lr_ref[...], -1e-4)
    li = li_ref[...]
    dt = jnp.exp(ldt_ref[...])
    mag = jnp.exp(lr * dt)
    a_re = mag * jnp.cos(li * dt)
    a_im = mag * jnp.sin(li * dt)
    den = lr * lr + li * li
    nr = a_re - 1.0
    coef_re = (nr * lr + a_im * li) / den
    coef_im = (a_im * lr - nr * li) / den
    ar_o[...] = a_re
    ai_o[...] = a_im
    br = br_ref[...]
    bi = bi_ref[...]
    bbr_o[...] = coef_re * br - coef_im * bi
    bbi_o[...] = coef_re * bi + coef_im * br


def _s5_prep(lam_re, lam_im, log_dt, b_re, b_im):
    g, p = lam_re.shape
    v3 = lambda a: a.reshape(g, 1, p)
    bt = lambda b: jnp.transpose(b, (0, 2, 1))
    sds = jax.ShapeDtypeStruct
    return pl.pallas_call(
        _s5_prep_kernel,
        out_shape=[sds((g, 1, p), F32), sds((g, 1, p), F32), sds((g, S5_CH, p), F32), sds((g, S5_CH, p), F32)],
        name="s5_prep",
    )(v3(lam_re), v3(lam_im), log_dt.reshape(g, 1, 1), bt(b_re), bt(b_im))


def _s5_blockdiag_in(bb_t):
    x = bb_t.reshape(S5_NBLK, S5_BLK_GROUPS, S5_CH, S5_STATE)
    bd = jnp.einsum('bgcp,gh->bgchp', x, jnp.eye(S5_BLK_GROUPS, dtype=x.dtype))
    return bd.reshape(S5_NBLK, S5_BLK_GROUPS * S5_CH, S5_BLK_GROUPS * S5_STATE)


def _s5_blockdiag_out(c):
    x = c.reshape(S5_NBLK, S5_BLK_GROUPS, S5_CH, S5_STATE)
    bd = jnp.einsum('bgcp,gh->bgphc', x, jnp.eye(S5_BLK_GROUPS, dtype=x.dtype))
    return bd.reshape(S5_NBLK, S5_BLK_GROUPS * S5_STATE, S5_BLK_GROUPS * S5_CH)


def _s5_project_in(u, bbr_ref, bbi_ref, xre_s, xim_s):
    for blk in range(S5_NBLK):
        ub = u[:, blk * LANES:(blk + 1) * LANES].astype(BF16)
        for w_ref, dst in ((bbr_ref, xre_s), (bbi_ref, xim_s)):
            r = _dot(ub, w_ref[blk])
            for j in range(S5_LG_PER_BLK):
                dst[blk * S5_LG_PER_BLK + j] = r[:, j * LANES:(j + 1) * LANES]


def _s5_scan(xre_s, xim_s, st_re, st_im, ar_ref, ai_ref, *, n_steps, r, store):
    def blk_body(blk, _):
        base = blk * S5_LG_PER_BLK
        ar = [jnp.broadcast_to(ar_ref[base + j], (r, LANES)) for j in range(S5_LG_PER_BLK)]
        ai = [jnp.broadcast_to(ai_ref[base + j], (r, LANES)) for j in range(S5_LG_PER_BLK)]

        def step(s, carry):
            r0 = pl.multiple_of(s * r, r)
            new = []
            for j in range(S5_LG_PER_BLK):
                sr, si = carry[2 * j], carry[2 * j + 1]
                nr = ar[j] * sr - ai[j] * si + xre_s[base + j, pl.ds(r0, r), :]
                ni = ar[j] * si + ai[j] * sr + xim_s[base + j, pl.ds(r0, r), :]
                if store:
                    xre_s[base + j, pl.ds(r0, r), :] = nr
                    xim_s[base + j, pl.ds(r0, r), :] = ni
                new += [nr, ni]
            return tuple(new)

        init = []
        for j in range(S5_LG_PER_BLK):
            init += [st_re[base + j], st_im[base + j]]
        fin = lax.fori_loop(0, n_steps, step, tuple(init))
        for j in range(S5_LG_PER_BLK):
            st_re[base + j] = fin[2 * j]
            st_im[base + j] = fin[2 * j + 1]
        return 0
    lax.fori_loop(0, S5_NBLK, blk_body, 0)


def _s5_local_kernel(x_ref, sc_ref, sh_ref, ng_ref, bbr_ref, bbi_ref, ar_ref, ai_ref,
                     lre_o, lim_o, xre_s, xim_s, st_re, st_im, *, n_steps, r):
    @pl.when(pl.program_id(0) == 0)
    def _():
        st_re[...] = jnp.zeros(st_re.shape, F32)
        st_im[...] = jnp.zeros(st_im.shape, F32)
    u = _norm_mod(x_ref[...], ng_ref[...], sc_ref[...], sh_ref[...])
    _s5_project_in(u, bbr_ref, bbi_ref, xre_s, xim_s)
    _s5_scan(xre_s, xim_s, st_re, st_im, ar_ref, ai_ref, n_steps=n_steps, r=r, store=False)
    lre_o[...] = st_re[...]
    lim_o[...] = st_im[...]


def _cpow(ar, ai, n):
    rr, ri = None, None
    br, bi = ar, ai
    while n:
        if n & 1:
            if rr is None:
                rr, ri = br, bi
            else:
                rr, ri = rr * br - ri * bi, rr * bi + ri * br
        n >>= 1
        if n:
            br, bi = br * br - bi * bi, 2.0 * br * bi
    return rr, ri


def _s5_full_kernel(x_ref, sc_ref, sh_ref, ng_ref, g1_ref, bbr_ref, bbi_ref, ar_ref, ai_ref,
                    cre_ref, cim_ref, d_ref, h0re_ref, h0im_ref, lre_ref, lim_ref, gluw_ref,
                    o_ref, fre_o, fim_o, xre_s, xim_s, st_re, st_im, *, n_steps, r, seg_len):
    @pl.when(pl.program_id(0) == 0)
    def _():
        if seg_len is None:
            st_re[...] = h0re_ref[...]
            st_im[...] = h0im_ref[...]
        else:
            pr, pi = _cpow(ar_ref[...], ai_ref[...], seg_len)
            sr = h0re_ref[:, 0:1, :]
            si = h0im_ref[:, 0:1, :]
            st_re[:, 0:1, :] = sr
            st_im[:, 0:1, :] = si
            for row in range(1, r):
                sr, si = (lre_ref[:, row - 1:row, :] + (pr * sr - pi * si),
                          lim_ref[:, row - 1:row, :] + (pr * si + pi * sr))
                st_re[:, row:row + 1, :] = sr
                st_im[:, row:row + 1, :] = si

    x = x_ref[...]
    u = _norm_mod(x, ng_ref[...], sc_ref[...], sh_ref[...])
    _s5_project_in(u, bbr_ref, bbi_ref, xre_s, xim_s)
    _s5_scan(xre_s, xim_s, st_re, st_im, ar_ref, ai_ref, n_steps=n_steps, r=r, store=True)
    fre_o[...] = st_re[...]
    fim_o[...] = st_im[...]
    ys = []
    for blk in range(S5_NBLK):
        acc = None
        for j in range(S5_LG_PER_BLK):
            lg = blk * S5_LG_PER_BLK + j
            t = (_dot(xre_s[lg].astype(BF16), cre_ref[blk, j * LANES:(j + 1) * LANES, :])
                 - _dot(xim_s[lg].astype(BF16), cim_ref[blk, j * LANES:(j + 1) * LANES, :]))
            acc = t if acc is None else acc + t
        ys.append(acc)
    y = jnp.concatenate(ys, axis=-1) + d_ref[...] * u
    z = _dot(_gelu_tanh(y).astype(BF16), gluw_ref[...])
    dm = x.shape[-1]
    mix = z[:, :dm] * _sigmoid(z[:, dm:])
    o_ref[...] = x + g1_ref[...] * mix


def _s5_weight_specs(ws):
    return [_resident(w.shape) for w in ws]


def _s5_local(x, sc, sh, ng, bb, a3, *, r, rows):
    n, d = x.shape
    n_steps = rows // r
    sds = jax.ShapeDtypeStruct((S5_LG, r, LANES), F32)
    return pl.pallas_call(
        functools.partial(_s5_local_kernel, n_steps=n_steps, r=r),
        grid=(n // rows,),
        in_specs=[pl.BlockSpec((rows, d), lambda i: (i, 0)), _mod_spec(sc, rows, d), _mod_spec(sh, rows, d),
                  _resident((1, d))] + _s5_weight_specs(bb + a3),
        out_specs=[_resident(sds.shape)] * 2,
        out_shape=[sds, sds],
        scratch_shapes=[pltpu.VMEM((S5_LG, rows, LANES), F32)] * 2 + [pltpu.VMEM((S5_LG, r, LANES), F32)] * 2,
        compiler_params=_cp(("arbitrary",)),
        name="s5_local_scan",
    )(x, sc, sh, ng, *bb, *a3)


def _s5_full(x, sc, sh, ng, g1, bb, a3, cre, cim, dskip, h0re, h0im, lre, lim, gluw, *, r, rows, seg_len):
    n, d = x.shape
    n_steps = rows // r
    st = jax.ShapeDtypeStruct((S5_LG, r, LANES), F32)
    return pl.pallas_call(
        functools.partial(_s5_full_kernel, n_steps=n_steps, r=r, seg_len=seg_len),
        grid=(n // rows,),
        in_specs=[pl.BlockSpec((rows, d), lambda i: (i, 0)), _mod_spec(sc, rows, d), _mod_spec(sh, rows, d),
                  _resident((1, d)), _mod_spec(g1, rows, d)]
                 + _s5_weight_specs(bb + a3 + [cre, cim, dskip, h0re, h0im, lre, lim, gluw]),
        out_specs=[pl.BlockSpec((rows, d), lambda i: (i, 0)), _resident(st.shape), _resident(st.shape)],
        out_shape=[jax.ShapeDtypeStruct((n, d), F32), st, st],
        scratch_shapes=[pltpu.VMEM((S5_LG, rows, LANES), F32)] * 2 + [pltpu.VMEM((S5_LG, r, LANES), F32)] * 2,
        compiler_params=_cp(("arbitrary",)),
        name="s5_scan_glu",
    )(x, sc, sh, ng, g1, *bb, *a3, cre, cim, dskip, h0re, h0im, lre, lim, gluw)


def _to_lane_groups(s):
    r = s.shape[0]
    return jnp.transpose(s.reshape(r, S5_LG, LANES), (1, 0, 2))


def _from_lane_groups(s):
    r = s.shape[1]
    return jnp.transpose(s, (1, 0, 2)).reshape(r, S5_W)


MOE_CAP = 128


def _moe_kernel(x_ref, sc_ref, sh_ref, g2_ref, ng_ref, rw_ref, rb_ref, w1_ref, w3_ref, w2_ref,
                o_ref, h_s, gate_s, rank_s, acc_s):
    e = pl.program_id(1)
    tm = gate_s.shape[0]
    lane = lax.broadcasted_iota(I32, gate_s.shape, 1)

    @pl.when(e == 0)
    def _():
        h = _norm_mod(x_ref[...], ng_ref[...], sc_ref[...], sh_ref[...])
        h_s[...] = h.astype(BF16)
        logits = _dot(h_s[...], rw_ref[...]) + rb_ref[...]
        logits = jnp.where(lane < N_EXPERTS, logits, -jnp.inf)
        m1 = jnp.max(logits, axis=-1, keepdims=True)
        i1 = jnp.min(jnp.where(logits == m1, lane, LANES), axis=-1, keepdims=True)
        rest = jnp.where(lane == i1, -jnp.inf, logits)
        m2 = jnp.max(rest, axis=-1, keepdims=True)
        i2 = jnp.min(jnp.where(rest == m2, lane, LANES), axis=-1, keepdims=True)
        e2 = jnp.exp(m2 - m1)
        den = 1.0 + e2
        gates = jnp.where(lane == i1, 1.0 / den, 0.0) + jnp.where(lane == i2, e2 / den, 0.0)
        gate_s[...] = gates
        earlier = jnp.where(lax.broadcasted_iota(I32, (tm, tm), 0) > lax.broadcasted_iota(I32, (tm, tm), 1),
                            1.0, 0.0).astype(BF16)
        rank_s[...] = _dot(earlier, jnp.where(gates > 0.0, 1.0, 0.0).astype(BF16))
        acc_s[...] = jnp.zeros(acc_s.shape, F32)

    ge = jnp.sum(jnp.where(lane == e, gate_s[...], 0.0), axis=-1, keepdims=True)
    rk = jnp.sum(jnp.where(lane == e, rank_s[...], 0.0), axis=-1, keepdims=True)
    routed = jnp.sum(jnp.where(ge > 0.0, 1.0, 0.0)).astype(I32)
    slot = lax.broadcasted_iota(I32, (tm, MOE_CAP), 1).astype(F32)
    ge_b = jnp.broadcast_to(ge, (tm, LANES))
    ge_hi = ge_b.astype(BF16)
    ge_lo = (ge_b - ge_hi.astype(F32)).astype(BF16)

    def chunk(j, _):
        scatter = jnp.where(jnp.logical_and(rk - (j * MOE_CAP).astype(F32) == slot, ge > 0.0), 1.0, 0.0)
        gather = scatter.T.astype(BF16)
        he = _dot(gather, h_s[...]).astype(BF16)
        y = _dot((_silu(_dot(he, w1_ref[...])) * _dot(he, w3_ref[...])).astype(BF16), w2_ref[...])
        gate = _dot(gather, ge_hi) + _dot(gather, ge_lo)
        y = y * jnp.tile(gate, (1, y.shape[1] // LANES))
        acc_s[...] += _dot(scatter.astype(BF16), y.astype(BF16))
        return 0
    lax.fori_loop(0, (routed + MOE_CAP - 1) // MOE_CAP, chunk, 0)

    @pl.when(e == N_EXPERTS - 1)
    def _():
        o_ref[...] = x_ref[...] + g2_ref[...] * acc_s[...]


def _moe(x, sc, sh, g2, ng, rw, rb, w1, w3, w2, *, tm):
    n, d = x.shape
    ne, _, ff = w1.shape
    mod = lambda m: (pl.BlockSpec((1, d), lambda i, e: (0, 0)) if m.shape[0] == 1
                     else pl.BlockSpec((tm, d), lambda i, e: (i, 0)))
    const = lambda shape: pl.BlockSpec(shape, lambda i, e: (0,) * len(shape), pipeline_mode=pl.Buffered(1))
    return pl.pallas_call(
        _moe_kernel,
        grid=(n // tm, ne),
        in_specs=[pl.BlockSpec((tm, d), lambda i, e: (i, 0)), mod(sc), mod(sh), mod(g2), const((1, d)),
                  const(rw.shape), const(rb.shape),
                  pl.BlockSpec((None, d, ff), lambda i, e: (e, 0, 0)),
                  pl.BlockSpec((None, d, ff), lambda i, e: (e, 0, 0)),
                  pl.BlockSpec((None, ff, d), lambda i, e: (e, 0, 0))],
        out_specs=pl.BlockSpec((tm, d), lambda i, e: (i, 0)),
        out_shape=jax.ShapeDtypeStruct((n, d), F32),
        scratch_shapes=[pltpu.VMEM((tm, d), BF16), pltpu.VMEM((tm, LANES), F32), pltpu.VMEM((tm, LANES), F32),
                        pltpu.VMEM((tm, d), F32)],
        compiler_params=_cp(("arbitrary", "arbitrary")),
        name="moe_top2",
    )(x, sc, sh, g2, ng, rw, rb, w1, w3, w2)


def _sidx_kernel(pt_ref, qi_ref, w_ref, *rest, n_pages, group, topk, t_new):
    del pt_ref
    page_refs, (ikn_ref, bias_ref, sc_ref, tot_ref) = rest[:group], rest[group:]
    step = pl.program_id(1)
    rows = t_new

    def page_scores(keys_t):
        r = jnp.maximum(_dot(qi_ref[...], keys_t.astype(BF16)), 0.0) * w_ref[...]
        acc = r[0:rows]
        for h in range(1, N_HEADS):
            acc = acc + r[h * rows:(h + 1) * rows]
        return acc

    for g in range(group):
        sc_ref[step * group + g] = page_scores(page_refs[g][...])

    @pl.when(step == n_pages // group - 1)
    def _():
        row = lax.broadcasted_iota(I32, (rows, PAGE), 0)
        col = lax.broadcasted_iota(I32, (rows, PAGE), 1)
        sc_ref[n_pages] = jnp.where(col <= row, page_scores(ikn_ref[...]), -jnp.inf)

        def counter(cmp):
            def count(thr):
                hit = jnp.where(cmp(sc_ref[...], jnp.broadcast_to(thr, (rows, PAGE))[None]), 1.0, 0.0)
                return jnp.sum(jnp.sum(hit, axis=0), axis=-1, keepdims=True)
            return count

        count_ge, count_gt = counter(lambda s, t: s >= t), counter(lambda s, t: s > t)
        thr = _kth_largest(count_ge, count_gt, (rows, 1), topk)
        need = topk - count_gt(thr)
        n_slots = n_pages + 1
        upper = jnp.where(lax.broadcasted_iota(I32, (PAGE, PAGE), 0) < lax.broadcasted_iota(I32, (PAGE, PAGE), 1),
                          1.0, 0.0).astype(BF16)
        thr_b = jnp.broadcast_to(thr, (rows, PAGE))[None]
        s_all = sc_ref[...]
        eq = jnp.where(s_all == thr_b, 1.0, 0.0).reshape(n_slots * rows, PAGE).astype(BF16)
        within = _dot(eq, upper).reshape(n_slots, rows, PAGE)
        tot_ref[...] = _dot(eq, jnp.ones((PAGE, PAGE), BF16)).reshape(n_slots, rows, PAGE)

        def running(j, before):
            total = tot_ref[j]
            tot_ref[j] = before
            return before + total
        lax.fori_loop(0, n_slots, running, jnp.zeros((rows, PAGE), F32))
        rank = within + tot_ref[...]
        keep = jnp.where(s_all > thr_b, 0.0,
                         jnp.where(s_all == thr_b,
                                   jnp.where(rank < jnp.broadcast_to(need, (rows, PAGE))[None], 0.0, NEG), NEG))
        bias_ref[...] = jnp.where(s_all == -jnp.inf, NEG, keep)


def _page_spec(width, group, g):
    return pl.BlockSpec((None, width, PAGE), lambda b, s, pt: (pt[b, s * group + g], 0, 0))


def _sample_indexer(page_table, qi_rows, w_rows, cache_ik, ki_new, *, topk, t_new, group):
    bd, n_pages = page_table.shape
    hq = qi_rows.shape[1]
    group = min(group, n_pages)
    grid_spec = pltpu.PrefetchScalarGridSpec(
        num_scalar_prefetch=1,
        grid=(bd, n_pages // group),
        in_specs=[pl.BlockSpec((None, hq, HEAD_DIM), lambda b, s, pt: (b, 0, 0)),
                  pl.BlockSpec((None, hq, 1), lambda b, s, pt: (b, 0, 0))]
                 + [_page_spec(HEAD_DIM, group, g) for g in range(group)]
                 + [pl.BlockSpec((None, HEAD_DIM, PAGE), lambda b, s, pt: (b, 0, 0))],
        out_specs=pl.BlockSpec((None, n_pages + 1, t_new, PAGE), lambda b, s, pt: (b, 0, 0, 0)),
        scratch_shapes=[pltpu.VMEM((n_pages + 1, t_new, PAGE), F32)] * 2)
    return pl.pallas_call(
        functools.partial(_sidx_kernel, n_pages=n_pages, group=group, topk=topk, t_new=t_new),
        grid_spec=grid_spec,
        out_shape=jax.ShapeDtypeStruct((bd, n_pages + 1, t_new, PAGE), F32),
        compiler_params=_cp(("arbitrary", "arbitrary")),
        name="sample_indexer",
    )(page_table, qi_rows, w_rows, *([cache_ik] * group), ki_new)


def _sattn_kernel(pt_ref, q_ref, *rest, n_pages, group, t_new):
    del pt_ref
    k_refs, v_refs = rest[:group], rest[group:2 * group]
    kn_ref, vn_ref, mb_ref, tabl_ref, tabn_ref, o_ref, m_s, l_s, acc_s = rest[2 * group:]
    step = pl.program_id(1)
    n_steps = n_pages // group

    @pl.when(step == 0)
    def _():
        m_s[...] = jnp.full(m_s.shape, NEG, F32)
        l_s[...] = jnp.zeros(l_s.shape, F32)
        acc_s[...] = jnp.zeros(acc_s.shape, F32)

    def logits(keys_t, mask_row, table):
        s = _dot(q_ref[...], keys_t) + jnp.concatenate([mb_ref[mask_row]] * N_HEADS, axis=0)
        return s if table is None else s + table

    def update(last):
        ks = [r[...].astype(BF16) for r in k_refs] + ([kn_ref[...].astype(BF16)] if last else [])
        vs = [r[...].astype(BF16) for r in v_refs] + ([vn_ref[...].astype(BF16)] if last else [])
        ss = [logits(ks[g], step * group + g, tabl_ref[...] if (last and g == group - 1) else None)
              for g in range(group)]
        if last:
            ss.append(logits(ks[group], n_pages, tabn_ref[...]))
        m_old = m_s[...]
        m_new = m_old
        for s in ss:
            m_new = jnp.maximum(m_new, jnp.max(s, axis=-1, keepdims=True))
        alpha = jnp.exp(m_old - m_new)
        l_new = alpha * l_s[...]
        acc = alpha * acc_s[...]
        for s, v in zip(ss, vs):
            pexp = jnp.exp(s - m_new)
            l_new = l_new + jnp.sum(pexp, axis=-1, keepdims=True)
            acc = acc + _dot_nt(pexp.astype(BF16), v)
        m_s[...] = m_new
        l_s[...] = l_new
        acc_s[...] = acc

    @pl.when(step < n_steps - 1)
    def _():
        update(False)

    @pl.when(step == n_steps - 1)
    def _():
        update(True)
        r = acc_s[...] / l_s[...]
        lane_head = lax.broadcasted_iota(I32, (t_new, A_W), 1) // HEAD_DIM
        out = jnp.zeros((t_new, A_W), F32)
        for h in range(N_HEADS):
            out = out + jnp.where(lane_head == h, r[h * t_new:(h + 1) * t_new], 0.0)
        o_ref[...] = out.astype(o_ref.dtype)


def _sample_attention(page_table, q_bd, cache_k, cache_v, k_new, v_new, mask, tab_last, tab_new, *, t_new, group):
    bd, n_pages = page_table.shape
    hq = q_bd.shape[1]
    group = min(group, n_pages)
    per_b = lambda b, s, pt: (b, 0, 0)
    const2 = lambda b, s, pt: (0, 0)
    pages = [pl.BlockSpec((None, A_W, PAGE), functools.partial(lambda b, s, pt, g: (pt[b, s * group + g], 0, 0), g=g))
             for g in range(group)]
    grid_spec = pltpu.PrefetchScalarGridSpec(
        num_scalar_prefetch=1,
        grid=(bd, n_pages // group),
        in_specs=[pl.BlockSpec((None, hq, A_W), per_b)] + pages + pages
                 + [pl.BlockSpec((None, A_W, PAGE), per_b), pl.BlockSpec((None, A_W, PAGE), per_b),
                    pl.BlockSpec((None, n_pages + 1, t_new, PAGE), lambda b, s, pt: (b, 0, 0, 0)),
                    pl.BlockSpec((hq, PAGE), const2), pl.BlockSpec((hq, PAGE), const2)],
        out_specs=pl.BlockSpec((None, t_new, A_W), per_b),
        scratch_shapes=[pltpu.VMEM((hq, 1), F32), pltpu.VMEM((hq, 1), F32), pltpu.VMEM((hq, A_W), F32)])
    return pl.pallas_call(
        functools.partial(_sattn_kernel, n_pages=n_pages, group=group, t_new=t_new),
        grid_spec=grid_spec,
        out_shape=jax.ShapeDtypeStruct((bd, t_new, A_W), BF16),
        compiler_params=_cp(("arbitrary", "arbitrary")),
        name="sample_attention",
    )(page_table, q_bd, *([cache_k] * group), *([cache_v] * group), k_new, v_new, mask, tab_last, tab_new)


ATTN_TILE = 256
ATTN_CHUNK = 4096
IDX_PAGE_GROUP = 16
ATTN_PAGE_GROUP = 16
MOE_ROWS = 1024
S5_ROWS = 512
S5_SEGMENTS = 8


def _even_layer_front(x, sc1, sh1, p, *, tm):
    return _inproj(x, sc1, sh1, p['ng0'], p['w_in'], p['gq_pad'], p['gk_tiled'], p['gamma'],
                   tm=tm)


def _odd_layer(x, m, p, h0re, h0im, *, r, seg_len):
    sh1, sc1, g1, sh2, sc2, g2 = m
    zeros = jnp.zeros((S5_LG, r, LANES), F32)
    rows = min(S5_ROWS, x.shape[0])
    if seg_len is None:
        lre, lim = zeros, zeros
    else:
        lre, lim = _s5_local(x, sc1, sh1, p['ng1'], p['bb'], p['a3'], r=r, rows=rows)
    x, fre, fim = _s5_full(x, sc1, sh1, p['ng1'], g1, p['bb'], p['a3'], p['cre'], p['cim'], p['dskip'],
                           h0re, h0im, lre, lim, p['gluw'], r=r, rows=rows, seg_len=seg_len)
    x = _moe(x, sc2, sh2, g2, p['ng1b'], p['rw'], p['rb'], p['mw1'], p['mw3'], p['mw2'],
             tm=min(MOE_ROWS, x.shape[0]))
    return x, _from_lane_groups(fre), _from_lane_groups(fim)


def kernel(x_prompt, x_sample, c_prompt, c_sample, cache_k, cache_v, cache_idx_k, state_hgrn, state_s5_re, state_s5_im, page_table, rel_bias, ada_w, ada_b, norm_g, w_in, qk_norm_g, hgrn_gamma, hgrn_onorm_g, w_out, ffn_w1, ffn_w3, ffn_w2, s5_lambda_re, s5_lambda_im, s5_log_dt, s5_b_re, s5_b_im, s5_c_re, s5_c_im, s5_d, s5_glu_w, moe_router_w, moe_router_b, moe_w1, moe_w3, moe_w2):
    bp, seq, d = x_prompt.shape
    bd, t_new, _ = x_sample.shape
    n_dec = bd * t_new
    n_phys = cache_k.shape[1]
    past = page_table.shape[1] * PAGE
    assert bp == 1 and d == D_MODEL and seq % 512 == 0
    assert t_new == 8

    g_q, g_k = qk_norm_g[0, 0], qk_norm_g[0, 1]
    ar, ai, bbr_t, bbi_t = _s5_prep(s5_lambda_re[0], s5_lambda_im[0], s5_log_dt[0], s5_b_re[0], s5_b_im[0])
    bb = [_s5_blockdiag_in(bbr_t).astype(BF16), _s5_blockdiag_in(bbi_t).astype(BF16)]
    p = dict(
        ng0=norm_g[0, 0][None], ng0b=norm_g[0, 1][None], ng1=norm_g[1, 0][None], ng1b=norm_g[1, 1][None],
        w_in=_pack_w_in(w_in[0]), gq_pad=jnp.tile(g_q, 2 * N_HEADS)[None], gk_tiled=jnp.tile(g_k, N_HEADS)[None],
        gamma=hgrn_gamma, onorm=jnp.tile(hgrn_onorm_g[0], N_HEADS)[None],
        wa=w_out[0, :A_W].astype(BF16), wb=w_out[0, A_W:].astype(BF16),
        w1=ffn_w1[0].astype(BF16), w3=ffn_w3[0].astype(BF16), w2=ffn_w2[0].astype(BF16),
        bb=bb, a3=[ar.reshape(S5_LG, 1, LANES), ai.reshape(S5_LG, 1, LANES)],
        cre=_s5_blockdiag_out(s5_c_re[0]).astype(BF16), cim=_s5_blockdiag_out(s5_c_im[0]).astype(BF16),
        dskip=s5_d[0][None], gluw=s5_glu_w[0].astype(BF16),
        rw=jnp.pad(moe_router_w[0], ((0, 0), (0, LANES - N_EXPERTS))).astype(BF16),
        rb=jnp.pad(moe_router_b[0], (0, LANES - N_EXPERTS))[None],
        mw1=moe_w1[0].astype(BF16), mw3=moe_w3[0].astype(BF16), mw2=moe_w2[0].astype(BF16),
    )

    c_rows = bp + bd
    c_all = jnp.concatenate([c_prompt, c_sample, jnp.zeros((-c_rows % 8, d), F32)], axis=0)
    mod = _ada(c_all, ada_w.astype(BF16), ada_b)

    def mods(layer, rows, expand):
        m = mod[layer, rows]
        return tuple(expand(m[:, i * d:(i + 1) * d]) for i in range(6))

    xp = x_prompt.reshape(seq, d)
    sh1, sc1, g1, sh2, sc2, g2 = mods(0, slice(0, 1), lambda a: a)
    (q_pad, k_f, k_b, v_f, v_b, qi, kw, gq, gk, gl, gi, gg) = _even_layer_front(
        xp, sc1, sh1, p, tm=512)
    ki_p = kw[:, :HEAD_DIM]
    mask = _prompt_indexer(qi.T, kw[:, HEAD_DIM:HEAD_DIM + N_HEADS].T, ki_p.astype(BF16),
                           tq=ATTN_TILE, tk=ATTN_TILE, topk=min(TOPK_MAX, seq // 4))
    a_out = _prompt_attention(q_pad.T, k_b, v_b.T, mask, rel_bias, tile=ATTN_TILE, chunk=ATTN_CHUNK).T
    b_out, st_p = _gla(gq, gk, gi, gl, gg, jnp.zeros((1, N_HEADS, HEAD_DIM, HEAD_DIM), F32), p['onorm'],
                       batch=1, chunk=64)
    xp = _even_tail(xp, a_out, b_out, g1, sc2, sh2, g2, p['ng0b'], p['wa'], p['wb'], p['w1'], p['w3'], p['w2'],
                    tm=512)
    seg_len = seq // S5_SEGMENTS
    to_seg = lambda a: a.reshape(S5_SEGMENTS, seg_len, d).transpose(1, 0, 2).reshape(seq, d)
    zero_state = jnp.zeros((S5_LG, S5_SEGMENTS, LANES), F32)
    xp, fre_p, fim_p = _odd_layer(to_seg(xp), mods(1, slice(0, 1), lambda a: a), p, zero_state, zero_state,
                                  r=S5_SEGMENTS, seg_len=seg_len)
    y_prompt = xp.reshape(seg_len, S5_SEGMENTS, d).transpose(1, 0, 2).reshape(bp, seq, d)

    xs = x_sample.reshape(n_dec, d)
    per_token = lambda a: jnp.repeat(a, t_new, axis=0)
    sh1, sc1, g1, sh2, sc2, g2 = mods(0, slice(bp, bp + bd), per_token)
    (q_pad_s, k_fs, _, v_fs, _, qi_s, kw_s, gq, gk, gl, gi, gg) = _even_layer_front(
        xs, sc1, sh1, p, tm=n_dec)
    heads_first = lambda a: jnp.transpose(a, (0, 2, 1, 3)).reshape(bd, N_HEADS * t_new, a.shape[-1])
    qi_rows = heads_first(qi_s.reshape(bd, t_new, N_HEADS, HEAD_DIM))
    w_rows = heads_first(kw_s[:, HEAD_DIM:HEAD_DIM + N_HEADS].reshape(bd, t_new, N_HEADS, 1))
    pad_new = lambda a: jnp.swapaxes(jnp.pad(a.reshape(bd, t_new, -1), ((0, 0), (0, PAGE - t_new), (0, 0))), 1, 2)
    ki_s = kw_s[:, :HEAD_DIM]
    mask_s = _sample_indexer(page_table, qi_rows, w_rows, jnp.swapaxes(cache_idx_k[0], 1, 2), pad_new(ki_s),
                             topk=min(TOPK_MAX, (past + t_new) // 4), t_new=t_new, group=IDX_PAGE_GROUP)
    q4 = q_pad_s.reshape(bd, t_new, N_HEADS, LANES)
    even = (jnp.arange(N_HEADS) % 2 == 0)[None, None, :, None]
    q_nat = jnp.where(even, q4[..., :HEAD_DIM], q4[..., HEAD_DIM:])
    q_bd = jnp.einsum('bthd,hg->bhtgd', q_nat, jnp.eye(N_HEADS, dtype=q_nat.dtype))
    q_bd = q_bd.reshape(bd, N_HEADS * t_new, A_W)
    qpos = np.arange(t_new)[:, None]
    col = np.arange(PAGE)[None, :]
    rows_hq = lambda t: t.reshape(N_HEADS * t_new, PAGE)
    tab_last = rows_hq(_bias_by_distance(rel_bias, PAGE + qpos - col))
    tab_new = rows_hq(_bias_by_distance(rel_bias, qpos - col))
    feature_major = lambda c: jnp.transpose(c[0], (0, 2, 3, 1)).reshape(n_phys, A_W, PAGE)
    a_out_s = _sample_attention(page_table, q_bd, feature_major(cache_k), feature_major(cache_v),
                                pad_new(k_fs), pad_new(v_fs), mask_s, tab_last, tab_new,
                                t_new=t_new, group=ATTN_PAGE_GROUP).reshape(n_dec, A_W)
    gla_chunk = 64
    pad_t = lambda a: jnp.pad(a.reshape(bd, t_new, A_W), ((0, 0), (0, gla_chunk - t_new), (0, 0))).reshape(-1, A_W)
    b_out_s, st_s = _gla(pad_t(gq), pad_t(gk), pad_t(gi), pad_t(gl), pad_t(gg),
                         state_hgrn[0], p['onorm'], batch=bd, chunk=gla_chunk)
    b_out_s = b_out_s.reshape(bd, gla_chunk, A_W)[:, :t_new].reshape(n_dec, A_W)
    xs = _even_tail(xs, a_out_s, b_out_s, g1, sc2, sh2, g2, p['ng0b'], p['wa'], p['wb'],
                    p['w1'], p['w3'], p['w2'], tm=n_dec)
    step_major = lambda a: a.reshape(bd, t_new, d).transpose(1, 0, 2).reshape(n_dec, d)
    xs, fre_s, fim_s = _odd_layer(step_major(xs), mods(1, slice(bp, bp + bd), lambda a: jnp.tile(a, (t_new, 1))), p,
                                  _to_lane_groups(state_s5_re[0].reshape(bd, S5_W)),
                                  _to_lane_groups(state_s5_im[0].reshape(bd, S5_W)), r=bd, seg_len=None)
    y_sample = xs.reshape(t_new, bd, d).transpose(1, 0, 2)

    heads = lambda a, b, t: a.reshape(1, b, t, N_HEADS, HEAD_DIM)
    s5_state = lambda f: f.reshape(1, -1, S5_GROUPS, S5_STATE)
    return (y_prompt, y_sample,
            heads(k_f, bp, seq), heads(v_f, bp, seq), ki_p.reshape(1, bp, seq, HEAD_DIM),
            st_p[None], s5_state(fre_p[S5_SEGMENTS - 1:]), s5_state(fim_p[S5_SEGMENTS - 1:]),
            heads(k_fs, bd, t_new), heads(v_fs, bd, t_new), ki_s.reshape(1, bd, t_new, HEAD_DIM),
            st_s[None], s5_state(fre_s), s5_state(fim_s))
```
